```python
import math
import jax, jax.numpy as jnp
from jax import lax
import numpy as np

D_MODEL = 1024
BATCH = 8
SEQ = 2048
DEPTH = 4

SSM_WIDTH = D_MODEL // 2
SSM_GROUP = 16
SSM_GROUPS = SSM_WIDTH // SSM_GROUP
SSM_STATE = 64
POOL_WIDTH = D_MODEL // 2
POOL_WINDOWS = (2, 4, 8, 16)
POOL_GROUP = POOL_WIDTH // len(POOL_WINDOWS)
HEAD_DIM = 64
N_HEADS = (D_MODEL // 2) // HEAD_DIM
N_KV_HEADS = 2
GQA_GROUP = N_HEADS // N_KV_HEADS
ATTN_WIDTH = N_HEADS * HEAD_DIM
KV_WIDTH = N_KV_HEADS * HEAD_DIM
BRANCH_WIDTH = D_MODEL // 2
N_BRANCH = 3
CMP_BLOCK = 32
CMP_STRIDE = 16
CMP_HIDDEN = 256
SEL_BLOCK = 64
SEL_TOP = 16
WINDOW = 512
Q_BLOCK = 64
ROPE_THETA = 10000.0
FF_HIDDEN = -(-8 * D_MODEL // (3 * 256)) * 256
DEEPNORM_ALPHA = (2 * DEPTH) ** 0.25
DEEPNORM_BETA = (8 * DEPTH) ** -0.25
LN_EPS = 1e-5
NEG = -1e30
IN_SIZES = (SSM_WIDTH, POOL_WIDTH, ATTN_WIDTH) + (KV_WIDTH,) * 6 + (3 * N_HEADS, N_BRANCH * D_MODEL)
IN_WIDTH = sum(IN_SIZES)
IN_OFFSETS = tuple(sum(IN_SIZES[:i + 1]) for i in range(len(IN_SIZES) - 1))

kernel_name = 'hybrid_s5_pool_nsa_deepnorm'


def layer_norm(x, g, b):
    xf = x.astype(jnp.float32)
    mu = jnp.mean(xf, -1, keepdims=True)
    var = jnp.mean(jnp.square(xf - mu), -1, keepdims=True)
    return ((xf - mu) * lax.rsqrt(var + LN_EPS) * g + b).astype(x.dtype)


def rope(x, positions):
    inv = ROPE_THETA ** (-jnp.arange(0, HEAD_DIM, 2, dtype=jnp.float32) / HEAD_DIM)
    ang = positions.astype(jnp.float32)[..., None] * inv
    cos = jnp.cos(ang)[:, :, None, :]
    sin = jnp.sin(ang)[:, :, None, :]
    x1, x2 = jnp.split(x.astype(jnp.float32), 2, axis=-1)
    return jnp.concatenate([x1 * cos - x2 * sin, x1 * sin + x2 * cos], -1).astype(x.dtype)


def masked_softmax(s, mask):
    s = jnp.where(mask, s.astype(jnp.float32), NEG)
    return jax.nn.softmax(s, axis=-1) * mask


def s5_mixer(u, a_re, a_im, log_dt, b_re, b_im, c_re, c_im, d_skip, w_glu):
    bsz, s, _ = u.shape
    f32 = jnp.float32
    uf = u.astype(f32).reshape(bsz, s, SSM_GROUPS, SSM_GROUP)
    lam = lax.complex(a_re.astype(f32), a_im.astype(f32))
    dt = jnp.exp(log_dt.astype(f32))[:, None]
    lam_bar = jnp.exp(lam * dt)
    b_bar = ((lam_bar - 1.0) / lam)[..., None] * lax.complex(b_re.astype(f32), b_im.astype(f32))
    bu = jnp.einsum('bsgc,gpc->bsgp', uf, b_bar)
    a = jnp.broadcast_to(lam_bar, bu.shape)

    def combine(left, right):
        a1, b1 = left
        a2, b2 = right
        return a1 * a2, a2 * b1 + b2

    _, h = lax.associative_scan(combine, (a, bu), axis=1)
    c = lax.complex(c_re.astype(f32), c_im.astype(f32))
    y = jnp.einsum('bsgp,gcp->bsgc', h, c).real + d_skip.astype(f32).reshape(SSM_GROUPS, SSM_GROUP) * uf
    y = jax.nn.gelu(y.reshape(bsz, s, SSM_WIDTH)).astype(u.dtype)
    val, gate = jnp.split(y @ w_glu, 2, axis=-1)
    return val * jax.nn.sigmoid(gate)


def pool_mixer(u, w_pool, pool_scale):
    bsz, s, _ = u.shape
    uf = u.astype(jnp.float32)
    csum = jnp.pad(jnp.cumsum(uf, axis=1), ((0, 0), (1, 0), (0, 0)))
    t = jnp.arange(s)
    outs = []
    for gi, w in enumerate(POOL_WINDOWS):
        sl = slice(gi * POOL_GROUP, (gi + 1) * POOL_GROUP)
        lo = jnp.maximum(t + 1 - w, 0)
        cnt = (t + 1 - lo).astype(jnp.float32)[None, :, None]
        mean = (csum[:, 1:, sl] - csum[:, lo, sl]) / cnt
        outs.append(mean - uf[:, :, sl])
    pooled = jnp.stack(outs, axis=2).astype(u.dtype)
    mixed = jnp.einsum('bsgc,gcd->bsgd', pooled, w_pool).reshape(bsz, s, POOL_WIDTH)
    return mixed * pool_scale


def nsa_mixer(q, k_cmp, v_cmp, k_sel, v_sel, k_win, v_win, gates, positions, cmp_pos, cmp_w1, cmp_b1, cmp_w2):
    bsz, s, _ = q.shape
    f32 = jnp.float32
    scale = HEAD_DIM ** -0.5
    t = jnp.arange(s)
    heads = lambda z: z.reshape(bsz, s, N_KV_HEADS, HEAD_DIM)
    q = rope(q.reshape(bsz, s, N_HEADS, HEAD_DIM), positions)
    qg = q.reshape(bsz, s, N_KV_HEADS, GQA_GROUP, HEAD_DIM)
    k_cmp = rope(heads(k_cmp), positions)
    v_cmp = heads(v_cmp)
    k_sel = rope(heads(k_sel), positions)
    v_sel = heads(v_sel)
    k_win = rope(heads(k_win), positions)
    v_win = heads(v_win)

    n_cmp = (s - CMP_BLOCK) // CMP_STRIDE + 1
    starts = jnp.arange(n_cmp) * CMP_STRIDE
    blk_idx = starts[:, None] + jnp.arange(CMP_BLOCK)[None, :]

    def compress(z, j):
        zb = z[:, blk_idx] + cmp_pos[j][None, None, :, None, :]
        zb = zb.transpose(0, 1, 3, 2, 4).reshape(bsz, n_cmp, N_KV_HEADS, CMP_BLOCK * HEAD_DIM)
        return jax.nn.gelu(zb @ cmp_w1[j] + cmp_b1[j]) @ cmp_w2[j]

    kc = compress(k_cmp, 0)
    vc = compress(v_cmp, 1)
    s_cmp = jnp.einsum('bshgd,bnhd->bhgsn', qg, kc, preferred_element_type=f32) * scale
    cmp_mask = (starts + CMP_BLOCK - 1)[None, :] <= t[:, None]
    p_cmp = masked_softmax(s_cmp, cmp_mask)
    o_cmp = jnp.einsum('bhgsn,bnhd->bshgd', p_cmp.astype(vc.dtype), vc)

    n_sel = s // SEL_BLOCK
    n_top = min(SEL_TOP, n_sel)
    sel_starts = jnp.arange(n_sel) * SEL_BLOCK
    overlap = jnp.clip(jnp.minimum(starts[:, None] + CMP_BLOCK, sel_starts[None, :] + SEL_BLOCK)
                       - jnp.maximum(starts[:, None], sel_starts[None, :]), 0).astype(f32) / CMP_BLOCK
    imp = jnp.einsum('bhgsn,nj->bhsj', p_cmp, overlap)
    cur = t // SEL_BLOCK
    jb = jnp.arange(n_sel)[None, :]
    forced = (jb == 0) | (jb == cur[:, None]) | (jb == cur[:, None] - 1)
    causal = sel_starts[None, :] <= t[:, None]
    score = jnp.where(forced, 1e30, jnp.where(causal, imp, NEG))
    top_val, top_idx = lax.top_k(score, n_top)
    top_ok = top_val > -1e29

    ks_blocks = k_sel.reshape(bsz, n_sel, SEL_BLOCK, N_KV_HEADS, HEAD_DIM).transpose(0, 3, 1, 2, 4)
    vs_blocks = v_sel.reshape(bsz, n_sel, SEL_BLOCK, N_KV_HEADS, HEAD_DIM).transpose(0, 3, 1, 2, 4)
    k_pad = jnp.pad(k_win, ((0, 0), (WINDOW, 0), (0, 0), (0, 0)))
    v_pad = jnp.pad(v_win, ((0, 0), (WINDOW, 0), (0, 0), (0, 0)))
    b_ix = jnp.arange(bsz)[:, None, None, None]
    h_ix = jnp.arange(N_KV_HEADS)[None, :, None, None]
    n_keys = n_top * SEL_BLOCK

    def query_block(i):
        s0 = i * Q_BLOCK
        tq = s0 + jnp.arange(Q_BLOCK)
        qb = lax.dynamic_slice_in_dim(qg, s0, Q_BLOCK, axis=1)
        idx = lax.dynamic_slice_in_dim(top_idx, s0, Q_BLOCK, axis=2)
        ok = lax.dynamic_slice_in_dim(top_ok, s0, Q_BLOCK, axis=2)
        kg = ks_blocks[b_ix, h_ix, idx].reshape(bsz, N_KV_HEADS, Q_BLOCK, n_keys, HEAD_DIM)
        vg = vs_blocks[b_ix, h_ix, idx].reshape(bsz, N_KV_HEADS, Q_BLOCK, n_keys, HEAD_DIM)
        key_pos = idx[..., None] * SEL_BLOCK + jnp.arange(SEL_BLOCK)
        m_sel = (ok[..., None] & (key_pos <= tq[None, None, :, None, None])).reshape(bsz, N_KV_HEADS, 1, Q_BLOCK, n_keys)
        sc = jnp.einsum('bqhgd,bhqkd->bhgqk', qb, kg, preferred_element_type=f32) * scale
        p = masked_softmax(sc, m_sel)
        o_sel = jnp.einsum('bhgqk,bhqkd->bqhgd', p.astype(vg.dtype), vg)
        kw = lax.dynamic_slice_in_dim(k_pad, s0, WINDOW + Q_BLOCK, axis=1)
        vw = lax.dynamic_slice_in_dim(v_pad, s0, WINDOW + Q_BLOCK, axis=1)
        kpos = s0 - WINDOW + jnp.arange(WINDOW + Q_BLOCK)
        diff = tq[:, None] - kpos[None, :]
        m_win = (kpos[None, :] >= 0) & (diff >= 0) & (diff < WINDOW)
        sc = jnp.einsum('bqhgd,bkhd->bhgqk', qb, kw, preferred_element_type=f32) * scale
        p = masked_softmax(sc, m_win)
        o_win = jnp.einsum('bhgqk,bkhd->bqhgd', p.astype(vw.dtype), vw)
        return o_sel, o_win

    o_sel, o_win = lax.map(query_block, jnp.arange(s // Q_BLOCK))
    o_sel = o_sel.transpose(1, 0, 2, 3, 4, 5).reshape(bsz, s, N_KV_HEADS, GQA_GROUP, HEAD_DIM)
    o_win = o_win.transpose(1, 0, 2, 3, 4, 5).reshape(bsz, s, N_KV_HEADS, GQA_GROUP, HEAD_DIM)
    g = jax.nn.sigmoid(gates.reshape(bsz, s, 3, N_KV_HEADS, GQA_GROUP, 1))
    out = g[:, :, 0] * o_cmp + g[:, :, 1] * o_sel + g[:, :, 2] * o_win
    return out.reshape(bsz, s, ATTN_WIDTH)


def setup_inputs(seed: int = 0) -> dict:
    key = jax.random.key(seed)
    ks = jax.random.split(key, 24)
    f32 = jnp.float32
    nrm = lambda k, shape, fan_in: jax.random.normal(k, shape, f32) * fan_in ** -0.5
    x = jax.random.normal(ks[0], (BATCH, SEQ, D_MODEL), f32)
    offsets = jax.random.randint(ks[1], (BATCH, 1), 0, 4096, dtype=jnp.int32)
    positions = offsets + jnp.arange(SEQ, dtype=jnp.int32)[None, :]
    w_in = nrm(ks[2], (DEPTH, D_MODEL, IN_WIDTH), D_MODEL)
    ssm_a_re = -0.5 + 0.01 * jax.random.normal(ks[3], (DEPTH, SSM_GROUPS, SSM_STATE), f32)
    ssm_a_im = math.pi * jnp.arange(SSM_STATE, dtype=f32) + 0.01 * jax.random.normal(ks[4], (DEPTH, SSM_GROUPS, SSM_STATE), f32)
    ssm_log_dt = jax.random.uniform(ks[5], (DEPTH, SSM_GROUPS), f32, math.log(1e-3), math.log(1e-1))
    ssm_b_re = nrm(ks[6], (DEPTH, SSM_GROUPS, SSM_STATE, SSM_GROUP), 2 * SSM_GROUP)
    ssm_b_im = nrm(ks[7], (DEPTH, SSM_GROUPS, SSM_STATE, SSM_GROUP), 2 * SSM_GROUP)
    ssm_c_re = nrm(ks[8], (DEPTH, SSM_GROUPS, SSM_GROUP, SSM_STATE), SSM_STATE)
    ssm_c_im = nrm(ks[9], (DEPTH, SSM_GROUPS, SSM_GROUP, SSM_STATE), SSM_STATE)
    ssm_d = jax.random.normal(ks[10], (DEPTH, SSM_WIDTH), f32)
    ssm_w_glu = nrm(ks[11], (DEPTH, SSM_WIDTH, 2 * SSM_WIDTH), SSM_WIDTH)
    pool_w = nrm(ks[12], (DEPTH, len(POOL_WINDOWS), POOL_GROUP, POOL_GROUP), POOL_GROUP)
    pool_scale = 1.0 + 0.02 * jax.random.normal(ks[13], (DEPTH, POOL_WIDTH), f32)
    cmp_pos = 0.02 * jax.random.normal(ks[14], (DEPTH, 2, CMP_BLOCK, HEAD_DIM), f32)
    cmp_w1 = nrm(ks[15], (DEPTH, 2, CMP_BLOCK * HEAD_DIM, CMP_HIDDEN), CMP_BLOCK * HEAD_DIM)
    cmp_b1 = 0.01 * jax.random.normal(ks[16], (DEPTH, 2, CMP_HIDDEN), f32)
    cmp_w2 = nrm(ks[17], (DEPTH, 2, CMP_HIDDEN, HEAD_DIM), CMP_HIDDEN)
    w_branch = nrm(ks[18], (DEPTH, N_BRANCH, BRANCH_WIDTH, D_MODEL), BRANCH_WIDTH)
    w_out = nrm(ks[19], (DEPTH, D_MODEL, D_MODEL), D_MODEL) * DEEPNORM_BETA
    ln_g = 1.0 + 0.02 * jax.random.normal(ks[20], (DEPTH, 2, D_MODEL), f32)
    ln_b = 0.02 * jax.random.normal(ks[21], (DEPTH, 2, D_MODEL), f32)
    ffn_w_in = nrm(ks[22], (DEPTH, D_MODEL, 2 * FF_HIDDEN), D_MODEL)
    ffn_w_out = nrm(ks[23], (DEPTH, FF_HIDDEN, D_MODEL), FF_HIDDEN) * DEEPNORM_BETA
    return {'x': x, 'positions': positions, 'w_in': w_in,
            'ssm_a_re': ssm_a_re, 'ssm_a_im': ssm_a_im, 'ssm_log_dt': ssm_log_dt,
            'ssm_b_re': ssm_b_re, 'ssm_b_im': ssm_b_im, 'ssm_c_re': ssm_c_re, 'ssm_c_im': ssm_c_im,
            'ssm_d': ssm_d, 'ssm_w_glu': ssm_w_glu, 'pool_w': pool_w, 'pool_scale': pool_scale,
            'cmp_pos': cmp_pos, 'cmp_w1': cmp_w1, 'cmp_b1': cmp_b1, 'cmp_w2': cmp_w2,
            'w_branch': w_branch, 'w_out': w_out, 'ln_g': ln_g, 'ln_b': ln_b,
            'ffn_w_in': ffn_w_in, 'ffn_w_out': ffn_w_out}


def reference(x, positions, w_in, ssm_a_re, ssm_a_im, ssm_log_dt, ssm_b_re, ssm_b_im, ssm_c_re, ssm_c_im,
              ssm_d, ssm_w_glu, pool_w, pool_scale, cmp_pos, cmp_w1, cmp_b1, cmp_w2,
              w_branch, w_out, ln_g, ln_b, ffn_w_in, ffn_w_out):
    bsz, s, _ = x.shape
    for l in range(DEPTH):
        proj = x @ w_in[l]
        (u_ssm, u_pool, q, k_c, v_c, k_s, v_s, k_w, v_w, nsa_g, br_g) = jnp.split(proj, IN_OFFSETS, axis=-1)
        y_ssm = s5_mixer(u_ssm, ssm_a_re[l], ssm_a_im[l], ssm_log_dt[l], ssm_b_re[l], ssm_b_im[l],
                         ssm_c_re[l], ssm_c_im[l], ssm_d[l], ssm_w_glu[l])
        y_pool = pool_mixer(u_pool, pool_w[l], pool_scale[l])
        y_nsa = nsa_mixer(q, k_c, v_c, k_s, v_s, k_w, v_w, nsa_g, positions,
                          cmp_pos[l], cmp_w1[l], cmp_b1[l], cmp_w2[l])
        branches = jnp.stack([y_ssm, y_pool, y_nsa], axis=2)
        branch_d = jnp.einsum('bskc,kcd->bskd', branches, w_branch[l])
        gate = jax.nn.sigmoid(br_g.reshape(bsz, s, N_BRANCH, D_MODEL))
        merged = jnp.sum(gate * branch_d, axis=2)
        x = layer_norm(DEEPNORM_ALPHA * x + merged @ w_out[l], ln_g[l, 0], ln_b[l, 0])
        h_gate, h_up = jnp.split(x @ ffn_w_in[l], 2, axis=-1)
        x = layer_norm(DEEPNORM_ALPHA * x + (jax.nn.silu(h_gate) * h_up) @ ffn_w_out[l], ln_g[l, 1], ln_b[l, 1])
    return x
```

```python
import functools
import math

import jax
import jax.numpy as jnp
from jax import lax
from jax.experimental import pallas as pl
from jax.experimental.pallas import tpu as pltpu

F32 = jnp.float32
BF16 = jnp.bfloat16

D_MODEL = 1024
SSM_WIDTH = 512
SSM_GROUP = 16
SSM_GROUPS = 32
SSM_STATE = 64
POOL_WIDTH = 512
POOL_WINDOWS = (2, 4, 8, 16)
POOL_GROUP = 128
HEAD_DIM = 64
N_HEADS = 8
N_KV_HEADS = 2
GQA_GROUP = 4
ATTN_WIDTH = 512
KV_WIDTH = 128
N_BRANCH = 3
CMP_BLOCK = 32
CMP_STRIDE = 16
CMP_HIDDEN = 256
SEL_BLOCK = 64
SEL_TOP = 16
WINDOW = 512
ROPE_THETA = 10000.0
FF_HIDDEN = 2816
LN_EPS = 1e-5
NEG = -1e30
N_GATE = 3 * N_HEADS
IN_RAW = 3 * 512 + 6 * KV_WIDTH + N_GATE + N_BRANCH * D_MODEL

LANES = 128
SUBLANES = 8
VMEM_LIMIT_BYTES = 56 * 1024 * 1024

COL_SSM = 0
COL_POOL = 512
COL_Q = 1024
COL_KV = 1536
COL_NG = 2304
COL_BG = COL_NG + LANES
IN_PACKED = COL_BG + N_BRANCH * D_MODEL

ROW_TILE = 512
S5_STEPS = 64
S5_LANE_CHUNK = 512
POOL_STEPS = 128
POOL_HALO = 16
Q_TILE = 128
K_TILE = 256
FF_CHUNK = 256


def _cparams(n_axes):
    return pltpu.CompilerParams(dimension_semantics=("arbitrary",) * n_axes,
                                vmem_limit_bytes=VMEM_LIMIT_BYTES)


def _resident(shape):
    nd = len(shape)
    return pl.BlockSpec(shape, lambda *_: (0,) * nd, pipeline_mode=pl.Buffered(1))


def _gelu_tanh(x):
    return x * (0.5 * (1.0 + jnp.tanh(math.sqrt(2.0 / math.pi) * (x + 0.044715 * (x * x * x)))))


def _layer_norm(r, g, b):
    mu = jnp.mean(r, axis=-1, keepdims=True)
    c = r - mu
    var = jnp.mean(c * c, axis=-1, keepdims=True)
    return c * lax.rsqrt(var + LN_EPS) * g + b


def _dot(a, b):
    return jnp.dot(a, b, preferred_element_type=F32)


def _dot_nt(a, b):
    return lax.dot_general(a, b, (((1,), (1,)), ((), ())), preferred_element_type=F32)


def _rope_table_kernel(pos_ref, inv_ref, cos_ref, sin_ref):
    ang = pos_ref[...] * inv_ref[...]
    lane = lax.broadcasted_iota(jnp.int32, ang.shape, 1)
    first_half = jnp.bitwise_and(lane, HEAD_DIM - 1) < HEAD_DIM // 2
    cos_ref[...] = jnp.cos(ang)
    sin_ref[...] = jnp.where(first_half, -jnp.sin(ang), jnp.sin(ang))


def _rope_tables(positions):
    rows = positions.size
    inv = ROPE_THETA ** (-jnp.arange(0, HEAD_DIM, 2, dtype=F32) / HEAD_DIM)
    inv = jnp.tile(inv, LANES // (HEAD_DIM // 2)).reshape(1, LANES)
    pos = positions.astype(F32).reshape(rows, 1)
    tile = ROW_TILE
    return pl.pallas_call(
        _rope_table_kernel,
        grid=(rows // tile,),
        in_specs=[pl.BlockSpec((tile, 1), lambda i: (i, 0)),
                  pl.BlockSpec((1, LANES), lambda i: (0, 0))],
        out_specs=[pl.BlockSpec((tile, LANES), lambda i: (i, 0))] * 2,
        out_shape=[jax.ShapeDtypeStruct((rows, LANES), F32)] * 2,
        compiler_params=_cparams(1),
        name="rope_tables",
    )(pos, inv)


def _inproj_kernel(x_ref, w_ref, cos_ref, sin_ref,
                   ussm_ref, upool_ref, q_ref, cmp_ref, ksel_ref, vsel_ref, kwin_ref, vwin_ref,
                   ng_ref, bg_ref):
    xb = x_ref[...].astype(BF16)
    cos = cos_ref[...]
    sin = sin_ref[...]
    lane = lax.broadcasted_iota(jnp.int32, cos.shape, 1)
    first_half = jnp.bitwise_and(lane, HEAD_DIM - 1) < HEAD_DIM // 2
    left = lane < HEAD_DIM

    def proj(c0, width):
        return _dot(xb, w_ref[:, c0:c0 + width])

    def rope(t):
        swapped = jnp.where(first_half, pltpu.roll(t, LANES - HEAD_DIM // 2, 1),
                            pltpu.roll(t, HEAD_DIM // 2, 1))
        return t * cos + swapped * sin

    def dup(t):
        r = pltpu.roll(t, HEAD_DIM, 1)
        return jnp.where(left, t, r), jnp.where(left, r, t)

    ussm_ref[...] = proj(COL_SSM, SSM_WIDTH).astype(BF16)
    upool_ref[...] = proj(COL_POOL, POOL_WIDTH).astype(BF16)
    scale = HEAD_DIM ** -0.5
    for j in range(ATTN_WIDTH // LANES):
        t = rope(proj(COL_Q + j * LANES, LANES))
        q_ref[:, j * LANES:(j + 1) * LANES] = (t * scale).astype(BF16)

    kv = proj(COL_KV, 6 * KV_WIDTH)
    for j, roped in enumerate((True, False)):
        t = kv[:, j * LANES:(j + 1) * LANES]
        a, b = dup(rope(t) if roped else t)
        cmp_ref[j, 0, 0] = a[:, :HEAD_DIM].astype(BF16)
        cmp_ref[j, 0, 1] = b[:, :HEAD_DIM].astype(BF16)
    for j, (ref, roped) in enumerate(((ksel_ref, True), (vsel_ref, False), (kwin_ref, True), (vwin_ref, False))):
        t = kv[:, (2 + j) * LANES:(3 + j) * LANES]
        a, b = dup(rope(t) if roped else t)
        ref[0, :, 0:LANES] = a.astype(BF16)
        ref[0, :, LANES:2 * LANES] = b.astype(BF16)

    ng_ref[...] = jax.nn.sigmoid(proj(COL_NG, LANES))
    for k in range(N_BRANCH):
        bg_ref[:, k * D_MODEL:(k + 1) * D_MODEL] = jax.nn.sigmoid(proj(COL_BG + k * D_MODEL, D_MODEL)).astype(BF16)


def _inproj(xr, w, cos, sin, bsz, seq):
    rows = bsz * seq
    tile = ROW_TILE
    n_s = seq // tile
    row = lambda i: (i, 0)
    tmajor = lambda i: (i % n_s, i // n_s)
    dup_spec = pl.BlockSpec((1, tile, 2 * LANES), lambda i: (i // n_s, i % n_s, 0))
    out_shape = [
        jax.ShapeDtypeStruct((seq, bsz * SSM_WIDTH), BF16),
        jax.ShapeDtypeStruct((seq, bsz * POOL_WIDTH), BF16),
        jax.ShapeDtypeStruct((rows, ATTN_WIDTH), BF16),
        jax.ShapeDtypeStruct((2, bsz, N_KV_HEADS, seq, HEAD_DIM), BF16),
        jax.ShapeDtypeStruct((bsz, seq, 2 * LANES), BF16),
        jax.ShapeDtypeStruct((bsz, seq, 2 * LANES), BF16),
        jax.ShapeDtypeStruct((bsz, seq, 2 * LANES), BF16),
        jax.ShapeDtypeStruct((bsz, seq, 2 * LANES), BF16),
        jax.ShapeDtypeStruct((rows, LANES), F32),
        jax.ShapeDtypeStruct((rows, N_BRANCH * D_MODEL), BF16),
    ]
    out_specs = [
        pl.BlockSpec((tile, SSM_WIDTH), tmajor),
        pl.BlockSpec((tile, POOL_WIDTH), tmajor),
        pl.BlockSpec((tile, ATTN_WIDTH), row),
        pl.BlockSpec((2, 1, N_KV_HEADS, tile, HEAD_DIM), lambda i: (0, i // n_s, 0, i % n_s, 0)),
        dup_spec, dup_spec, dup_spec, dup_spec,
        pl.BlockSpec((tile, LANES), row),
        pl.BlockSpec((tile, N_BRANCH * D_MODEL), row),
    ]
    return pl.pallas_call(
        _inproj_kernel,
        grid=(rows // tile,),
        in_specs=[pl.BlockSpec((tile, D_MODEL), row), _resident(w.shape),
                  pl.BlockSpec((tile, LANES), row), pl.BlockSpec((tile, LANES), row)],
        out_specs=out_specs,
        out_shape=out_shape,
        compiler_params=_cparams(1),
        name="in_projection",
    )(xr, w, cos, sin)


def _s5_kernel(u_ref, lre_ref, lim_ref, wb_ref, wc_ref, d_ref, wglu_ref, o_ref, bu_ref, st_ref, y_ref):
    n_rows = u_ref.shape[0]
    n_steps = n_rows // SUBLANES
    n_chunk = wb_ref.shape[0]
    cw = S5_LANE_CHUNK

    @pl.when(pl.program_id(0) == 0)
    def _():
        st_ref[...] = jnp.zeros(st_ref.shape, F32)

    u = u_ref[...]
    for c in range(n_chunk):
        bu_ref[:, c * 2 * cw:(c + 1) * 2 * cw] = _dot(u[:, c * LANES:(c + 1) * LANES], wb_ref[c])

    for c in range(n_chunk):
        re0, im0 = c * 2 * cw, c * 2 * cw + cw
        lre = lre_ref[:, c * cw:(c + 1) * cw]
        lim = lim_ref[:, c * cw:(c + 1) * cw]

        def step(t, carry, re0=re0, im0=im0, lre=lre, lim=lim):
            hre, him = carry
            r0 = pl.multiple_of(t * SUBLANES, SUBLANES)
            nre = lre * hre - lim * him + bu_ref[pl.ds(r0, SUBLANES), re0:re0 + cw]
            nim = lre * him + lim * hre + bu_ref[pl.ds(r0, SUBLANES), im0:im0 + cw]
            bu_ref[pl.ds(r0, SUBLANES), re0:re0 + cw] = nre
            bu_ref[pl.ds(r0, SUBLANES), im0:im0 + cw] = nim
            return nre, nim

        hre, him = lax.fori_loop(0, n_steps, step,
                                 (st_ref[0, :, c * cw:(c + 1) * cw], st_ref[1, :, c * cw:(c + 1) * cw]),
                                 unroll=2)
        st_ref[0, :, c * cw:(c + 1) * cw] = hre
        st_ref[1, :, c * cw:(c + 1) * cw] = him

    for c in range(n_chunk):
        y_ref[:, c * LANES:(c + 1) * LANES] = _dot(bu_ref[:, c * 2 * cw:(c + 1) * 2 * cw].astype(BF16), wc_ref[c])
    y = y_ref[...] + d_ref[...] * u.astype(F32)
    z = _dot(_gelu_tanh(y).astype(BF16), wglu_ref[...])
    o_ref[...] = (z[:, :SSM_WIDTH] * jax.nn.sigmoid(z[:, SSM_WIDTH:])).astype(BF16)


def _s5(u_t, lre, lim, wb, wc, d, wglu):
    rows = u_t.shape[0]
    tile = S5_STEPS * SUBLANES
    state_w = lre.shape[1]
    return pl.pallas_call(
        _s5_kernel,
        grid=(rows // tile,),
        in_specs=[pl.BlockSpec((tile, SSM_WIDTH), lambda i: (i, 0)),
                  _resident(lre.shape), _resident(lim.shape), _resident(wb.shape), _resident(wc.shape),
                  _resident(d.shape), _resident(wglu.shape)],
        out_specs=pl.BlockSpec((tile, SSM_WIDTH), lambda i: (i, 0)),
        out_shape=jax.ShapeDtypeStruct((rows, SSM_WIDTH), BF16),
        scratch_shapes=[pltpu.VMEM((tile, 2 * state_w), F32),
                        pltpu.VMEM((2, SUBLANES, state_w), F32),
                        pltpu.VMEM((tile, SSM_WIDTH), F32)],
        compiler_params=_cparams(1),
        name="s5_mixer",
    )(u_t, lre, lim, wb, wc, d, wglu)


def _pool_kernel(u_ref, w_ref, sc_ref, o_ref, ext_ref):
    n_rows = u_ref.shape[0]
    halo = POOL_HALO * SUBLANES
    i = pl.program_id(0)

    @pl.when(i == 0)
    def _():
        ext_ref[0:halo, :] = jnp.zeros((halo, POOL_WIDTH), F32)

    ext_ref[halo:halo + n_rows, :] = u_ref[...].astype(F32)
    row = lax.broadcasted_iota(jnp.int32, (n_rows, POOL_GROUP), 0)
    t = i * (n_rows // SUBLANES) + jnp.right_shift(row, 3)
    for gi, w in enumerate(POOL_WINDOWS):
        c0 = gi * POOL_GROUP
        cur = ext_ref[halo:halo + n_rows, c0:c0 + POOL_GROUP]
        acc = cur
        for k in range(1, w):
            acc = acc + ext_ref[halo - k * SUBLANES:halo - k * SUBLANES + n_rows, c0:c0 + POOL_GROUP]
        cnt = jnp.minimum(t + 1, w).astype(F32)
        pooled = acc / cnt - cur
        mixed = _dot(pooled.astype(BF16), w_ref[gi]) * sc_ref[:, c0:c0 + POOL_GROUP]
        o_ref[:, c0:c0 + POOL_GROUP] = mixed.astype(BF16)
    ext_ref[0:halo, :] = ext_ref[n_rows:n_rows + halo, :]


def _pool(u_t, w, sc):
    rows = u_t.shape[0]
    tile = POOL_STEPS * SUBLANES
    return pl.pallas_call(
        _pool_kernel,
        grid=(rows // tile,),
        in_specs=[pl.BlockSpec((tile, POOL_WIDTH), lambda i: (i, 0)), _resident(w.shape), _resident(sc.shape)],
        out_specs=pl.BlockSpec((tile, POOL_WIDTH), lambda i: (i, 0)),
        out_shape=jax.ShapeDtypeStruct((rows, POOL_WIDTH), BF16),
        scratch_shapes=[pltpu.VMEM((tile + POOL_HALO * SUBLANES, POOL_WIDTH), F32)],
        compiler_params=_cparams(1),
        name="pool_mixer",
    )(u_t, w, sc)


def _compress_kernel(z_ref, w1_ref, pos_ref, b1_ref, w2_ref, o_ref):
    half = CMP_STRIDE * HEAD_DIM
    w1 = w1_ref[0]
    ab = _dot(z_ref[0, 0], w1)
    n_chunks = ab.shape[0]
    top = ab[:, :CMP_HIDDEN]
    bottom_next = pltpu.roll(ab[:, CMP_HIDDEN:], n_chunks - 1, 0)
    pos = pos_ref[0].astype(BF16)
    cst = _dot(pos[:, :half], w1[:, :CMP_HIDDEN]) + _dot(pos[:, half:], w1[:, CMP_HIDDEN:])
    hid = _gelu_tanh(top + bottom_next + cst[0:1, :] + b1_ref[0])
    o_ref[0, 0] = _dot(hid.astype(BF16), w2_ref[0]).astype(BF16)


def _compress(z, w1cat, pos, b1, w2dup):
    _, n_bh, n_chunks, width = z.shape
    return pl.pallas_call(
        _compress_kernel,
        grid=(2, n_bh),
        in_specs=[pl.BlockSpec((1, 1, n_chunks, width), lambda j, r: (j, r, 0, 0)),
                  pl.BlockSpec((1,) + w1cat.shape[1:], lambda j, r: (j, 0, 0)),
                  pl.BlockSpec((1,) + pos.shape[1:], lambda j, r: (j, 0, 0)),
                  pl.BlockSpec((1,) + b1.shape[1:], lambda j, r: (j, 0, 0)),
                  pl.BlockSpec((1,) + w2dup.shape[1:], lambda j, r: (j, 0, 0))],
        out_specs=pl.BlockSpec((1, 1, n_chunks, LANES), lambda j, r: (j, r, 0, 0)),
        out_shape=jax.ShapeDtypeStruct((2, n_bh, n_chunks, LANES), BF16),
        compiler_params=_cparams(2),
        name="compress_mlp",
    )(z, w1cat, pos, b1, w2dup)


def _nsa_kernel(q_ref, kc_ref, vc_ref, ks_ref, vs_ref, kw_ref, vw_ref, g_ref, o_ref, *, seq):
    tq = q_ref.shape[0]
    n_sel = seq // SEL_BLOCK
    n_top = min(SEL_TOP, n_sel)
    n_cmp = (seq - CMP_BLOCK) // CMP_STRIDE + 1
    head = pl.program_id(1)
    t0 = pl.program_id(2) * tq

    lane = lax.broadcasted_iota(jnp.int32, (tq, LANES), 1)
    left = lane < HEAD_DIM
    zero = jnp.zeros((tq, LANES), BF16)
    pairs = (q_ref[:, 0:LANES], q_ref[:, LANES:2 * LANES])
    q4 = jnp.concatenate([jnp.where(left, pairs[0], zero), jnp.where(left, zero, pairs[0]),
                          jnp.where(left, pairs[1], zero), jnp.where(left, zero, pairs[1])], axis=0)
    row = lax.broadcasted_iota(jnp.int32, (4 * tq, 1), 0)
    tpos = t0 + jnp.bitwise_and(row, tq - 1)

    def softmax_rows(s, valid):
        s = jnp.where(valid, s, NEG)
        m = jnp.max(s, axis=-1, keepdims=True)
        e = jnp.where(valid, jnp.exp(s - m), 0.0)
        l = jnp.sum(e, axis=-1, keepdims=True)
        return e, jnp.where(l > 0.0, l, 1.0)

    n_idx = lax.broadcasted_iota(jnp.int32, (1, kc_ref.shape[2]), 1)
    cmp_valid = (n_idx * CMP_STRIDE + (CMP_BLOCK - 1) <= tpos) & (n_idx < n_cmp)
    e, l = softmax_rows(_dot_nt(q4, kc_ref[0, 0]), cmp_valid)
    p_cmp = e / l
    o_cmp = _dot(p_cmp.astype(BF16), vc_ref[0, 0])

    p_sum = p_cmp[0:tq] + p_cmp[tq:2 * tq] + p_cmp[2 * tq:3 * tq] + p_cmp[3 * tq:4 * tq]
    oj = lax.broadcasted_iota(jnp.int32, (n_sel, kc_ref.shape[2]), 0)
    on = lax.broadcasted_iota(jnp.int32, (n_sel, kc_ref.shape[2]), 1)
    overlap = jnp.clip(jnp.minimum(on * CMP_STRIDE + CMP_BLOCK, oj * SEL_BLOCK + SEL_BLOCK)
                       - jnp.maximum(on * CMP_STRIDE, oj * SEL_BLOCK), 0, CMP_BLOCK).astype(F32) * (1.0 / CMP_BLOCK)
    overlap = jnp.where(on < n_cmp, overlap, 0.0)
    imp = lax.dot_general(overlap, p_sum, (((1,), (1,)), ((), ())), precision=lax.Precision.HIGHEST,
                          preferred_element_type=F32)
    jb = lax.broadcasted_iota(jnp.int32, (n_sel, tq), 0)
    tt = t0 + lax.broadcasted_iota(jnp.int32, (n_sel, tq), 1)
    cur = jnp.right_shift(tt, 6)
    forced = (jb == 0) | (jb == cur) | (jb == cur - 1)
    causal = jb * SEL_BLOCK <= tt
    score = jnp.where(forced, 1e30, jnp.where(causal, imp, NEG))
    rank = jnp.zeros((n_sel, tq), F32)
    for a in range(n_sel):
        sa = score[a:a + 1, :]
        ahead = (sa > score) | ((sa == score) & (jb > a))
        rank = rank + jnp.where(ahead, 1.0, 0.0)
    chosen = (rank < float(n_top)) & causal
    sel_bias = jnp.where(chosen, 0.0, NEG).T.astype(BF16)

    tk = K_TILE
    tq_col = t0 + lax.broadcasted_iota(jnp.int32, (tq, 1), 0)

    def sel_step(kt, carry):
        m, l, acc = carry
        k0 = pl.multiple_of(kt * tk, tk)
        s = _dot_nt(q4, ks_ref[0, pl.ds(k0, tk), :])
        kpos = k0 + lax.broadcasted_iota(jnp.int32, (1, tk), 1)
        expand = (lax.broadcasted_iota(jnp.int32, (n_sel, tk), 0)
                  == jnp.right_shift(k0 + lax.broadcasted_iota(jnp.int32, (n_sel, tk), 1), 6))
        bias = _dot(sel_bias, jnp.where(expand, 1.0, 0.0).astype(BF16))
        bias = jnp.where(kpos <= tq_col, bias, NEG)
        s = s + jnp.concatenate([bias] * GQA_GROUP, axis=0)
        m_new = jnp.maximum(m, jnp.max(s, axis=-1, keepdims=True))
        alpha = jnp.exp(m - m_new)
        p = jnp.exp(s - m_new)
        l = alpha * l + jnp.sum(p, axis=-1, keepdims=True)
        acc = alpha * acc + _dot(p.astype(BF16), vs_ref[0, pl.ds(k0, tk), :])
        return m_new, l, acc

    n_kt = (t0 + tq + tk - 1) // tk
    _, l, acc = lax.fori_loop(0, n_kt, sel_step,
                              (jnp.full((4 * tq, 1), NEG, F32), jnp.zeros((4 * tq, 1), F32),
                               jnp.zeros((4 * tq, LANES), F32)))
    o_sel = acc / l

    wk = WINDOW + tq
    w0 = pl.multiple_of(jnp.maximum(t0 - WINDOW, 0), tq)
    kpos = w0 + lax.broadcasted_iota(jnp.int32, (1, wk), 1)
    dist = tpos - kpos
    e, l = softmax_rows(_dot_nt(q4, kw_ref[0, pl.ds(w0, wk), :]), (dist >= 0) & (dist < WINDOW))
    o_win = _dot(e.astype(BF16), vw_ref[0, pl.ds(w0, wk), :]) / l

    g = g_ref[...]
    src = lax.broadcasted_iota(jnp.int32, (LANES, LANES), 0)
    dst_head = jnp.right_shift(lax.broadcasted_iota(jnp.int32, (LANES, LANES), 1), 6)
    for pair in range(GQA_GROUP // 2):
        out = jnp.zeros((tq, LANES), F32)
        for br, o_br in enumerate((o_cmp, o_sel, o_win)):
            col = br * N_HEADS + head * GQA_GROUP + 2 * pair + dst_head
            gate = jnp.dot(g, jnp.where(src == col, 1.0, 0.0), precision=lax.Precision.HIGHEST,
                           preferred_element_type=F32)
            both = jnp.where(left, o_br[2 * pair * tq:(2 * pair + 1) * tq], o_br[(2 * pair + 1) * tq:(2 * pair + 2) * tq])
            out = out + gate * both
        o_ref[:, pair * LANES:(pair + 1) * LANES] = out.astype(BF16)


def _nsa(q, cmp_kv, ksel, vsel, kwin, vwin, ng, bsz, seq):
    tq = Q_TILE
    n_q = seq // tq
    n_chunks = cmp_kv.shape[2]
    qrow = lambda b, h, i: (b * n_q + i, h)
    kv_spec = pl.BlockSpec((1, seq, LANES), lambda b, h, i: (b, 0, h))
    return pl.pallas_call(
        functools.partial(_nsa_kernel, seq=seq),
        grid=(bsz, N_KV_HEADS, n_q),
        in_specs=[pl.BlockSpec((tq, 2 * LANES), qrow),
                  pl.BlockSpec((1, 1, n_chunks, LANES), lambda b, h, i: (0, b * N_KV_HEADS + h, 0, 0)),
                  pl.BlockSpec((1, 1, n_chunks, LANES), lambda b, h, i: (1, b * N_KV_HEADS + h, 0, 0)),
                  kv_spec, kv_spec, kv_spec, kv_spec,
                  pl.BlockSpec((tq, LANES), lambda b, h, i: (b * n_q + i, 0))],
        out_specs=pl.BlockSpec((tq, 2 * LANES), qrow),
        out_shape=jax.ShapeDtypeStruct((bsz * seq, ATTN_WIDTH), BF16),
        compiler_params=_cparams(3),
        name="nsa_attention",
    )(q, cmp_kv, cmp_kv, ksel, vsel, kwin, vwin, ng)


def _merge_kernel(x_ref, ys_ref, yp_ref, yn_ref, bg_ref, wb_ref, wo_ref, g_ref, b_ref, o_ref, *, alpha):
    merged = None
    for k, y_ref in enumerate((ys_ref, yp_ref, yn_ref)):
        term = bg_ref[:, k * D_MODEL:(k + 1) * D_MODEL].astype(F32) * _dot(y_ref[...], wb_ref[k])
        merged = term if merged is None else merged + term
    r = alpha * x_ref[...] + _dot(merged.astype(BF16), wo_ref[...])
    o_ref[...] = _layer_norm(r, g_ref[...], b_ref[...])


def _merge(xr, ys_t, yp_t, yn, bg, wb, wo, g, b, bsz, seq, alpha):
    rows = bsz * seq
    tile = ROW_TILE
    n_s = seq // tile
    row = lambda i: (i, 0)
    tmajor = lambda i: (i % n_s, i // n_s)
    return pl.pallas_call(
        functools.partial(_merge_kernel, alpha=alpha),
        grid=(rows // tile,),
        in_specs=[pl.BlockSpec((tile, D_MODEL), row),
                  pl.BlockSpec((tile, SSM_WIDTH), tmajor), pl.BlockSpec((tile, POOL_WIDTH), tmajor),
                  pl.BlockSpec((tile, ATTN_WIDTH), row), pl.BlockSpec((tile, N_BRANCH * D_MODEL), row),
                  _resident(wb.shape), _resident(wo.shape), _resident(g.shape), _resident(b.shape)],
        out_specs=pl.BlockSpec((tile, D_MODEL), row),
        out_shape=jax.ShapeDtypeStruct((rows, D_MODEL), F32),
        compiler_params=_cparams(1),
        name="branch_merge",
    )(xr, ys_t, yp_t, yn, bg, wb, wo, g, b)


def _ffn_kernel(x_ref, wi_ref, wo_ref, g_ref, b_ref, o_ref, *, alpha):
    x = x_ref[...]
    xb = x.astype(BF16)
    acc = None
    for c in range(FF_HIDDEN // FF_CHUNK):
        c0 = c * FF_CHUNK
        hg = _dot(xb, wi_ref[:, c0:c0 + FF_CHUNK])
        hu = _dot(xb, wi_ref[:, FF_HIDDEN + c0:FF_HIDDEN + c0 + FF_CHUNK])
        act = (hg * jax.nn.sigmoid(hg) * hu).astype(BF16)
        part = _dot(act, wo_ref[c0:c0 + FF_CHUNK, :])
        acc = part if acc is None else acc + part
    o_ref[...] = _layer_norm(alpha * x + acc, g_ref[...], b_ref[...])


def _ffn(xr, wi, wo, g, b, alpha):
    rows = xr.shape[0]
    tile = ROW_TILE
    row = lambda i: (i, 0)
    return pl.pallas_call(
        functools.partial(_ffn_kernel, alpha=alpha),
        grid=(rows // tile,),
        in_specs=[pl.BlockSpec((tile, D_MODEL), row), _resident(wi.shape), _resident(wo.shape),
                  _resident(g.shape), _resident(b.shape)],
        out_specs=pl.BlockSpec((tile, D_MODEL), row),
        out_shape=jax.ShapeDtypeStruct((rows, D_MODEL), F32),
        compiler_params=_cparams(1),
        name="swiglu_ffn",
    )(xr, wi, wo, g, b)


def _pack_s5(a_re, a_im, log_dt, b_re, b_im, c_re, c_im):
    depth = a_re.shape[0]
    lam = lax.complex(a_re.astype(F32), a_im.astype(F32))
    dt = jnp.exp(log_dt.astype(F32))[..., None]
    lam_bar = jnp.exp(lam * dt)
    b_bar = ((lam_bar - 1.0) / lam)[..., None] * lax.complex(b_re.astype(F32), b_im.astype(F32))
    gpc = S5_LANE_CHUNK // SSM_STATE
    n_chunk = SSM_GROUPS // gpc
    eye = jnp.eye(gpc, dtype=F32)

    def b_block(part):
        v = part.transpose(0, 1, 3, 2).reshape(depth, n_chunk, gpc, SSM_GROUP, SSM_STATE)
        return jnp.einsum('xy,dqxcp->dqxcyp', eye, v).reshape(depth, n_chunk, gpc * SSM_GROUP, gpc * SSM_STATE)

    def c_block(part):
        v = part.reshape(depth, n_chunk, gpc, SSM_GROUP, SSM_STATE)
        return jnp.einsum('xy,dqxcp->dqxpyc', eye, v).reshape(depth, n_chunk, gpc * SSM_STATE, gpc * SSM_GROUP)

    wb = jnp.concatenate([b_block(b_bar.real), b_block(b_bar.imag)], axis=-1).astype(BF16)
    wc = jnp.concatenate([c_block(c_re.astype(F32)), c_block(-c_im.astype(F32))], axis=-2).astype(BF16)
    state_w = SSM_GROUPS * SSM_STATE
    lre = jnp.broadcast_to(lam_bar.real.reshape(depth, 1, state_w), (depth, SUBLANES, state_w))
    lim = jnp.broadcast_to(lam_bar.imag.reshape(depth, 1, state_w), (depth, SUBLANES, state_w))
    return lre, lim, wb, wc


def kernel(x, positions, w_in, ssm_a_re, ssm_a_im, ssm_log_dt, ssm_b_re, ssm_b_im, ssm_c_re, ssm_c_im,
           ssm_d, ssm_w_glu, pool_w, pool_scale, cmp_pos, cmp_w1, cmp_b1, cmp_w2,
           w_branch, w_out, ln_g, ln_b, ffn_w_in, ffn_w_out):
    bsz, seq, _ = x.shape
    depth = w_in.shape[0]
    rows = bsz * seq
    assert bsz == SUBLANES and seq % ROW_TILE == 0 and seq >= WINDOW + Q_TILE
    alpha = (2 * depth) ** 0.25

    split = COL_NG + N_GATE
    w_in_p = jnp.concatenate([w_in[:, :, :split], jnp.zeros((depth, D_MODEL, COL_BG - split), w_in.dtype),
                              w_in[:, :, split:]], axis=-1).astype(BF16)
    assert w_in.shape[-1] == IN_RAW and w_in_p.shape[-1] == IN_PACKED
    lre, lim, s5_wb, s5_wc = _pack_s5(ssm_a_re, ssm_a_im, ssm_log_dt, ssm_b_re, ssm_b_im, ssm_c_re, ssm_c_im)
    s5_d = ssm_d.astype(F32).reshape(depth, 1, SSM_WIDTH)
    s5_wglu = ssm_w_glu.astype(BF16)
    pool_wb = pool_w.astype(BF16)
    pool_sc = pool_scale.astype(F32).reshape(depth, 1, POOL_WIDTH)
    half = CMP_STRIDE * HEAD_DIM
    cmp_w1cat = jnp.concatenate([cmp_w1[:, :, :half, :], cmp_w1[:, :, half:, :]], axis=-1).astype(BF16)
    cmp_posr = jnp.broadcast_to(cmp_pos.astype(F32).reshape(depth, 2, 1, CMP_BLOCK * HEAD_DIM),
                                (depth, 2, SUBLANES, CMP_BLOCK * HEAD_DIM))
    cmp_b1r = cmp_b1.astype(F32).reshape(depth, 2, 1, CMP_HIDDEN)
    cmp_w2dup = jnp.concatenate([cmp_w2, cmp_w2], axis=-1).astype(BF16)
    wbr = w_branch.astype(BF16)
    wo = w_out.astype(BF16)
    lng = ln_g.astype(F32).reshape(depth, 2, 1, D_MODEL)
    lnb = ln_b.astype(F32).reshape(depth, 2, 1, D_MODEL)
    ffn_wi = ffn_w_in.astype(BF16)
    ffn_wo = ffn_w_out.astype(BF16)

    cos, sin = _rope_tables(positions)
    xr = x.astype(F32).reshape(rows, D_MODEL)
    for l in range(depth):
        (u_ssm, u_pool, q, cmp_in, ksel, vsel, kwin, vwin, ng, bg) = _inproj(xr, w_in_p[l], cos, sin, bsz, seq)
        y_ssm = _s5(u_ssm.reshape(seq * bsz, SSM_WIDTH), lre[l], lim[l], s5_wb[l], s5_wc[l], s5_d[l], s5_wglu[l])
        y_pool = _pool(u_pool.reshape(seq * bsz, POOL_WIDTH), pool_wb[l], pool_sc[l])
        cmp_z = cmp_in.reshape(2, bsz * N_KV_HEADS, seq // CMP_STRIDE, CMP_STRIDE * HEAD_DIM)
        cmp_kv = _compress(cmp_z, cmp_w1cat[l], cmp_posr[l], cmp_b1r[l], cmp_w2dup[l])
        y_nsa = _nsa(q, cmp_kv, ksel, vsel, kwin, vwin, ng, bsz, seq)
        x1 = _merge(xr, y_ssm.reshape(seq, bsz * SSM_WIDTH), y_pool.reshape(seq, bsz * POOL_WIDTH), y_nsa, bg,
                    wbr[l], wo[l], lng[l, 0], lnb[l, 0], bsz, seq, alpha)
        xr = _ffn(x1, ffn_wi[l], ffn_wo[l], lng[l, 1], lnb[l, 1], alpha)
    return xr.reshape(bsz, seq, D_MODEL).astype(x.dtype)
```

```python
import functools
import math

import jax
import jax.numpy as jnp
from jax import lax
from jax.experimental import pallas as pl
from jax.experimental.pallas import tpu as pltpu

F32 = jnp.float32
BF16 = jnp.bfloat16

D_MODEL = 1024
SSM_WIDTH = 512
SSM_GROUP = 16
SSM_GROUPS = 32
SSM_STATE = 64
POOL_WIDTH = 512
POOL_WINDOWS = (2, 4, 8, 16)
POOL_GROUP = 128
HEAD_DIM = 64
N_HEADS = 8
N_KV_HEADS = 2
GQA_GROUP = 4
ATTN_WIDTH = 512
KV_WIDTH = 128
N_BRANCH = 3
CMP_BLOCK = 32
CMP_STRIDE = 16
CMP_HIDDEN = 256
SEL_BLOCK = 64
SEL_TOP = 16
WINDOW = 512
ROPE_THETA = 10000.0
FF_HIDDEN = 2816
LN_EPS = 1e-5
NEG = -1e30
N_GATE = 3 * N_HEADS
IN_RAW = 3 * 512 + 6 * KV_WIDTH + N_GATE + N_BRANCH * D_MODEL

LANES = 128
SUBLANES = 8
VMEM_LIMIT_BYTES = 56 * 1024 * 1024

COL_SSM = 0
COL_POOL = 512
COL_Q = 1024
COL_KV = 1536
COL_NG = 2304
COL_BG = COL_NG + N_KV_HEADS * LANES
IN_PACKED = COL_BG + N_BRANCH * D_MODEL
LOG2E = 1.4426950408889634

ROW_TILE = 512
S5_STEPS = 64
S5_LANE_CHUNK = 512
POOL_STEPS = 128
POOL_HALO = 16
Q_TILE = 256
K_TILE = 256
FF_CHUNK = 256


def _cparams(n_axes):
    return pltpu.CompilerParams(dimension_semantics=("arbitrary",) * n_axes,
                                vmem_limit_bytes=VMEM_LIMIT_BYTES)


def _resident(shape):
    nd = len(shape)
    return pl.BlockSpec(shape, lambda *_: (0,) * nd, pipeline_mode=pl.Buffered(1))


def _gelu_tanh(x):
    return x * (0.5 * (1.0 + jnp.tanh(math.sqrt(2.0 / math.pi) * (x + 0.044715 * (x * x * x)))))


def _layer_norm(r, g, b):
    mu = jnp.mean(r, axis=-1, keepdims=True)
    c = r - mu
    var = jnp.mean(c * c, axis=-1, keepdims=True)
    return c * lax.rsqrt(var + LN_EPS) * g + b


def _dot(a, b):
    return jnp.dot(a, b, preferred_element_type=F32)


def _dot_nt(a, b):
    return lax.dot_general(a, b, (((1,), (1,)), ((), ())), preferred_element_type=F32)


def _rope_table_kernel(pos_ref, inv_ref, cos_ref, sin_ref):
    ang = pos_ref[...] * inv_ref[...]
    lane = lax.broadcasted_iota(jnp.int32, ang.shape, 1)
    first_half = jnp.bitwise_and(lane, HEAD_DIM - 1) < HEAD_DIM // 2
    cos_ref[...] = jnp.cos(ang)
    sin_ref[...] = jnp.where(first_half, -jnp.sin(ang), jnp.sin(ang))


def _rope_tables(positions):
    rows = positions.size
    inv = ROPE_THETA ** (-jnp.arange(0, HEAD_DIM, 2, dtype=F32) / HEAD_DIM)
    inv = jnp.tile(inv, LANES // (HEAD_DIM // 2)).reshape(1, LANES)
    pos = positions.astype(F32).reshape(rows, 1)
    tile = ROW_TILE
    return pl.pallas_call(
        _rope_table_kernel,
        grid=(rows // tile,),
        in_specs=[pl.BlockSpec((tile, 1), lambda i: (i, 0)),
                  pl.BlockSpec((1, LANES), lambda i: (0, 0))],
        out_specs=[pl.BlockSpec((tile, LANES), lambda i: (i, 0))] * 2,
        out_shape=[jax.ShapeDtypeStruct((rows, LANES), F32)] * 2,
        compiler_params=_cparams(1),
        name="rope_tables",
    )(pos, inv)


def _inproj_kernel(x_ref, w_ref, cos_ref, sin_ref,
                   ussm_ref, upool_ref, q_ref, cmp_ref, ksel_ref, vsel_ref, kwin_ref, vwin_ref,
                   ng_ref, bg_ref):
    xb = x_ref[...].astype(BF16)
    cos = cos_ref[...]
    sin = sin_ref[...]
    lane = lax.broadcasted_iota(jnp.int32, cos.shape, 1)
    first_half = jnp.bitwise_and(lane, HEAD_DIM - 1) < HEAD_DIM // 2
    left = lane < HEAD_DIM

    def proj(c0, width):
        return _dot(xb, w_ref[:, c0:c0 + width])

    def rope(t):
        swapped = jnp.where(first_half, pltpu.roll(t, LANES - HEAD_DIM // 2, 1),
                            pltpu.roll(t, HEAD_DIM // 2, 1))
        return t * cos + swapped * sin

    def dup(t):
        r = pltpu.roll(t, HEAD_DIM, 1)
        return jnp.where(left, t, r), jnp.where(left, r, t)

    def with_ones(t):
        return jnp.where(left, t, 1.0), jnp.where(left, pltpu.roll(t, HEAD_DIM, 1), 1.0)

    ussm_ref[...] = proj(COL_SSM, SSM_WIDTH).astype(BF16)
    upool_ref[...] = proj(COL_POOL, POOL_WIDTH).astype(BF16)
    scale = HEAD_DIM ** -0.5 * LOG2E
    for j in range(ATTN_WIDTH // LANES):
        t = rope(proj(COL_Q + j * LANES, LANES))
        q_ref[:, j * LANES:(j + 1) * LANES] = (t * scale).astype(BF16)

    kv = proj(COL_KV, 6 * KV_WIDTH)
    for j, roped in enumerate((True, False)):
        t = kv[:, j * LANES:(j + 1) * LANES]
        a, b = dup(rope(t) if roped else t)
        cmp_ref[j, 0, 0] = a[:, :HEAD_DIM].astype(BF16)
        cmp_ref[j, 0, 1] = b[:, :HEAD_DIM].astype(BF16)
    for j, (ref, is_key) in enumerate(((ksel_ref, True), (vsel_ref, False), (kwin_ref, True), (vwin_ref, False))):
        t = kv[:, (2 + j) * LANES:(3 + j) * LANES]
        a, b = dup(rope(t)) if is_key else with_ones(t)
        ref[0, :, 0:LANES] = a.astype(BF16)
        ref[0, :, LANES:2 * LANES] = b.astype(BF16)

    ng_ref[...] = jax.nn.sigmoid(proj(COL_NG, N_KV_HEADS * LANES))
    for k in range(N_BRANCH):
        bg_ref[:, k * D_MODEL:(k + 1) * D_MODEL] = jax.nn.sigmoid(proj(COL_BG + k * D_MODEL, D_MODEL)).astype(BF16)


def _inproj(xr, w, cos, sin, bsz, seq):
    rows = bsz * seq
    tile = ROW_TILE
    n_s = seq // tile
    row = lambda i: (i, 0)
    tmajor = lambda i: (i % n_s, i // n_s)
    dup_spec = pl.BlockSpec((1, tile, 2 * LANES), lambda i: (i // n_s, i % n_s, 0))
    out_shape = [
        jax.ShapeDtypeStruct((seq, bsz * SSM_WIDTH), BF16),
        jax.ShapeDtypeStruct((seq, bsz * POOL_WIDTH), BF16),
        jax.ShapeDtypeStruct((rows, ATTN_WIDTH), BF16),
        jax.ShapeDtypeStruct((2, bsz, N_KV_HEADS, seq, HEAD_DIM), BF16),
        jax.ShapeDtypeStruct((bsz, seq, 2 * LANES), BF16),
        jax.ShapeDtypeStruct((bsz, seq, 2 * LANES), BF16),
        jax.ShapeDtypeStruct((bsz, seq, 2 * LANES), BF16),
        jax.ShapeDtypeStruct((bsz, seq, 2 * LANES), BF16),
        jax.ShapeDtypeStruct((rows, N_KV_HEADS * LANES), F32),
        jax.ShapeDtypeStruct((rows, N_BRANCH * D_MODEL), BF16),
    ]
    out_specs = [
        pl.BlockSpec((tile, SSM_WIDTH), tmajor),
        pl.BlockSpec((tile, POOL_WIDTH), tmajor),
        pl.BlockSpec((tile, ATTN_WIDTH), row),
        pl.BlockSpec((2, 1, N_KV_HEADS, tile, HEAD_DIM), lambda i: (0, i // n_s, 0, i % n_s, 0)),
        dup_spec, dup_spec, dup_spec, dup_spec,
        pl.BlockSpec((tile, N_KV_HEADS * LANES), row),
        pl.BlockSpec((tile, N_BRANCH * D_MODEL), row),
    ]
    return pl.pallas_call(
        _inproj_kernel,
        grid=(rows // tile,),
        in_specs=[pl.BlockSpec((tile, D_MODEL), row), _resident(w.shape),
                  pl.BlockSpec((tile, LANES), row), pl.BlockSpec((tile, LANES), row)],
        out_specs=out_specs,
        out_shape=out_shape,
        compiler_params=_cparams(1),
        name="in_projection",
    )(xr, w, cos, sin)


def _s5_kernel(u_ref, lre_ref, lim_ref, wb_ref, wc_ref, d_ref, wglu_ref, o_ref, bu_ref, st_ref, y_ref):
    n_rows = u_ref.shape[0]
    n_steps = n_rows // SUBLANES
    n_chunk = wb_ref.shape[0]
    cw = S5_LANE_CHUNK

    @pl.when(pl.program_id(0) == 0)
    def _():
        st_ref[...] = jnp.zeros(st_ref.shape, F32)

    u = u_ref[...]
    for c in range(n_chunk):
        bu_ref[:, c * 2 * cw:(c + 1) * 2 * cw] = _dot(u[:, c * LANES:(c + 1) * LANES], wb_ref[c])

    for c in range(n_chunk):
        re0, im0 = c * 2 * cw, c * 2 * cw + cw
        lre = lre_ref[:, c * cw:(c + 1) * cw]
        lim = lim_ref[:, c * cw:(c + 1) * cw]

        def step(t, carry, re0=re0, im0=im0, lre=lre, lim=lim):
            hre, him = carry
            r0 = pl.multiple_of(t * SUBLANES, SUBLANES)
            nre = lre * hre - lim * him + bu_ref[pl.ds(r0, SUBLANES), re0:re0 + cw]
            nim = lre * him + lim * hre + bu_ref[pl.ds(r0, SUBLANES), im0:im0 + cw]
            bu_ref[pl.ds(r0, SUBLANES), re0:re0 + cw] = nre
            bu_ref[pl.ds(r0, SUBLANES), im0:im0 + cw] = nim
            return nre, nim

        hre, him = lax.fori_loop(0, n_steps, step,
                                 (st_ref[0, :, c * cw:(c + 1) * cw], st_ref[1, :, c * cw:(c + 1) * cw]),
                                 unroll=2)
        st_ref[0, :, c * cw:(c + 1) * cw] = hre
        st_ref[1, :, c * cw:(c + 1) * cw] = him

    for c in range(n_chunk):
        y_ref[:, c * LANES:(c + 1) * LANES] = _dot(bu_ref[:, c * 2 * cw:(c + 1) * 2 * cw].astype(BF16), wc_ref[c])
    y = y_ref[...] + d_ref[...] * u.astype(F32)
    z = _dot(_gelu_tanh(y).astype(BF16), wglu_ref[...])
    o_ref[...] = (z[:, :SSM_WIDTH] * jax.nn.sigmoid(z[:, SSM_WIDTH:])).astype(BF16)


def _s5(u_t, lre, lim, wb, wc, d, wglu):
    rows = u_t.shape[0]
    tile = S5_STEPS * SUBLANES
    state_w = lre.shape[1]
    return pl.pallas_call(
        _s5_kernel,
        grid=(rows // tile,),
        in_specs=[pl.BlockSpec((tile, SSM_WIDTH), lambda i: (i, 0)),
                  _resident(lre.shape), _resident(lim.shape), _resident(wb.shape), _resident(wc.shape),
                  _resident(d.shape), _resident(wglu.shape)],
        out_specs=pl.BlockSpec((tile, SSM_WIDTH), lambda i: (i, 0)),
        out_shape=jax.ShapeDtypeStruct((rows, SSM_WIDTH), BF16),
        scratch_shapes=[pltpu.VMEM((tile, 2 * state_w), F32),
                        pltpu.VMEM((2, SUBLANES, state_w), F32),
                        pltpu.VMEM((tile, SSM_WIDTH), F32)],
        compiler_params=_cparams(1),
        name="s5_mixer",
    )(u_t, lre, lim, wb, wc, d, wglu)


def _pool_kernel(u_ref, w_ref, sc_ref, o_ref, ext_ref):
    n_rows = u_ref.shape[0]
    halo = POOL_HALO * SUBLANES
    i = pl.program_id(0)

    @pl.when(i == 0)
    def _():
        ext_ref[0:halo, :] = jnp.zeros((halo, POOL_WIDTH), F32)

    ext_ref[halo:halo + n_rows, :] = u_ref[...].astype(F32)
    row = lax.broadcasted_iota(jnp.int32, (n_rows, POOL_GROUP), 0)
    t = i * (n_rows // SUBLANES) + jnp.right_shift(row, 3)
    for gi, w in enumerate(POOL_WINDOWS):
        c0 = gi * POOL_GROUP
        cur = ext_ref[halo:halo + n_rows, c0:c0 + POOL_GROUP]
        acc = cur
        for k in range(1, w):
            acc = acc + ext_ref[halo - k * SUBLANES:halo - k * SUBLANES + n_rows, c0:c0 + POOL_GROUP]
        cnt = jnp.minimum(t + 1, w).astype(F32)
        pooled = acc / cnt - cur
        mixed = _dot(pooled.astype(BF16), w_ref[gi]) * sc_ref[:, c0:c0 + POOL_GROUP]
        o_ref[:, c0:c0 + POOL_GROUP] = mixed.astype(BF16)
    ext_ref[0:halo, :] = ext_ref[n_rows:n_rows + halo, :]


def _pool(u_t, w, sc):
    rows = u_t.shape[0]
    tile = POOL_STEPS * SUBLANES
    return pl.pallas_call(
        _pool_kernel,
        grid=(rows // tile,),
        in_specs=[pl.BlockSpec((tile, POOL_WIDTH), lambda i: (i, 0)), _resident(w.shape), _resident(sc.shape)],
        out_specs=pl.BlockSpec((tile, POOL_WIDTH), lambda i: (i, 0)),
        out_shape=jax.ShapeDtypeStruct((rows, POOL_WIDTH), BF16),
        scratch_shapes=[pltpu.VMEM((tile + POOL_HALO * SUBLANES, POOL_WIDTH), F32)],
        compiler_params=_cparams(1),
        name="pool_mixer",
    )(u_t, w, sc)


def _compress_kernel(z_ref, w1_ref, pos_ref, b1_ref, w2_ref, o_ref):
    half = CMP_STRIDE * HEAD_DIM
    w1 = w1_ref[0]
    ab = _dot(z_ref[0, 0], w1)
    n_chunks = ab.shape[0]
    top = ab[:, :CMP_HIDDEN]
    bottom_next = pltpu.roll(ab[:, CMP_HIDDEN:], n_chunks - 1, 0)
    pos = pos_ref[0].astype(BF16)
    cst = _dot(pos[:, :half], w1[:, :CMP_HIDDEN]) + _dot(pos[:, half:], w1[:, CMP_HIDDEN:])
    hid = _gelu_tanh(top + bottom_next + cst[0:1, :] + b1_ref[0])
    out = _dot(hid.astype(BF16), w2_ref[0])
    lane = lax.broadcasted_iota(jnp.int32, out.shape, 1)
    is_value = pl.program_id(0) == 1
    o_ref[0, 0] = jnp.where((lane >= HEAD_DIM) & is_value, 1.0, out).astype(BF16)


def _compress(z, w1cat, pos, b1, w2dup):
    _, n_bh, n_chunks, width = z.shape
    return pl.pallas_call(
        _compress_kernel,
        grid=(2, n_bh),
        in_specs=[pl.BlockSpec((1, 1, n_chunks, width), lambda j, r: (j, r, 0, 0)),
                  pl.BlockSpec((1,) + w1cat.shape[1:], lambda j, r: (j, 0, 0)),
                  pl.BlockSpec((1,) + pos.shape[1:], lambda j, r: (j, 0, 0)),
                  pl.BlockSpec((1,) + b1.shape[1:], lambda j, r: (j, 0, 0)),
                  pl.BlockSpec((1,) + w2dup.shape[1:], lambda j, r: (j, 0, 0))],
        out_specs=pl.BlockSpec((1, 1, n_chunks, LANES), lambda j, r: (j, r, 0, 0)),
        out_shape=jax.ShapeDtypeStruct((2, n_bh, n_chunks, LANES), BF16),
        compiler_params=_cparams(2),
        name="compress_mlp",
    )(z, w1cat, pos, b1, w2dup)


def _nsa_kernel(q_ref, kc_ref, vc_ref, ks_ref, vs_ref, kw_ref, vw_ref, g_ref, o_ref,
                sa_ref, sb_ref, p_ref, pw_ref, pc_ref, m_ref, a_ref, acc_ref, *, seq):
    tq = q_ref.shape[0]
    tk = sa_ref.shape[1]
    wk = pw_ref.shape[1]
    n_sel = seq // SEL_BLOCK
    n_top = min(SEL_TOP, n_sel)
    n_cmp = (seq - CMP_BLOCK) // CMP_STRIDE + 1
    t0 = pl.program_id(2) * tq
    head_rows = [slice(g * tq, (g + 1) * tq) for g in range(GQA_GROUP)]

    lane = lax.broadcasted_iota(jnp.int32, (tq, LANES), 1)
    left = lane < HEAD_DIM
    zero = jnp.zeros((tq, LANES), BF16)
    pairs = (q_ref[:, 0:LANES], q_ref[:, LANES:2 * LANES])
    q4 = jnp.concatenate([jnp.where(left, pairs[0], zero), jnp.where(left, zero, pairs[0]),
                          jnp.where(left, pairs[1], zero), jnp.where(left, zero, pairs[1])], axis=0)
    tq_col = t0 + lax.broadcasted_iota(jnp.int32, (tq, 1), 0)

    w0 = pl.multiple_of(jnp.maximum(t0 - WINDOW, 0), tq)
    dist = tq_col - (w0 + lax.broadcasted_iota(jnp.int32, (1, wk), 1))
    bias_w = jnp.where((dist >= 0) & (dist < WINDOW), 0.0, NEG)
    s_w = _dot_nt(q4, kw_ref[0, pl.ds(w0, wk), :])
    for rows in head_rows:
        s = s_w[rows] + bias_w
        pw_ref[rows, :] = jnp.exp2(s - jnp.max(s, axis=-1, keepdims=True)).astype(BF16)
    acc_w = _dot(pw_ref[...], vw_ref[0, pl.ds(w0, wk), :])

    n_idx = lax.broadcasted_iota(jnp.int32, (1, kc_ref.shape[2]), 1)
    cmp_valid = (n_idx * CMP_STRIDE + (CMP_BLOCK - 1) <= tq_col) & (n_idx < n_cmp)
    s_c = _dot_nt(q4, kc_ref[0, 0])
    p_sum = None
    for rows in head_rows:
        s = jnp.where(cmp_valid, s_c[rows], NEG)
        e = jnp.where(cmp_valid, jnp.exp2(s - jnp.max(s, axis=-1, keepdims=True)), 0.0)
        l = jnp.sum(e, axis=-1, keepdims=True)
        p = e / jnp.where(l > 0.0, l, 1.0)
        p_sum = p if p_sum is None else p_sum + p
        pc_ref[rows, :] = p.astype(BF16)
    o_cmp = _dot(pc_ref[...], vc_ref[0, 0])

    oj = lax.broadcasted_iota(jnp.int32, (n_sel, kc_ref.shape[2]), 0)
    on = lax.broadcasted_iota(jnp.int32, (n_sel, kc_ref.shape[2]), 1)
    overlap = jnp.clip(jnp.minimum(on * CMP_STRIDE + CMP_BLOCK, oj * SEL_BLOCK + SEL_BLOCK)
                       - jnp.maximum(on * CMP_STRIDE, oj * SEL_BLOCK), 0, CMP_BLOCK).astype(F32) * (1.0 / CMP_BLOCK)
    overlap = jnp.where(on < n_cmp, overlap, 0.0)
    imp = lax.dot_general(overlap, p_sum, (((1,), (1,)), ((), ())), precision=lax.Precision.HIGHEST,
                          preferred_element_type=F32)
    jb = lax.broadcasted_iota(jnp.int32, (n_sel, tq), 0)
    tt = t0 + lax.broadcasted_iota(jnp.int32, (n_sel, tq), 1)
    cur = jnp.right_shift(tt, 6)
    forced = (jb == 0) | (jb == cur) | (jb == cur - 1)
    causal = jb * SEL_BLOCK <= tt
    score = jnp.where(forced, 1e30, jnp.where(causal, imp, NEG))
    rank = jnp.zeros((n_sel, tq), F32)
    for a in range(n_sel):
        sa = score[a:a + 1, :]
        ahead = (sa > score) | ((sa == score) & (jb > a))
        rank = rank + jnp.where(ahead, 1.0, 0.0)
    chosen = (rank < float(n_top)) & causal
    sel_bias = jnp.where(chosen, 0.0, NEG).T.astype(BF16)

    m_ref[...] = jnp.full(m_ref.shape, NEG, F32)
    acc_ref[...] = jnp.zeros(acc_ref.shape, F32)

    def tile_start(kt):
        return pl.multiple_of(jnp.minimum(kt * tk, seq - tk), tk)

    def scores(kt, dst_ref):
        k0 = kt * tk
        expand = (lax.broadcasted_iota(jnp.int32, (n_sel, tk), 0)
                  == jnp.right_shift(k0 + lax.broadcasted_iota(jnp.int32, (n_sel, tk), 1), 6))
        bias = _dot(sel_bias, jnp.where(expand, 1.0, 0.0).astype(BF16))
        bias = jnp.where(k0 + lax.broadcasted_iota(jnp.int32, (1, tk), 1) <= tq_col, bias, NEG)
        s = _dot_nt(q4, ks_ref[0, pl.ds(tile_start(kt), tk), :])
        for rows in head_rows:
            dst_ref[rows, :] = s[rows] + bias

    def consume(kt, src_ref):
        for rows in head_rows:
            s = src_ref[rows, :]
            m_old = m_ref[rows, :]
            m_new = jnp.maximum(m_old, jnp.max(s, axis=-1, keepdims=True))
            a_ref[rows, :] = jnp.exp2(m_old - m_new)
            m_ref[rows, :] = m_new
            p_ref[rows, :] = jnp.exp2(s - jnp.concatenate([m_new] * (tk // LANES), axis=1)).astype(BF16)
        acc_ref[...] = a_ref[...] * acc_ref[...] + _dot(p_ref[...], vs_ref[0, pl.ds(tile_start(kt), tk), :])

    def pair_step(j, carry):
        scores(2 * j + 1, sb_ref)
        consume(2 * j, sa_ref)
        scores(2 * j + 2, sa_ref)
        consume(2 * j + 1, sb_ref)
        return carry

    scores(0, sa_ref)
    n_kt = (t0 + tq + tk - 1) // tk
    lax.fori_loop(0, (n_kt + 1) // 2, pair_step, 0)
    acc_s = acc_ref[...]

    g = g_ref[...]
    g_hi = g.astype(BF16)
    g_lo = (g - g_hi.astype(F32)).astype(BF16)
    src = lax.broadcasted_iota(jnp.int32, (LANES, LANES), 0)
    dst_head = jnp.right_shift(lax.broadcasted_iota(jnp.int32, (LANES, LANES), 1), 6)

    def pair_tile(acc, pair, normalise):
        a, b = acc[head_rows[2 * pair]], acc[head_rows[2 * pair + 1]]
        num = jnp.where(left, a, pltpu.roll(b, HEAD_DIM, 1))
        return num / jnp.where(left, pltpu.roll(a, HEAD_DIM, 1), b) if normalise else num

    for pair in range(GQA_GROUP // 2):
        out = None
        for br, (acc, normalise) in enumerate(((o_cmp, False), (acc_s, True), (acc_w, True))):
            pick = jnp.where(src == br * GQA_GROUP + 2 * pair + dst_head, 1.0, 0.0).astype(BF16)
            gate = _dot(g_hi, pick) + _dot(g_lo, pick)
            term = gate * pair_tile(acc, pair, normalise)
            out = term if out is None else out + term
        o_ref[:, pair * LANES:(pair + 1) * LANES] = out.astype(BF16)


def _nsa(q, cmp_kv, ksel, vsel, kwin, vwin, ng, bsz, seq):
    tq = Q_TILE
    n_q = seq // tq
    n_chunks = cmp_kv.shape[2]
    rows = GQA_GROUP * tq
    qrow = lambda b, h, i: (b * n_q + i, h)
    kv_spec = pl.BlockSpec((1, seq, LANES), lambda b, h, i: (b, 0, h))
    return pl.pallas_call(
        functools.partial(_nsa_kernel, seq=seq),
        grid=(bsz, N_KV_HEADS, n_q),
        in_specs=[pl.BlockSpec((tq, 2 * LANES), qrow),
                  pl.BlockSpec((1, 1, n_chunks, LANES), lambda b, h, i: (0, b * N_KV_HEADS + h, 0, 0)),
                  pl.BlockSpec((1, 1, n_chunks, LANES), lambda b, h, i: (1, b * N_KV_HEADS + h, 0, 0)),
                  kv_spec, kv_spec, kv_spec, kv_spec,
                  pl.BlockSpec((tq, LANES), qrow)],
        out_specs=pl.BlockSpec((tq, 2 * LANES), qrow),
        out_shape=jax.ShapeDtypeStruct((bsz * seq, ATTN_WIDTH), BF16),
        scratch_shapes=[pltpu.VMEM((rows, K_TILE), F32), pltpu.VMEM((rows, K_TILE), F32),
                        pltpu.VMEM((rows, K_TILE), BF16), pltpu.VMEM((rows, WINDOW + tq), BF16),
                        pltpu.VMEM((rows, n_chunks), BF16),
                        pltpu.VMEM((rows, LANES), F32), pltpu.VMEM((rows, LANES), F32),
                        pltpu.VMEM((rows, LANES), F32)],
        compiler_params=_cparams(3),
        name="nsa_attention",
    )(q, cmp_kv, cmp_kv, ksel, vsel, kwin, vwin, ng)


def _merge_kernel(x_ref, ys_ref, yp_ref, yn_ref, bg_ref, wb_ref, wo_ref, g_ref, b_ref, o_ref, *, alpha):
    merged = None
    for k, y_ref in enumerate((ys_ref, yp_ref, yn_ref)):
        term = bg_ref[:, k * D_MODEL:(k + 1) * D_MODEL].astype(F32) * _dot(y_ref[...], wb_ref[k])
        merged = term if merged is None else merged + term
    r = alpha * x_ref[...] + _dot(merged.astype(BF16), wo_ref[...])
    o_ref[...] = _layer_norm(r, g_ref[...], b_ref[...])


def _merge(xr, ys_t, yp_t, yn, bg, wb, wo, g, b, bsz, seq, alpha):
    rows = bsz * seq
    tile = ROW_TILE
    n_s = seq // tile
    row = lambda i: (i, 0)
    tmajor = lambda i: (i % n_s, i // n_s)
    return pl.pallas_call(
        functools.partial(_merge_kernel, alpha=alpha),
        grid=(rows // tile,),
        in_specs=[pl.BlockSpec((tile, D_MODEL), row),
                  pl.BlockSpec((tile, SSM_WIDTH), tmajor), pl.BlockSpec((tile, POOL_WIDTH), tmajor),
                  pl.BlockSpec((tile, ATTN_WIDTH), row), pl.BlockSpec((tile, N_BRANCH * D_MODEL), row),
                  _resident(wb.shape), _resident(wo.shape), _resident(g.shape), _resident(b.shape)],
        out_specs=pl.BlockSpec((tile, D_MODEL), row),
        out_shape=jax.ShapeDtypeStruct((rows, D_MODEL), F32),
        compiler_params=_cparams(1),
        name="branch_merge",
    )(xr, ys_t, yp_t, yn, bg, wb, wo, g, b)


def _ffn_kernel(x_ref, wi_ref, wo_ref, g_ref, b_ref, o_ref, *, alpha):
    x = x_ref[...]
    xb = x.astype(BF16)
    acc = None
    for c in range(FF_HIDDEN // FF_CHUNK):
        c0 = c * FF_CHUNK
        hg = _dot(xb, wi_ref[:, c0:c0 + FF_CHUNK])
        hu = _dot(xb, wi_ref[:, FF_HIDDEN + c0:FF_HIDDEN + c0 + FF_CHUNK])
        act = (hg * jax.nn.sigmoid(hg) * hu).astype(BF16)
        part = _dot(act, wo_ref[c0:c0 + FF_CHUNK, :])
        acc = part if acc is None else acc + part
    o_ref[...] = _layer_norm(alpha * x + acc, g_ref[...], b_ref[...])


def _ffn(xr, wi, wo, g, b, alpha):
    rows = xr.shape[0]
    tile = ROW_TILE
    row = lambda i: (i, 0)
    return pl.pallas_call(
        functools.partial(_ffn_kernel, alpha=alpha),
        grid=(rows // tile,),
        in_specs=[pl.BlockSpec((tile, D_MODEL), row), _resident(wi.shape), _resident(wo.shape),
                  _resident(g.shape), _resident(b.shape)],
        out_specs=pl.BlockSpec((tile, D_MODEL), row),
        out_shape=jax.ShapeDtypeStruct((rows, D_MODEL), F32),
        compiler_params=_cparams(1),
        name="swiglu_ffn",
    )(xr, wi, wo, g, b)


def _pack_s5(a_re, a_im, log_dt, b_re, b_im, c_re, c_im):
    depth = a_re.shape[0]
    a_re, a_im = a_re.astype(F32), a_im.astype(F32)
    dt = jnp.exp(log_dt.astype(F32))[..., None]
    mag = jnp.exp(a_re * dt)
    lbar_re, lbar_im = mag * jnp.cos(a_im * dt), mag * jnp.sin(a_im * dt)
    den = a_re * a_re + a_im * a_im
    coef_re = ((lbar_re - 1.0) * a_re + lbar_im * a_im) / den
    coef_im = (lbar_im * a_re - (lbar_re - 1.0) * a_im) / den
    b_re, b_im = b_re.astype(F32), b_im.astype(F32)
    bbar_re = coef_re[..., None] * b_re - coef_im[..., None] * b_im
    bbar_im = coef_re[..., None] * b_im + coef_im[..., None] * b_re
    gpc = S5_LANE_CHUNK // SSM_STATE
    n_chunk = SSM_GROUPS // gpc
    eye = jnp.eye(gpc, dtype=F32)

    def b_block(part):
        v = part.transpose(0, 1, 3, 2).reshape(depth, n_chunk, gpc, SSM_GROUP, SSM_STATE)
        return jnp.einsum('xy,dqxcp->dqxcyp', eye, v).reshape(depth, n_chunk, gpc * SSM_GROUP, gpc * SSM_STATE)

    def c_block(part):
        v = part.reshape(depth, n_chunk, gpc, SSM_GROUP, SSM_STATE)
        return jnp.einsum('xy,dqxcp->dqxpyc', eye, v).reshape(depth, n_chunk, gpc * SSM_STATE, gpc * SSM_GROUP)

    wb = jnp.concatenate([b_block(bbar_re), b_block(bbar_im)], axis=-1).astype(BF16)
    wc = jnp.concatenate([c_block(c_re.astype(F32)), c_block(-c_im.astype(F32))], axis=-2).astype(BF16)
    state_w = SSM_GROUPS * SSM_STATE
    lre = jnp.broadcast_to(lbar_re.reshape(depth, 1, state_w), (depth, SUBLANES, state_w))
    lim = jnp.broadcast_to(lbar_im.reshape(depth, 1, state_w), (depth, SUBLANES, state_w))
    return lre, lim, wb, wc


def kernel(x, positions, w_in, ssm_a_re, ssm_a_im, ssm_log_dt, ssm_b_re, ssm_b_im, ssm_c_re, ssm_c_im,
           ssm_d, ssm_w_glu, pool_w, pool_scale, cmp_pos, cmp_w1, cmp_b1, cmp_w2,
           w_branch, w_out, ln_g, ln_b, ffn_w_in, ffn_w_out):
    bsz, seq, _ = x.shape
    depth = w_in.shape[0]
    rows = bsz * seq
    assert bsz == SUBLANES and seq % ROW_TILE == 0 and seq >= WINDOW + Q_TILE
    alpha = (2 * depth) ** 0.25

    w_gate = w_in[:, :, COL_NG:COL_NG + N_GATE].reshape(depth, D_MODEL, 3, N_KV_HEADS, GQA_GROUP)
    w_gate = w_gate.transpose(0, 1, 3, 2, 4).reshape(depth, D_MODEL, N_KV_HEADS, 3 * GQA_GROUP)
    w_gate = jnp.pad(w_gate, ((0, 0), (0, 0), (0, 0), (0, LANES - 3 * GQA_GROUP))).reshape(depth, D_MODEL, N_KV_HEADS * LANES)
    w_in_p = jnp.concatenate([w_in[:, :, :COL_NG], w_gate, w_in[:, :, COL_NG + N_GATE:]], axis=-1).astype(BF16)
    assert w_in.shape[-1] == IN_RAW and w_in_p.shape[-1] == IN_PACKED
    lre, lim, s5_wb, s5_wc = _pack_s5(ssm_a_re, ssm_a_im, ssm_log_dt, ssm_b_re, ssm_b_im, ssm_c_re, ssm_c_im)
    s5_d = ssm_d.astype(F32).reshape(depth, 1, SSM_WIDTH)
    s5_wglu = ssm_w_glu.astype(BF16)
    pool_wb = pool_w.astype(BF16)
    pool_sc = pool_scale.astype(F32).reshape(depth, 1, POOL_WIDTH)
    half = CMP_STRIDE * HEAD_DIM
    cmp_w1cat = jnp.concatenate([cmp_w1[:, :, :half, :], cmp_w1[:, :, half:, :]], axis=-1).astype(BF16)
    cmp_posr = jnp.broadcast_to(cmp_pos.astype(F32).reshape(depth, 2, 1, CMP_BLOCK * HEAD_DIM),
                                (depth, 2, SUBLANES, CMP_BLOCK * HEAD_DIM))
    cmp_b1r = cmp_b1.astype(F32).reshape(depth, 2, 1, CMP_HIDDEN)
    cmp_w2dup = jnp.concatenate([cmp_w2, cmp_w2 * jnp.array([1.0, 0.0], cmp_w2.dtype).reshape(1, 2, 1, 1)],
                                axis=-1).astype(BF16)
    wbr = w_branch.astype(BF16)
    wo = w_out.astype(BF16)
    lng = ln_g.astype(F32).reshape(depth, 2, 1, D_MODEL)
    lnb = ln_b.astype(F32).reshape(depth, 2, 1, D_MODEL)
    ffn_wi = ffn_w_in.astype(BF16)
    ffn_wo = ffn_w_out.astype(BF16)

    cos, sin = _rope_tables(positions)
    xr = x.astype(F32).reshape(rows, D_MODEL)
    for l in range(depth):
        (u_ssm, u_pool, q, cmp_in, ksel, vsel, kwin, vwin, ng, bg) = _inproj(xr, w_in_p[l], cos, sin, bsz, seq)
        y_ssm = _s5(u_ssm.reshape(seq * bsz, SSM_WIDTH), lre[l], lim[l], s5_wb[l], s5_wc[l], s5_d[l], s5_wglu[l])
        y_pool = _pool(u_pool.reshape(seq * bsz, POOL_WIDTH), pool_wb[l], pool_sc[l])
        cmp_z = cmp_in.reshape(2, bsz * N_KV_HEADS, seq // CMP_STRIDE, CMP_STRIDE * HEAD_DIM)
        cmp_kv = _compress(cmp_z, cmp_w1cat[l], cmp_posr[l], cmp_b1r[l], cmp_w2dup[l])
        y_nsa = _nsa(q, cmp_kv, ksel, vsel, kwin, vwin, ng, bsz, seq)
        x1 = _merge(xr, y_ssm.reshape(seq, bsz * SSM_WIDTH), y_pool.reshape(seq, bsz * POOL_WIDTH), y_nsa, bg,
                    wbr[l], wo[l], lng[l, 0], lnb[l, 0], bsz, seq, alpha)
        xr = _ffn(x1, ffn_wi[l], ffn_wo[l], lng[l, 1], lnb[l, 1], alpha)
    return xr.reshape(bsz, seq, D_MODEL).astype(x.dtype)
```

```python
import functools
import math

import jax
import jax.numpy as jnp
from jax import lax
from jax.experimental import pallas as pl
from jax.experimental.pallas import tpu as pltpu

F32 = jnp.float32
BF16 = jnp.bfloat16

D_MODEL = 1024
SSM_WIDTH = 512
SSM_GROUP = 16
SSM_GROUPS = 32
SSM_STATE = 64
POOL_WIDTH = 512
POOL_WINDOWS = (2, 4, 8, 16)
POOL_GROUP = 128
HEAD_DIM = 64
N_HEADS = 8
N_KV_HEADS = 2
GQA_GROUP = 4
ATTN_WIDTH = 512
KV_WIDTH = 128
N_BRANCH = 3
CMP_BLOCK = 32
CMP_STRIDE = 16
CMP_HIDDEN = 256
SEL_BLOCK = 64
SEL_TOP = 16
WINDOW = 512
ROPE_THETA = 10000.0
FF_HIDDEN = 2816
LN_EPS = 1e-5
NEG = -1e30
N_GATE = 3 * N_HEADS
IN_RAW = 3 * 512 + 6 * KV_WIDTH + N_GATE + N_BRANCH * D_MODEL

LANES = 128
SUBLANES = 8
VMEM_LIMIT_BYTES = 56 * 1024 * 1024

COL_SSM = 0
COL_POOL = 512
COL_Q = 1024
COL_KV = 1536
COL_NG = 2304
COL_BG = COL_NG + N_KV_HEADS * LANES
IN_PACKED = COL_BG + N_BRANCH * D_MODEL
LOG2E = 1.4426950408889634

ROW_TILE = 512
S5_STEPS = 64
S5_LANE_CHUNK = 512
POOL_STEPS = 64
POOL_HALO = 16
Q_TILE = 256
K_TILE = 256
FF_CHUNK = 256


def _cparams(n_axes):
    return pltpu.CompilerParams(dimension_semantics=("arbitrary",) * n_axes,
                                vmem_limit_bytes=VMEM_LIMIT_BYTES)


def _resident(shape):
    nd = len(shape)
    return pl.BlockSpec(shape, lambda *_: (0,) * nd, pipeline_mode=pl.Buffered(1))


def _gelu_tanh(x):
    return x * (0.5 * (1.0 + jnp.tanh(math.sqrt(2.0 / math.pi) * (x + 0.044715 * (x * x * x)))))


def _layer_norm(r, g, b):
    mu = jnp.mean(r, axis=-1, keepdims=True)
    c = r - mu
    var = jnp.mean(c * c, axis=-1, keepdims=True)
    return c * lax.rsqrt(var + LN_EPS) * g + b


def _dot(a, b):
    return jnp.dot(a, b, preferred_element_type=F32)


def _dot_nt(a, b):
    return lax.dot_general(a, b, (((1,), (1,)), ((), ())), preferred_element_type=F32)


def _rope_table_kernel(pos_ref, inv_ref, cos_ref, sin_ref):
    ang = pos_ref[...] * inv_ref[...]
    lane = lax.broadcasted_iota(jnp.int32, ang.shape, 1)
    first_half = jnp.bitwise_and(lane, HEAD_DIM - 1) < HEAD_DIM // 2
    cos_ref[...] = jnp.cos(ang)
    sin_ref[...] = jnp.where(first_half, -jnp.sin(ang), jnp.sin(ang))


def _rope_tables(positions):
    rows = positions.size
    inv = ROPE_THETA ** (-jnp.arange(0, HEAD_DIM, 2, dtype=F32) / HEAD_DIM)
    inv = jnp.tile(inv, LANES // (HEAD_DIM // 2)).reshape(1, LANES)
    pos = positions.astype(F32).reshape(rows, 1)
    tile = ROW_TILE
    return pl.pallas_call(
        _rope_table_kernel,
        grid=(rows // tile,),
        in_specs=[pl.BlockSpec((tile, 1), lambda i: (i, 0)),
                  pl.BlockSpec((1, LANES), lambda i: (0, 0))],
        out_specs=[pl.BlockSpec((tile, LANES), lambda i: (i, 0))] * 2,
        out_shape=[jax.ShapeDtypeStruct((rows, LANES), F32)] * 2,
        compiler_params=_cparams(1),
        name="rope_tables",
    )(pos, inv)


def _inproj_kernel(x_ref, w_ref, cos_ref, sin_ref, chunk_perm_ref,
                   ussm_ref, upool_ref, q_ref, cmp_ref, ksel_ref, vsel_ref, kwin_ref, vwin_ref,
                   ng_ref, bg_ref):
    xb = x_ref[...].astype(BF16)
    cos = cos_ref[...]
    sin = sin_ref[...]
    lane = lax.broadcasted_iota(jnp.int32, cos.shape, 1)
    first_half = jnp.bitwise_and(lane, HEAD_DIM - 1) < HEAD_DIM // 2
    left = lane < HEAD_DIM

    def proj(c0, width):
        return _dot(xb, w_ref[:, c0:c0 + width])

    def rope(t):
        swapped = jnp.where(first_half, pltpu.roll(t, LANES - HEAD_DIM // 2, 1),
                            pltpu.roll(t, HEAD_DIM // 2, 1))
        return t * cos + swapped * sin

    def dup(t):
        r = pltpu.roll(t, HEAD_DIM, 1)
        return jnp.where(left, t, r), jnp.where(left, r, t)

    def with_ones(t):
        return jnp.where(left, t, 1.0), jnp.where(left, pltpu.roll(t, HEAD_DIM, 1), 1.0)

    ussm_ref[...] = proj(COL_SSM, SSM_WIDTH).astype(BF16)
    upool_ref[...] = proj(COL_POOL, POOL_WIDTH).astype(BF16)
    scale = HEAD_DIM ** -0.5 * LOG2E
    for j in range(ATTN_WIDTH // LANES):
        t = rope(proj(COL_Q + j * LANES, LANES))
        q_ref[:, j * LANES:(j + 1) * LANES] = (t * scale).astype(BF16)

    kv = proj(COL_KV, 6 * KV_WIDTH)
    n_chunks = x_ref.shape[0] // CMP_STRIDE
    left_c = lax.broadcasted_iota(jnp.int32, (n_chunks, LANES), 1) < HEAD_DIM
    for j, roped in enumerate((True, False)):
        t = kv[:, j * LANES:(j + 1) * LANES]
        for h, dup_h in enumerate(dup(rope(t) if roped else t)):
            by_pos = _dot(chunk_perm_ref[...], dup_h.astype(BF16)).astype(BF16)
            for i in range(CMP_STRIDE // 2):
                even = by_pos[(2 * i) * n_chunks:(2 * i + 1) * n_chunks]
                odd = by_pos[(2 * i + 1) * n_chunks:(2 * i + 2) * n_chunks]
                cmp_ref[j, 0, h, :, i * LANES:(i + 1) * LANES] = jnp.where(left_c, even, odd)
    for j, (ref, is_key) in enumerate(((ksel_ref, True), (vsel_ref, False), (kwin_ref, True), (vwin_ref, False))):
        t = kv[:, (2 + j) * LANES:(3 + j) * LANES]
        a, b = dup(rope(t)) if is_key else with_ones(t)
        ref[0, :, 0:LANES] = a.astype(BF16)
        ref[0, :, LANES:2 * LANES] = b.astype(BF16)

    ng_ref[...] = jax.nn.sigmoid(proj(COL_NG, N_KV_HEADS * LANES))
    for k in range(N_BRANCH):
        bg_ref[:, k * D_MODEL:(k + 1) * D_MODEL] = jax.nn.sigmoid(proj(COL_BG + k * D_MODEL, D_MODEL)).astype(BF16)


def _inproj(xr, w, cos, sin, bsz, seq):
    rows = bsz * seq
    tile = ROW_TILE
    n_s = seq // tile
    row = lambda i: (i, 0)
    dup_spec = pl.BlockSpec((1, tile, 2 * LANES), lambda i: (i // n_s, i % n_s, 0))
    chunk_w = CMP_STRIDE * HEAD_DIM
    r = jnp.arange(tile)
    chunk_perm = (r[None, :] == (r[:, None] % (tile // CMP_STRIDE)) * CMP_STRIDE + r[:, None] // (tile // CMP_STRIDE))
    chunk_perm = chunk_perm.astype(BF16)
    out_shape = [
        jax.ShapeDtypeStruct((rows, SSM_WIDTH), BF16),
        jax.ShapeDtypeStruct((rows, POOL_WIDTH), BF16),
        jax.ShapeDtypeStruct((rows, ATTN_WIDTH), BF16),
        jax.ShapeDtypeStruct((2, bsz, N_KV_HEADS, seq // CMP_STRIDE, chunk_w), BF16),
        jax.ShapeDtypeStruct((bsz, seq, 2 * LANES), BF16),
        jax.ShapeDtypeStruct((bsz, seq, 2 * LANES), BF16),
        jax.ShapeDtypeStruct((bsz, seq, 2 * LANES), BF16),
        jax.ShapeDtypeStruct((bsz, seq, 2 * LANES), BF16),
        jax.ShapeDtypeStruct((rows, N_KV_HEADS * LANES), F32),
        jax.ShapeDtypeStruct((rows, N_BRANCH * D_MODEL), BF16),
    ]
    out_specs = [
        pl.BlockSpec((tile, SSM_WIDTH), row),
        pl.BlockSpec((tile, POOL_WIDTH), row),
        pl.BlockSpec((tile, ATTN_WIDTH), row),
        pl.BlockSpec((2, 1, N_KV_HEADS, tile // CMP_STRIDE, chunk_w), lambda i: (0, i // n_s, 0, i % n_s, 0)),
        dup_spec, dup_spec, dup_spec, dup_spec,
        pl.BlockSpec((tile, N_KV_HEADS * LANES), row),
        pl.BlockSpec((tile, N_BRANCH * D_MODEL), row),
    ]
    return pl.pallas_call(
        _inproj_kernel,
        grid=(rows // tile,),
        in_specs=[pl.BlockSpec((tile, D_MODEL), row), _resident(w.shape),
                  pl.BlockSpec((tile, LANES), row), pl.BlockSpec((tile, LANES), row), _resident(chunk_perm.shape)],
        out_specs=out_specs,
        out_shape=out_shape,
        compiler_params=_cparams(1),
        name="in_projection",
    )(xr, w, cos, sin, chunk_perm)


def _time_major_perms(bsz, steps):
    r = jnp.arange(bsz * steps)
    to_tm = (r[None, :] == (r[:, None] % bsz) * steps + r[:, None] // bsz).astype(BF16)
    return to_tm, to_tm.T


def _s5_kernel(u_ref, tm_ref, bm_ref, lre_ref, lim_ref, wb_ref, wc_ref, d_ref, wglu_ref, o_ref,
               bu_ref, st_ref, y_ref):
    bsz, n_steps, _ = u_ref.shape
    n_rows = bsz * n_steps
    n_chunk = wb_ref.shape[0]
    cw = S5_LANE_CHUNK

    @pl.when(pl.program_id(0) == 0)
    def _():
        st_ref[...] = jnp.zeros(st_ref.shape, F32)

    u = _dot(tm_ref[...], u_ref[...].reshape(n_rows, SSM_WIDTH)).astype(BF16)
    for c in range(n_chunk):
        bu_ref[:, c * 2 * cw:(c + 1) * 2 * cw] = _dot(u[:, c * LANES:(c + 1) * LANES], wb_ref[c])

    for c in range(n_chunk):
        re0, im0 = c * 2 * cw, c * 2 * cw + cw
        lre = lre_ref[:, c * cw:(c + 1) * cw]
        lim = lim_ref[:, c * cw:(c + 1) * cw]

        def step(t, carry, re0=re0, im0=im0, lre=lre, lim=lim):
            hre, him = carry
            r0 = pl.multiple_of(t * SUBLANES, SUBLANES)
            nre = lre * hre - lim * him + bu_ref[pl.ds(r0, SUBLANES), re0:re0 + cw]
            nim = lre * him + lim * hre + bu_ref[pl.ds(r0, SUBLANES), im0:im0 + cw]
            bu_ref[pl.ds(r0, SUBLANES), re0:re0 + cw] = nre
            bu_ref[pl.ds(r0, SUBLANES), im0:im0 + cw] = nim
            return nre, nim

        hre, him = lax.fori_loop(0, n_steps, step,
                                 (st_ref[0, :, c * cw:(c + 1) * cw], st_ref[1, :, c * cw:(c + 1) * cw]),
                                 unroll=2)
        st_ref[0, :, c * cw:(c + 1) * cw] = hre
        st_ref[1, :, c * cw:(c + 1) * cw] = him

    for c in range(n_chunk):
        y_ref[:, c * LANES:(c + 1) * LANES] = _dot(bu_ref[:, c * 2 * cw:(c + 1) * 2 * cw].astype(BF16), wc_ref[c])
    y = y_ref[...] + d_ref[...] * u.astype(F32)
    z = _dot(_gelu_tanh(y).astype(BF16), wglu_ref[...])
    out = (z[:, :SSM_WIDTH] * jax.nn.sigmoid(z[:, SSM_WIDTH:])).astype(BF16)
    o_ref[...] = _dot(bm_ref[...], out).astype(BF16).reshape(bsz, n_steps, SSM_WIDTH)


def _s5(u, lre, lim, wb, wc, d, wglu):
    bsz, seq, _ = u.shape
    steps = S5_STEPS
    tile = steps * bsz
    state_w = lre.shape[1]
    to_tm, to_bm = _time_major_perms(bsz, steps)
    blk = pl.BlockSpec((bsz, steps, SSM_WIDTH), lambda i: (0, i, 0))
    return pl.pallas_call(
        _s5_kernel,
        grid=(seq // steps,),
        in_specs=[blk, _resident(to_tm.shape), _resident(to_bm.shape),
                  _resident(lre.shape), _resident(lim.shape), _resident(wb.shape), _resident(wc.shape),
                  _resident(d.shape), _resident(wglu.shape)],
        out_specs=blk,
        out_shape=jax.ShapeDtypeStruct(u.shape, BF16),
        scratch_shapes=[pltpu.VMEM((tile, 2 * state_w), F32),
                        pltpu.VMEM((2, SUBLANES, state_w), F32),
                        pltpu.VMEM((tile, SSM_WIDTH), F32)],
        compiler_params=_cparams(1),
        name="s5_mixer",
    )(u, to_tm, to_bm, lre, lim, wb, wc, d, wglu)


def _pool_kernel(u_ref, tm_ref, bm_ref, w_ref, sc_ref, o_ref, ext_ref, y_ref):
    bsz, n_steps, _ = u_ref.shape
    n_rows = bsz * n_steps
    halo = POOL_HALO * SUBLANES
    i = pl.program_id(0)

    @pl.when(i == 0)
    def _():
        ext_ref[0:halo, :] = jnp.zeros((halo, POOL_WIDTH), F32)

    ext_ref[halo:halo + n_rows, :] = _dot(tm_ref[...], u_ref[...].reshape(n_rows, POOL_WIDTH))
    row = lax.broadcasted_iota(jnp.int32, (n_rows, POOL_GROUP), 0)
    t = i * (n_rows // SUBLANES) + jnp.right_shift(row, 3)
    for gi, w in enumerate(POOL_WINDOWS):
        c0 = gi * POOL_GROUP
        cur = ext_ref[halo:halo + n_rows, c0:c0 + POOL_GROUP]
        acc = cur
        for k in range(1, w):
            acc = acc + ext_ref[halo - k * SUBLANES:halo - k * SUBLANES + n_rows, c0:c0 + POOL_GROUP]
        cnt = jnp.minimum(t + 1, w).astype(F32)
        pooled = acc / cnt - cur
        mixed = _dot(pooled.astype(BF16), w_ref[gi]) * sc_ref[:, c0:c0 + POOL_GROUP]
        y_ref[:, c0:c0 + POOL_GROUP] = mixed.astype(BF16)
    ext_ref[0:halo, :] = ext_ref[n_rows:n_rows + halo, :]
    o_ref[...] = _dot(bm_ref[...], y_ref[...]).astype(BF16).reshape(bsz, n_steps, POOL_WIDTH)


def _pool(u, w, sc):
    bsz, seq, _ = u.shape
    steps = POOL_STEPS
    tile = steps * bsz
    to_tm, to_bm = _time_major_perms(bsz, steps)
    blk = pl.BlockSpec((bsz, steps, POOL_WIDTH), lambda i: (0, i, 0))
    return pl.pallas_call(
        _pool_kernel,
        grid=(seq // steps,),
        in_specs=[blk, _resident(to_tm.shape), _resident(to_bm.shape), _resident(w.shape), _resident(sc.shape)],
        out_specs=blk,
        out_shape=jax.ShapeDtypeStruct(u.shape, BF16),
        scratch_shapes=[pltpu.VMEM((tile + POOL_HALO * SUBLANES, POOL_WIDTH), F32),
                        pltpu.VMEM((tile, POOL_WIDTH), BF16)],
        compiler_params=_cparams(1),
        name="pool_mixer",
    )(u, to_tm, to_bm, w, sc)


def _compress_kernel(z_ref, w1_ref, pos_ref, b1_ref, w2_ref, o_ref):
    half = CMP_STRIDE * HEAD_DIM
    w1 = w1_ref[0]
    ab = _dot(z_ref[0, 0], w1)
    n_chunks = ab.shape[0]
    top = ab[:, :CMP_HIDDEN]
    bottom_next = pltpu.roll(ab[:, CMP_HIDDEN:], n_chunks - 1, 0)
    pos = pos_ref[0].astype(BF16)
    cst = _dot(pos[:, :half], w1[:, :CMP_HIDDEN]) + _dot(pos[:, half:], w1[:, CMP_HIDDEN:])
    hid = _gelu_tanh(top + bottom_next + cst[0:1, :] + b1_ref[0])
    out = _dot(hid.astype(BF16), w2_ref[0])
    lane = lax.broadcasted_iota(jnp.int32, out.shape, 1)
    is_value = pl.program_id(0) == 1
    o_ref[0, 0] = jnp.where((lane >= HEAD_DIM) & is_value, 1.0, out).astype(BF16)


def _compress(z, w1cat, pos, b1, w2dup):
    _, n_bh, n_chunks, width = z.shape
    return pl.pallas_call(
        _compress_kernel,
        grid=(2, n_bh),
        in_specs=[pl.BlockSpec((1, 1, n_chunks, width), lambda j, r: (j, r, 0, 0)),
                  pl.BlockSpec((1,) + w1cat.shape[1:], lambda j, r: (j, 0, 0)),
                  pl.BlockSpec((1,) + pos.shape[1:], lambda j, r: (j, 0, 0)),
                  pl.BlockSpec((1,) + b1.shape[1:], lambda j, r: (j, 0, 0)),
                  pl.BlockSpec((1,) + w2dup.shape[1:], lambda j, r: (j, 0, 0))],
        out_specs=pl.BlockSpec((1, 1, n_chunks, LANES), lambda j, r: (j, r, 0, 0)),
        out_shape=jax.ShapeDtypeStruct((2, n_bh, n_chunks, LANES), BF16),
        compiler_params=_cparams(2),
        name="compress_mlp",
    )(z, w1cat, pos, b1, w2dup)


def _nsa_kernel(q_ref, kc_ref, vc_ref, ks_ref, vs_ref, kw_ref, vw_ref, g_ref, o_ref,
                sa_ref, sb_ref, p_ref, pw_ref, pc_ref, m_ref, a_ref, acc_ref, *, seq):
    tq = q_ref.shape[0]
    tk = sa_ref.shape[1]
    wk = pw_ref.shape[1]
    n_sel = seq // SEL_BLOCK
    n_top = min(SEL_TOP, n_sel)
    n_cmp = (seq - CMP_BLOCK) // CMP_STRIDE + 1
    t0 = pl.program_id(2) * tq
    head_rows = [slice(g * tq, (g + 1) * tq) for g in range(GQA_GROUP)]

    lane = lax.broadcasted_iota(jnp.int32, (tq, LANES), 1)
    left = lane < HEAD_DIM
    zero = jnp.zeros((tq, LANES), BF16)
    pairs = (q_ref[:, 0:LANES], q_ref[:, LANES:2 * LANES])
    q4 = jnp.concatenate([jnp.where(left, pairs[0], zero), jnp.where(left, zero, pairs[0]),
                          jnp.where(left, pairs[1], zero), jnp.where(left, zero, pairs[1])], axis=0)
    tq_col = t0 + lax.broadcasted_iota(jnp.int32, (tq, 1), 0)

    w0 = pl.multiple_of(jnp.maximum(t0 - WINDOW, 0), tq)
    dist = tq_col - (w0 + lax.broadcasted_iota(jnp.int32, (1, wk), 1))
    bias_w = jnp.where((dist >= 0) & (dist < WINDOW), 0.0, NEG)
    s_w = _dot_nt(q4, kw_ref[0, pl.ds(w0, wk), :])
    for rows in head_rows:
        s = s_w[rows] + bias_w
        pw_ref[rows, :] = jnp.exp2(s - jnp.max(s, axis=-1, keepdims=True)).astype(BF16)
    acc_w = _dot(pw_ref[...], vw_ref[0, pl.ds(w0, wk), :])

    n_idx = lax.broadcasted_iota(jnp.int32, (1, kc_ref.shape[2]), 1)
    cmp_valid = (n_idx * CMP_STRIDE + (CMP_BLOCK - 1) <= tq_col) & (n_idx < n_cmp)
    s_c = _dot_nt(q4, kc_ref[0, 0])
    p_sum = None
    for rows in head_rows:
        s = jnp.where(cmp_valid, s_c[rows], NEG)
        e = jnp.where(cmp_valid, jnp.exp2(s - jnp.max(s, axis=-1, keepdims=True)), 0.0)
        l = jnp.sum(e, axis=-1, keepdims=True)
        p = e / jnp.where(l > 0.0, l, 1.0)
        p_sum = p if p_sum is None else p_sum + p
        pc_ref[rows, :] = p.astype(BF16)
    o_cmp = _dot(pc_ref[...], vc_ref[0, 0])

    oj = lax.broadcasted_iota(jnp.int32, (n_sel, kc_ref.shape[2]), 0)
    on = lax.broadcasted_iota(jnp.int32, (n_sel, kc_ref.shape[2]), 1)
    overlap = jnp.clip(jnp.minimum(on * CMP_STRIDE + CMP_BLOCK, oj * SEL_BLOCK + SEL_BLOCK)
                       - jnp.maximum(on * CMP_STRIDE, oj * SEL_BLOCK), 0, CMP_BLOCK).astype(F32) * (1.0 / CMP_BLOCK)
    overlap = jnp.where(on < n_cmp, overlap, 0.0)
    imp = lax.dot_general(overlap, p_sum, (((1,), (1,)), ((), ())), precision=lax.Precision.HIGHEST,
                          preferred_element_type=F32)
    jb = lax.broadcasted_iota(jnp.int32, (n_sel, tq), 0)
    tt = t0 + lax.broadcasted_iota(jnp.int32, (n_sel, tq), 1)
    cur = jnp.right_shift(tt, 6)
    forced = (jb == 0) | (jb == cur) | (jb == cur - 1)
    causal = jb * SEL_BLOCK <= tt
    score = jnp.where(forced, 1e30, jnp.where(causal, imp, NEG))
    rank = jnp.zeros((n_sel, tq), F32)
    for a in range(n_sel):
        sa = score[a:a + 1, :]
        ahead = (sa > score) | ((sa == score) & (jb > a))
        rank = rank + jnp.where(ahead, 1.0, 0.0)
    chosen = (rank < float(n_top)) & causal
    sel_bias = jnp.where(chosen, 0.0, NEG).T.astype(BF16)

    m_ref[...] = jnp.full(m_ref.shape, NEG, F32)
    acc_ref[...] = jnp.zeros(acc_ref.shape, F32)

    def tile_start(kt):
        return pl.multiple_of(jnp.minimum(kt * tk, seq - tk), tk)

    def scores(kt, dst_ref):
        k0 = kt * tk
        expand = (lax.broadcasted_iota(jnp.int32, (n_sel, tk), 0)
                  == jnp.right_shift(k0 + lax.broadcasted_iota(jnp.int32, (n_sel, tk), 1), 6))
        bias = _dot(sel_bias, jnp.where(expand, 1.0, 0.0).astype(BF16))
        bias = jnp.where(k0 + lax.broadcasted_iota(jnp.int32, (1, tk), 1) <= tq_col, bias, NEG)
        s = _dot_nt(q4, ks_ref[0, pl.ds(tile_start(kt), tk), :])
        for rows in head_rows:
            dst_ref[rows, :] = s[rows] + bias

    def consume(kt, src_ref):
        for rows in head_rows:
            s = src_ref[rows, :]
            m_old = m_ref[rows, :]
            m_new = jnp.maximum(m_old, jnp.max(s, axis=-1, keepdims=True))
            a_ref[rows, :] = jnp.exp2(m_old - m_new)
            m_ref[rows, :] = m_new
            p_ref[rows, :] = jnp.exp2(s - jnp.concatenate([m_new] * (tk // LANES), axis=1)).astype(BF16)
        acc_ref[...] = a_ref[...] * acc_ref[...] + _dot(p_ref[...], vs_ref[0, pl.ds(tile_start(kt), tk), :])

    def pair_step(j, carry):
        scores(2 * j + 1, sb_ref)
        consume(2 * j, sa_ref)
        scores(2 * j + 2, sa_ref)
        consume(2 * j + 1, sb_ref)
        return carry

    scores(0, sa_ref)
    n_kt = (t0 + tq + tk - 1) // tk
    lax.fori_loop(0, (n_kt + 1) // 2, pair_step, 0)
    acc_s = acc_ref[...]

    g = g_ref[...]
    g_hi = g.astype(BF16)
    g_lo = (g - g_hi.astype(F32)).astype(BF16)
    src = lax.broadcasted_iota(jnp.int32, (LANES, LANES), 0)
    dst_head = jnp.right_shift(lax.broadcasted_iota(jnp.int32, (LANES, LANES), 1), 6)

    def pair_tile(acc, pair, normalise):
        a, b = acc[head_rows[2 * pair]], acc[head_rows[2 * pair + 1]]
        num = jnp.where(left, a, pltpu.roll(b, HEAD_DIM, 1))
        return num / jnp.where(left, pltpu.roll(a, HEAD_DIM, 1), b) if normalise else num

    for pair in range(GQA_GROUP // 2):
        out = None
        for br, (acc, normalise) in enumerate(((o_cmp, False), (acc_s, True), (acc_w, True))):
            pick = jnp.where(src == br * GQA_GROUP + 2 * pair + dst_head, 1.0, 0.0).astype(BF16)
            gate = _dot(g_hi, pick) + _dot(g_lo, pick)
            term = gate * pair_tile(acc, pair, normalise)
            out = term if out is None else out + term
        o_ref[:, pair * LANES:(pair + 1) * LANES] = out.astype(BF16)


def _nsa(q, cmp_kv, ksel, vsel, kwin, vwin, ng, bsz, seq):
    tq = Q_TILE
    n_q = seq // tq
    n_chunks = cmp_kv.shape[2]
    rows = GQA_GROUP * tq
    qrow = lambda b, h, i: (b * n_q + i, h)
    kv_spec = pl.BlockSpec((1, seq, LANES), lambda b, h, i: (b, 0, h))
    return pl.pallas_call(
        functools.partial(_nsa_kernel, seq=seq),
        grid=(bsz, N_KV_HEADS, n_q),
        in_specs=[pl.BlockSpec((tq, 2 * LANES), qrow),
                  pl.BlockSpec((1, 1, n_chunks, LANES), lambda b, h, i: (0, b * N_KV_HEADS + h, 0, 0)),
                  pl.BlockSpec((1, 1, n_chunks, LANES), lambda b, h, i: (1, b * N_KV_HEADS + h, 0, 0)),
                  kv_spec, kv_spec, kv_spec, kv_spec,
                  pl.BlockSpec((tq, LANES), qrow)],
        out_specs=pl.BlockSpec((tq, 2 * LANES), qrow),
        out_shape=jax.ShapeDtypeStruct((bsz * seq, ATTN_WIDTH), BF16),
        scratch_shapes=[pltpu.VMEM((rows, K_TILE), F32), pltpu.VMEM((rows, K_TILE), F32),
                        pltpu.VMEM((rows, K_TILE), BF16), pltpu.VMEM((rows, WINDOW + tq), BF16),
                        pltpu.VMEM((rows, n_chunks), BF16),
                        pltpu.VMEM((rows, LANES), F32), pltpu.VMEM((rows, LANES), F32),
                        pltpu.VMEM((rows, LANES), F32)],
        compiler_params=_cparams(3),
        name="nsa_attention",
    )(q, cmp_kv, cmp_kv, ksel, vsel, kwin, vwin, ng)


def _merge_kernel(x_ref, ys_ref, yp_ref, yn_ref, bg_ref, wb_ref, wo_ref, g_ref, b_ref, o_ref, *, alpha):
    merged = None
    for k, y_ref in enumerate((ys_ref, yp_ref, yn_ref)):
        term = bg_ref[:, k * D_MODEL:(k + 1) * D_MODEL].astype(F32) * _dot(y_ref[...], wb_ref[k])
        merged = term if merged is None else merged + term
    r = alpha * x_ref[...] + _dot(merged.astype(BF16), wo_ref[...])
    o_ref[...] = _layer_norm(r, g_ref[...], b_ref[...])


def _merge(xr, ys, yp, yn, bg, wb, wo, g, b, alpha):
    rows = xr.shape[0]
    tile = ROW_TILE
    row = lambda i: (i, 0)
    return pl.pallas_call(
        functools.partial(_merge_kernel, alpha=alpha),
        grid=(rows // tile,),
        in_specs=[pl.BlockSpec((tile, D_MODEL), row),
                  pl.BlockSpec((tile, SSM_WIDTH), row), pl.BlockSpec((tile, POOL_WIDTH), row),
                  pl.BlockSpec((tile, ATTN_WIDTH), row), pl.BlockSpec((tile, N_BRANCH * D_MODEL), row),
                  _resident(wb.shape), _resident(wo.shape), _resident(g.shape), _resident(b.shape)],
        out_specs=pl.BlockSpec((tile, D_MODEL), row),
        out_shape=jax.ShapeDtypeStruct((rows, D_MODEL), F32),
        compiler_params=_cparams(1),
        name="branch_merge",
    )(xr, ys, yp, yn, bg, wb, wo, g, b)


def _ffn_kernel(x_ref, wi_ref, wo_ref, g_ref, b_ref, o_ref, *, alpha):
    x = x_ref[...]
    xb = x.astype(BF16)
    acc = None
    for c in range(FF_HIDDEN // FF_CHUNK):
        c0 = c * FF_CHUNK
        hg = _dot(xb, wi_ref[:, c0:c0 + FF_CHUNK])
        hu = _dot(xb, wi_ref[:, FF_HIDDEN + c0:FF_HIDDEN + c0 + FF_CHUNK])
        act = (hg * jax.nn.sigmoid(hg) * hu).astype(BF16)
        part = _dot(act, wo_ref[c0:c0 + FF_CHUNK, :])
        acc = part if acc is None else acc + part
    o_ref[...] = _layer_norm(alpha * x + acc, g_ref[...], b_ref[...])


def _ffn(xr, wi, wo, g, b, alpha):
    rows = xr.shape[0]
    tile = ROW_TILE
    row = lambda i: (i, 0)
    return pl.pallas_call(
        functools.partial(_ffn_kernel, alpha=alpha),
        grid=(rows // tile,),
        in_specs=[pl.BlockSpec((tile, D_MODEL), row), _resident(wi.shape), _resident(wo.shape),
                  _resident(g.shape), _resident(b.shape)],
        out_specs=pl.BlockSpec((tile, D_MODEL), row),
        out_shape=jax.ShapeDtypeStruct((rows, D_MODEL), F32),
        compiler_params=_cparams(1),
        name="swiglu_ffn",
    )(xr, wi, wo, g, b)


def _pack_s5(a_re, a_im, log_dt, b_re, b_im, c_re, c_im):
    depth = a_re.shape[0]
    a_re, a_im = a_re.astype(F32), a_im.astype(F32)
    dt = jnp.exp(log_dt.astype(F32))[..., None]
    mag = jnp.exp(a_re * dt)
    lbar_re, lbar_im = mag * jnp.cos(a_im * dt), mag * jnp.sin(a_im * dt)
    den = a_re * a_re + a_im * a_im
    coef_re = ((lbar_re - 1.0) * a_re + lbar_im * a_im) / den
    coef_im = (lbar_im * a_re - (lbar_re - 1.0) * a_im) / den
    b_re, b_im = b_re.astype(F32), b_im.astype(F32)
    bbar_re = coef_re[..., None] * b_re - coef_im[..., None] * b_im
    bbar_im = coef_re[..., None] * b_im + coef_im[..., None] * b_re
    gpc = S5_LANE_CHUNK // SSM_STATE
    n_chunk = SSM_GROUPS // gpc
    eye = jnp.eye(gpc, dtype=F32)

    def b_block(part):
        v = part.transpose(0, 1, 3, 2).reshape(depth, n_chunk, gpc, SSM_GROUP, SSM_STATE)
        return jnp.einsum('xy,dqxcp->dqxcyp', eye, v).reshape(depth, n_chunk, gpc * SSM_GROUP, gpc * SSM_STATE)

    def c_block(part):
        v = part.reshape(depth, n_chunk, gpc, SSM_GROUP, SSM_STATE)
        return jnp.einsum('xy,dqxcp->dqxpyc', eye, v).reshape(depth, n_chunk, gpc * SSM_STATE, gpc * SSM_GROUP)

    wb = jnp.concatenate([b_block(bbar_re), b_block(bbar_im)], axis=-1).astype(BF16)
    wc = jnp.concatenate([c_block(c_re.astype(F32)), c_block(-c_im.astype(F32))], axis=-2).astype(BF16)
    state_w = SSM_GROUPS * SSM_STATE
    lre = jnp.broadcast_to(lbar_re.reshape(depth, 1, state_w), (depth, SUBLANES, state_w))
    lim = jnp.broadcast_to(lbar_im.reshape(depth, 1, state_w), (depth, SUBLANES, state_w))
    return lre, lim, wb, wc


def kernel(x, positions, w_in, ssm_a_re, ssm_a_im, ssm_log_dt, ssm_b_re, ssm_b_im, ssm_c_re, ssm_c_im,
           ssm_d, ssm_w_glu, pool_w, pool_scale, cmp_pos, cmp_w1, cmp_b1, cmp_w2,
           w_branch, w_out, ln_g, ln_b, ffn_w_in, ffn_w_out):
    bsz, seq, _ = x.shape
    depth = w_in.shape[0]
    rows = bsz * seq
    assert bsz == SUBLANES and seq % ROW_TILE == 0 and seq >= WINDOW + Q_TILE
    alpha = (2 * depth) ** 0.25

    assert w_in.shape[-1] == IN_RAW
    lre, lim, s5_wb, s5_wc = _pack_s5(ssm_a_re, ssm_a_im, ssm_log_dt, ssm_b_re, ssm_b_im, ssm_c_re, ssm_c_im)
    s5_d = ssm_d.astype(F32).reshape(depth, 1, SSM_WIDTH)
    pool_wb = pool_w.astype(BF16)
    pool_sc = pool_scale.astype(F32).reshape(depth, 1, POOL_WIDTH)
    half = CMP_STRIDE * HEAD_DIM
    cmp_posr = jnp.broadcast_to(cmp_pos.astype(F32).reshape(depth, 2, 1, CMP_BLOCK * HEAD_DIM),
                                (depth, 2, SUBLANES, CMP_BLOCK * HEAD_DIM))
    cmp_b1r = cmp_b1.astype(F32).reshape(depth, 2, 1, CMP_HIDDEN)
    cmp_w2dup = jnp.concatenate([cmp_w2, cmp_w2 * jnp.array([1.0, 0.0], cmp_w2.dtype).reshape(1, 2, 1, 1)],
                                axis=-1).astype(BF16)
    lng = ln_g.astype(F32).reshape(depth, 2, 1, D_MODEL)
    lnb = ln_b.astype(F32).reshape(depth, 2, 1, D_MODEL)

    def pack_w_in(w):
        w_gate = w[:, COL_NG:COL_NG + N_GATE].reshape(D_MODEL, 3, N_KV_HEADS, GQA_GROUP)
        w_gate = w_gate.transpose(0, 2, 1, 3).reshape(D_MODEL, N_KV_HEADS, 3 * GQA_GROUP)
        w_gate = jnp.pad(w_gate, ((0, 0), (0, 0), (0, LANES - 3 * GQA_GROUP))).reshape(D_MODEL, N_KV_HEADS * LANES)
        return jnp.concatenate([w[:, :COL_NG], w_gate, w[:, COL_NG + N_GATE:]], axis=-1).astype(BF16)

    cos, sin = _rope_tables(positions)
    xr = x.astype(F32).reshape(rows, D_MODEL)
    for l in range(depth):
        (u_ssm, u_pool, q, cmp_in, ksel, vsel, kwin, vwin, ng, bg) = _inproj(xr, pack_w_in(w_in[l]), cos, sin, bsz, seq)
        y_ssm = _s5(u_ssm.reshape(bsz, seq, SSM_WIDTH), lre[l], lim[l], s5_wb[l], s5_wc[l], s5_d[l],
                    ssm_w_glu[l].astype(BF16))
        y_pool = _pool(u_pool.reshape(bsz, seq, POOL_WIDTH), pool_wb[l], pool_sc[l])
        cmp_z = cmp_in.reshape(2, bsz * N_KV_HEADS, seq // CMP_STRIDE, CMP_STRIDE * HEAD_DIM)
        cmp_w1cat = jnp.concatenate([cmp_w1[l, :, :half, :], cmp_w1[l, :, half:, :]], axis=-1).astype(BF16)
        cmp_kv = _compress(cmp_z, cmp_w1cat, cmp_posr[l], cmp_b1r[l], cmp_w2dup[l])
        y_nsa = _nsa(q, cmp_kv, ksel, vsel, kwin, vwin, ng, bsz, seq)
        x1 = _merge(xr, y_ssm.reshape(rows, SSM_WIDTH), y_pool.reshape(rows, POOL_WIDTH), y_nsa, bg,
                    w_branch[l].astype(BF16), w_out[l].astype(BF16), lng[l, 0], lnb[l, 0], alpha)
        xr = _ffn(x1, ffn_w_in[l].astype(BF16), ffn_w_out[l].astype(BF16), lng[l, 1], lnb[l, 1], alpha)
    return xr.reshape(bsz, seq, D_MODEL).astype(x.dtype)
```

```python
import functools
import math

import jax
import jax.numpy as jnp
from jax import lax
from jax.experimental import pallas as pl
from jax.experimental.pallas import tpu as pltpu

F32 = jnp.float32
BF16 = jnp.bfloat16

D_MODEL = 1024
SSM_WIDTH = 512
SSM_GROUP = 16
SSM_GROUPS = 32
SSM_STATE = 64
POOL_WIDTH = 512
POOL_WINDOWS = (2, 4, 8, 16)
POOL_GROUP = 128
HEAD_DIM = 64
N_HEADS = 8
N_KV_HEADS = 2
GQA_GROUP = 4
ATTN_WIDTH = 512
KV_WIDTH = 128
N_BRANCH = 3
CMP_BLOCK = 32
CMP_STRIDE = 16
CMP_HIDDEN = 256
SEL_BLOCK = 64
SEL_TOP = 16
WINDOW = 512
ROPE_THETA = 10000.0
FF_HIDDEN = 2816
LN_EPS = 1e-5
NEG = -1e30
N_GATE = 3 * N_HEADS
IN_RAW = 3 * 512 + 6 * KV_WIDTH + N_GATE + N_BRANCH * D_MODEL

LANES = 128
SUBLANES = 8
VMEM_LIMIT_BYTES = 56 * 1024 * 1024

COL_SSM = 0
COL_POOL = 512
COL_Q = 1024
COL_KV = 1536
COL_NG = 2304
COL_BG = COL_NG + N_KV_HEADS * LANES
IN_PACKED = COL_BG + N_BRANCH * D_MODEL
LOG2E = 1.4426950408889634

ROW_TILE = 512
S5_STEPS = 64
S5_LANE_CHUNK = 512
POOL_STEPS = 64
POOL_HALO = 16
Q_TILE = 256
K_TILE = 256
FF_CHUNK = 256


def _cparams(n_axes):
    return pltpu.CompilerParams(dimension_semantics=("arbitrary",) * n_axes,
                                vmem_limit_bytes=VMEM_LIMIT_BYTES)


def _resident(shape):
    nd = len(shape)
    return pl.BlockSpec(shape, lambda *_: (0,) * nd, pipeline_mode=pl.Buffered(1))


def _resident_layer(stacked, layer):
    nd = stacked.ndim
    return pl.BlockSpec((None,) + stacked.shape[1:], lambda *_: (layer,) + (0,) * (nd - 1),
                        pipeline_mode=pl.Buffered(1))


def _gelu_tanh(x):
    return x * (0.5 * (1.0 + jnp.tanh(math.sqrt(2.0 / math.pi) * (x + 0.044715 * (x * x * x)))))


def _layer_norm(r, g, b):
    mu = jnp.mean(r, axis=-1, keepdims=True)
    c = r - mu
    var = jnp.mean(c * c, axis=-1, keepdims=True)
    return c * lax.rsqrt(var + LN_EPS) * g + b


def _dot(a, b):
    return jnp.dot(a, b, preferred_element_type=F32)


def _dot_nt(a, b):
    return lax.dot_general(a, b, (((1,), (1,)), ((), ())), preferred_element_type=F32)


def _rope_table_kernel(pos_ref, inv_ref, cos_ref, sin_ref):
    ang = pos_ref[...] * inv_ref[...]
    lane = lax.broadcasted_iota(jnp.int32, ang.shape, 1)
    first_half = jnp.bitwise_and(lane, HEAD_DIM - 1) < HEAD_DIM // 2
    cos_ref[...] = jnp.cos(ang)
    sin_ref[...] = jnp.where(first_half, -jnp.sin(ang), jnp.sin(ang))


def _rope_tables(positions):
    rows = positions.size
    inv = ROPE_THETA ** (-jnp.arange(0, HEAD_DIM, 2, dtype=F32) / HEAD_DIM)
    inv = jnp.tile(inv, LANES // (HEAD_DIM // 2)).reshape(1, LANES)
    pos = positions.astype(F32).reshape(rows, 1)
    tile = ROW_TILE
    return pl.pallas_call(
        _rope_table_kernel,
        grid=(rows // tile,),
        in_specs=[pl.BlockSpec((tile, 1), lambda i: (i, 0)),
                  pl.BlockSpec((1, LANES), lambda i: (0, 0))],
        out_specs=[pl.BlockSpec((tile, LANES), lambda i: (i, 0))] * 2,
        out_shape=[jax.ShapeDtypeStruct((rows, LANES), F32)] * 2,
        compiler_params=_cparams(1),
        name="rope_tables",
    )(pos, inv)


def _inproj_kernel(x_ref, w_ref, cos_ref, sin_ref, chunk_perm_ref,
                   ussm_ref, upool_ref, q_ref, cmp_ref, ksel_ref, vsel_ref, kwin_ref, vwin_ref,
                   ng_ref, bg_ref):
    xb = x_ref[...].astype(BF16)
    cos = cos_ref[...]
    sin = sin_ref[...]
    lane = lax.broadcasted_iota(jnp.int32, cos.shape, 1)
    first_half = jnp.bitwise_and(lane, HEAD_DIM - 1) < HEAD_DIM // 2
    left = lane < HEAD_DIM

    def proj(c0, width):
        return _dot(xb, w_ref[:, c0:c0 + width])

    def rope(t):
        swapped = jnp.where(first_half, pltpu.roll(t, LANES - HEAD_DIM // 2, 1),
                            pltpu.roll(t, HEAD_DIM // 2, 1))
        return t * cos + swapped * sin

    def dup(t):
        r = pltpu.roll(t, HEAD_DIM, 1)
        return jnp.where(left, t, r), jnp.where(left, r, t)

    def with_ones(t):
        return jnp.where(left, t, 1.0), jnp.where(left, pltpu.roll(t, HEAD_DIM, 1), 1.0)

    ussm_ref[...] = proj(COL_SSM, SSM_WIDTH).astype(BF16)
    upool_ref[...] = proj(COL_POOL, POOL_WIDTH).astype(BF16)
    scale = HEAD_DIM ** -0.5 * LOG2E
    for j in range(ATTN_WIDTH // LANES):
        t = rope(proj(COL_Q + j * LANES, LANES))
        q_ref[:, j * LANES:(j + 1) * LANES] = (t * scale).astype(BF16)

    kv = proj(COL_KV, 6 * KV_WIDTH)
    n_chunks = x_ref.shape[0] // CMP_STRIDE
    left_c = lax.broadcasted_iota(jnp.int32, (n_chunks, LANES), 1) < HEAD_DIM
    for j, roped in enumerate((True, False)):
        t = kv[:, j * LANES:(j + 1) * LANES]
        t = (rope(t) if roped else t).astype(BF16)
        by_pos = _dot(chunk_perm_ref[...], t)
        for h, dup_h in enumerate(dup(by_pos)):
            for i in range(CMP_STRIDE // 2):
                even = dup_h[(2 * i) * n_chunks:(2 * i + 1) * n_chunks]
                odd = dup_h[(2 * i + 1) * n_chunks:(2 * i + 2) * n_chunks]
                cmp_ref[j, 0, h, :, i * LANES:(i + 1) * LANES] = jnp.where(left_c, even, odd).astype(BF16)
    for j, (ref, is_key) in enumerate(((ksel_ref, True), (vsel_ref, False), (kwin_ref, True), (vwin_ref, False))):
        t = kv[:, (2 + j) * LANES:(3 + j) * LANES]
        a, b = dup(rope(t)) if is_key else with_ones(t)
        ref[0, :, 0:LANES] = a.astype(BF16)
        ref[0, :, LANES:2 * LANES] = b.astype(BF16)

    ng_ref[...] = jax.nn.sigmoid(proj(COL_NG, N_KV_HEADS * LANES))
    for k in range(N_BRANCH):
        bg_ref[:, k * D_MODEL:(k + 1) * D_MODEL] = jax.nn.sigmoid(proj(COL_BG + k * D_MODEL, D_MODEL)).astype(BF16)


def _inproj(xr, w, cos, sin, bsz, seq):
    rows = bsz * seq
    tile = ROW_TILE
    n_s = seq // tile
    row = lambda i: (i, 0)
    dup_spec = pl.BlockSpec((1, tile, 2 * LANES), lambda i: (i // n_s, i % n_s, 0))
    chunk_w = CMP_STRIDE * HEAD_DIM
    r = jnp.arange(tile)
    chunk_perm = (r[None, :] == (r[:, None] % (tile // CMP_STRIDE)) * CMP_STRIDE + r[:, None] // (tile // CMP_STRIDE))
    chunk_perm = chunk_perm.astype(BF16)
    out_shape = [
        jax.ShapeDtypeStruct((rows, SSM_WIDTH), BF16),
        jax.ShapeDtypeStruct((rows, POOL_WIDTH), BF16),
        jax.ShapeDtypeStruct((rows, ATTN_WIDTH), BF16),
        jax.ShapeDtypeStruct((2, bsz, N_KV_HEADS, seq // CMP_STRIDE, chunk_w), BF16),
        jax.ShapeDtypeStruct((bsz, seq, 2 * LANES), BF16),
        jax.ShapeDtypeStruct((bsz, seq, 2 * LANES), BF16),
        jax.ShapeDtypeStruct((bsz, seq, 2 * LANES), BF16),
        jax.ShapeDtypeStruct((bsz, seq, 2 * LANES), BF16),
        jax.ShapeDtypeStruct((rows, N_KV_HEADS * LANES), F32),
        jax.ShapeDtypeStruct((rows, N_BRANCH * D_MODEL), BF16),
    ]
    out_specs = [
        pl.BlockSpec((tile, SSM_WIDTH), row),
        pl.BlockSpec((tile, POOL_WIDTH), row),
        pl.BlockSpec((tile, ATTN_WIDTH), row),
        pl.BlockSpec((2, 1, N_KV_HEADS, tile // CMP_STRIDE, chunk_w), lambda i: (0, i // n_s, 0, i % n_s, 0)),
        dup_spec, dup_spec, dup_spec, dup_spec,
        pl.BlockSpec((tile, N_KV_HEADS * LANES), row),
        pl.BlockSpec((tile, N_BRANCH * D_MODEL), row),
    ]
    return pl.pallas_call(
        _inproj_kernel,
        grid=(rows // tile,),
        in_specs=[pl.BlockSpec((tile, D_MODEL), row), _resident(w.shape),
                  pl.BlockSpec((tile, LANES), row), pl.BlockSpec((tile, LANES), row), _resident(chunk_perm.shape)],
        out_specs=out_specs,
        out_shape=out_shape,
        compiler_params=_cparams(1),
        name="in_projection",
    )(xr, w, cos, sin, chunk_perm)


def _time_major_perms(bsz, steps):
    r = jnp.arange(bsz * steps)
    to_tm = (r[None, :] == (r[:, None] % bsz) * steps + r[:, None] // bsz).astype(BF16)
    return to_tm, to_tm.T


def _s5_kernel(u_ref, tm_ref, bm_ref, lre_ref, lim_ref, wb_ref, wc_ref, d_ref, wglu_ref, o_ref,
               bu_ref, st_ref, y_ref):
    bsz, n_steps, _ = u_ref.shape
    n_rows = bsz * n_steps
    n_chunk = wb_ref.shape[0]
    cw = S5_LANE_CHUNK

    @pl.when(pl.program_id(0) == 0)
    def _():
        st_ref[...] = jnp.zeros(st_ref.shape, F32)

    u = _dot(tm_ref[...], u_ref[...].reshape(n_rows, SSM_WIDTH)).astype(BF16)
    for c in range(n_chunk):
        bu_ref[:, c * 2 * cw:(c + 1) * 2 * cw] = _dot(u[:, c * LANES:(c + 1) * LANES], wb_ref[c])

    for c in range(n_chunk):
        re0, im0 = c * 2 * cw, c * 2 * cw + cw
        lre = lre_ref[:, c * cw:(c + 1) * cw]
        lim = lim_ref[:, c * cw:(c + 1) * cw]

        def step(t, carry, re0=re0, im0=im0, lre=lre, lim=lim):
            hre, him = carry
            r0 = pl.multiple_of(t * SUBLANES, SUBLANES)
            nre = lre * hre - lim * him + bu_ref[pl.ds(r0, SUBLANES), re0:re0 + cw]
            nim = lre * him + lim * hre + bu_ref[pl.ds(r0, SUBLANES), im0:im0 + cw]
            bu_ref[pl.ds(r0, SUBLANES), re0:re0 + cw] = nre
            bu_ref[pl.ds(r0, SUBLANES), im0:im0 + cw] = nim
            return nre, nim

        hre, him = lax.fori_loop(0, n_steps, step,
                                 (st_ref[0, :, c * cw:(c + 1) * cw], st_ref[1, :, c * cw:(c + 1) * cw]),
                                 unroll=2)
        st_ref[0, :, c * cw:(c + 1) * cw] = hre
        st_ref[1, :, c * cw:(c + 1) * cw] = him

    for c in range(n_chunk):
        y_ref[:, c * LANES:(c + 1) * LANES] = _dot(bu_ref[:, c * 2 * cw:(c + 1) * 2 * cw].astype(BF16), wc_ref[c])
    y = y_ref[...] + d_ref[...] * u.astype(F32)
    z = _dot(_gelu_tanh(y).astype(BF16), wglu_ref[...].astype(BF16))
    out = (z[:, :SSM_WIDTH] * jax.nn.sigmoid(z[:, SSM_WIDTH:])).astype(BF16)
    o_ref[...] = _dot(bm_ref[...], out).astype(BF16).reshape(bsz, n_steps, SSM_WIDTH)


def _s5(u, lre, lim, wb, wc, d, wglu_stack, layer):
    bsz, seq, _ = u.shape
    steps = S5_STEPS
    tile = steps * bsz
    state_w = lre.shape[1]
    to_tm, to_bm = _time_major_perms(bsz, steps)
    blk = pl.BlockSpec((bsz, steps, SSM_WIDTH), lambda i: (0, i, 0))
    return pl.pallas_call(
        _s5_kernel,
        grid=(seq // steps,),
        in_specs=[blk, _resident(to_tm.shape), _resident(to_bm.shape),
                  _resident(lre.shape), _resident(lim.shape), _resident(wb.shape), _resident(wc.shape),
                  _resident(d.shape), _resident_layer(wglu_stack, layer)],
        out_specs=blk,
        out_shape=jax.ShapeDtypeStruct(u.shape, BF16),
        scratch_shapes=[pltpu.VMEM((tile, 2 * state_w), F32),
                        pltpu.VMEM((2, SUBLANES, state_w), F32),
                        pltpu.VMEM((tile, SSM_WIDTH), F32)],
        compiler_params=_cparams(1),
        name="s5_mixer",
    )(u, to_tm, to_bm, lre, lim, wb, wc, d, wglu_stack)


def _pool_kernel(u_ref, tm_ref, bm_ref, w_ref, sc_ref, o_ref, ext_ref, y_ref):
    bsz, n_steps, _ = u_ref.shape
    n_rows = bsz * n_steps
    halo = POOL_HALO * SUBLANES
    i = pl.program_id(0)

    @pl.when(i == 0)
    def _():
        ext_ref[0:halo, :] = jnp.zeros((halo, POOL_WIDTH), F32)

    ext_ref[halo:halo + n_rows, :] = _dot(tm_ref[...], u_ref[...].reshape(n_rows, POOL_WIDTH))
    row = lax.broadcasted_iota(jnp.int32, (n_rows, POOL_GROUP), 0)
    t = i * (n_rows // SUBLANES) + jnp.right_shift(row, 3)
    for gi, w in enumerate(POOL_WINDOWS):
        c0 = gi * POOL_GROUP
        cur = ext_ref[halo:halo + n_rows, c0:c0 + POOL_GROUP]
        acc = cur
        for k in range(1, w):
            acc = acc + ext_ref[halo - k * SUBLANES:halo - k * SUBLANES + n_rows, c0:c0 + POOL_GROUP]
        cnt = jnp.minimum(t + 1, w).astype(F32)
        pooled = acc / cnt - cur
        mixed = _dot(pooled.astype(BF16), w_ref[gi]) * sc_ref[:, c0:c0 + POOL_GROUP]
        y_ref[:, c0:c0 + POOL_GROUP] = mixed.astype(BF16)
    ext_ref[0:halo, :] = ext_ref[n_rows:n_rows + halo, :]
    o_ref[...] = _dot(bm_ref[...], y_ref[...]).astype(BF16).reshape(bsz, n_steps, POOL_WIDTH)


def _pool(u, w, sc):
    bsz, seq, _ = u.shape
    steps = POOL_STEPS
    tile = steps * bsz
    to_tm, to_bm = _time_major_perms(bsz, steps)
    blk = pl.BlockSpec((bsz, steps, POOL_WIDTH), lambda i: (0, i, 0))
    return pl.pallas_call(
        _pool_kernel,
        grid=(seq // steps,),
        in_specs=[blk, _resident(to_tm.shape), _resident(to_bm.shape), _resident(w.shape), _resident(sc.shape)],
        out_specs=blk,
        out_shape=jax.ShapeDtypeStruct(u.shape, BF16),
        scratch_shapes=[pltpu.VMEM((tile + POOL_HALO * SUBLANES, POOL_WIDTH), F32),
                        pltpu.VMEM((tile, POOL_WIDTH), BF16)],
        compiler_params=_cparams(1),
        name="pool_mixer",
    )(u, to_tm, to_bm, w, sc)


def _compress_kernel(z_ref, w1_ref, pos_ref, b1_ref, w2_ref, o_ref):
    half = CMP_STRIDE * HEAD_DIM
    _, n_bh, n_chunks, width = z_ref.shape
    z = z_ref[0].reshape(n_bh * n_chunks, width)
    w_top = w1_ref[0, :half, :].astype(BF16)
    w_bot = w1_ref[0, half:, :].astype(BF16)
    bottom_next = pltpu.roll(_dot(z, w_bot), n_bh * n_chunks - 1, 0)
    pos = pos_ref[0].astype(BF16)
    cst = _dot(pos[:, :half], w_top) + _dot(pos[:, half:], w_bot)
    hid = _gelu_tanh(_dot(z, w_top) + bottom_next + cst[0:1, :] + b1_ref[0])
    out = _dot(hid.astype(BF16), w2_ref[0])
    lane = lax.broadcasted_iota(jnp.int32, out.shape, 1)
    is_value = pl.program_id(0) == 1
    out = jnp.where((lane >= HEAD_DIM) & is_value, 1.0, out).astype(BF16)
    o_ref[0] = out.reshape(n_bh, n_chunks, LANES)


def _compress(z, w1_stack, layer, pos, b1, w2dup):
    _, n_bh, n_chunks, width = z.shape
    return pl.pallas_call(
        _compress_kernel,
        grid=(2,),
        in_specs=[pl.BlockSpec((1, n_bh, n_chunks, width), lambda j: (j, 0, 0, 0)),
                  pl.BlockSpec((None, 1) + w1_stack.shape[2:], lambda j: (layer, j, 0, 0)),
                  pl.BlockSpec((1,) + pos.shape[1:], lambda j: (j, 0, 0)),
                  pl.BlockSpec((1,) + b1.shape[1:], lambda j: (j, 0, 0)),
                  pl.BlockSpec((1,) + w2dup.shape[1:], lambda j: (j, 0, 0))],
        out_specs=pl.BlockSpec((1, n_bh, n_chunks, LANES), lambda j: (j, 0, 0, 0)),
        out_shape=jax.ShapeDtypeStruct((2, n_bh, n_chunks, LANES), BF16),
        compiler_params=_cparams(1),
        name="compress_mlp",
    )(z, w1_stack, pos, b1, w2dup)


def _nsa_kernel(q_ref, kc_ref, vc_ref, ks_ref, vs_ref, kw_ref, vw_ref, g_ref, o_ref,
                sel_state, pw_ref, pc_ref, *, seq):
    tq = q_ref.shape[0]
    tk = sel_state[0].shape[1]
    n_sel = seq // SEL_BLOCK
    n_top = min(SEL_TOP, n_sel)
    n_cmp = (seq - CMP_BLOCK) // CMP_STRIDE + 1
    t0 = pl.program_id(2) * tq
    head_rows = [slice(g * tq, (g + 1) * tq) for g in range(GQA_GROUP)]

    lane = lax.broadcasted_iota(jnp.int32, (tq, LANES), 1)
    left = lane < HEAD_DIM
    zero = jnp.zeros((tq, LANES), BF16)
    pairs = (q_ref[:, 0:LANES], q_ref[:, LANES:2 * LANES])
    q4 = jnp.concatenate([jnp.where(left, pairs[0], zero), jnp.where(left, zero, pairs[0]),
                          jnp.where(left, pairs[1], zero), jnp.where(left, zero, pairs[1])], axis=0)
    tq_col = t0 + lax.broadcasted_iota(jnp.int32, (tq, 1), 0)

    def online_branch(state, k_ref, v_ref, tile_start, tile_bias):
        buf_a, buf_b, p_ref, m_ref, a_ref, acc_ref = state
        m_ref[...] = jnp.full(m_ref.shape, NEG, F32)
        acc_ref[...] = jnp.zeros(acc_ref.shape, F32)

        def scores(kt, dst_ref):
            bias = tile_bias(kt)
            s = _dot_nt(q4, k_ref[0, pl.ds(tile_start(kt), tk), :])
            for rows in head_rows:
                dst_ref[rows, :] = s[rows] + bias

        def consume(kt, src_ref):
            for rows in head_rows:
                s = src_ref[rows, :]
                m_old = m_ref[rows, :]
                m_new = jnp.maximum(m_old, jnp.max(s, axis=-1, keepdims=True))
                a_ref[rows, :] = jnp.exp2(m_old - m_new)
                m_ref[rows, :] = m_new
                p_ref[rows, :] = jnp.exp2(s - jnp.concatenate([m_new] * (tk // LANES), axis=1)).astype(BF16)
            acc_ref[...] = a_ref[...] * acc_ref[...] + _dot(p_ref[...], v_ref[0, pl.ds(tile_start(kt), tk), :])

        return buf_a, buf_b, scores, consume, acc_ref

    wk = pw_ref.shape[1]
    w0 = pl.multiple_of(jnp.maximum(t0 - WINDOW, 0), tq)
    dist = tq_col - (w0 + lax.broadcasted_iota(jnp.int32, (1, wk), 1))
    bias_w = jnp.where((dist >= 0) & (dist < WINDOW), 0.0, NEG)
    s_w = _dot_nt(q4, kw_ref[0, pl.ds(w0, wk), :])
    for rows in head_rows:
        s = s_w[rows] + bias_w
        pw_ref[rows, :] = jnp.exp2(s - jnp.max(s, axis=-1, keepdims=True)).astype(BF16)
    acc_w = _dot(pw_ref[...], vw_ref[0, pl.ds(w0, wk), :])

    n_idx = lax.broadcasted_iota(jnp.int32, (1, kc_ref.shape[2]), 1)
    cmp_valid = (n_idx * CMP_STRIDE + (CMP_BLOCK - 1) <= tq_col) & (n_idx < n_cmp)
    s_c = _dot_nt(q4, kc_ref[0, 0])
    p_sum = None
    for rows in head_rows:
        s = jnp.where(cmp_valid, s_c[rows], NEG)
        e = jnp.where(cmp_valid, jnp.exp2(s - jnp.max(s, axis=-1, keepdims=True)), 0.0)
        l = jnp.sum(e, axis=-1, keepdims=True)
        p = e / jnp.where(l > 0.0, l, 1.0)
        p_sum = p if p_sum is None else p_sum + p
        pc_ref[rows, :] = p.astype(BF16)
    o_cmp = _dot(pc_ref[...], vc_ref[0, 0])

    oj = lax.broadcasted_iota(jnp.int32, (n_sel, kc_ref.shape[2]), 0)
    on = lax.broadcasted_iota(jnp.int32, (n_sel, kc_ref.shape[2]), 1)
    overlap = jnp.clip(jnp.minimum(on * CMP_STRIDE + CMP_BLOCK, oj * SEL_BLOCK + SEL_BLOCK)
                       - jnp.maximum(on * CMP_STRIDE, oj * SEL_BLOCK), 0, CMP_BLOCK).astype(F32) * (1.0 / CMP_BLOCK)
    overlap = jnp.where(on < n_cmp, overlap, 0.0)
    imp = lax.dot_general(overlap, p_sum, (((1,), (1,)), ((), ())), precision=lax.Precision.HIGHEST,
                          preferred_element_type=F32)
    jb = lax.broadcasted_iota(jnp.int32, (n_sel, tq), 0)
    tt = t0 + lax.broadcasted_iota(jnp.int32, (n_sel, tq), 1)
    cur = jnp.right_shift(tt, 6)
    forced = (jb == 0) | (jb == cur) | (jb == cur - 1)
    causal = jb * SEL_BLOCK <= tt
    score = jnp.where(forced, 1e30, jnp.where(causal, imp, NEG))
    rank = jnp.zeros((n_sel, tq), F32)
    for a in range(n_sel):
        sa = score[a:a + 1, :]
        ahead = (sa > score) | ((sa == score) & (jb > a))
        rank = rank + jnp.where(ahead, 1.0, 0.0)
    chosen = (rank < float(n_top)) & causal
    sel_bias = jnp.where(chosen, 0.0, NEG).T.astype(BF16)

    def sel_bias_tile(kt):
        k0 = kt * tk
        expand = (lax.broadcasted_iota(jnp.int32, (n_sel, tk), 0)
                  == jnp.right_shift(k0 + lax.broadcasted_iota(jnp.int32, (n_sel, tk), 1), 6))
        bias = _dot(sel_bias, jnp.where(expand, 1.0, 0.0).astype(BF16))
        return jnp.where(k0 + lax.broadcasted_iota(jnp.int32, (1, tk), 1) <= tq_col, bias, NEG)

    sa, sb, sel_scores, sel_consume, sel_acc = online_branch(
        sel_state, ks_ref, vs_ref, lambda kt: pl.multiple_of(jnp.minimum(kt * tk, seq - tk), tk), sel_bias_tile)

    n_kt = (t0 + tq + tk - 1) // tk

    def pair_step(j, carry):
        sel_scores(2 * j + 1, sb)
        sel_consume(2 * j, sa)
        sel_scores(2 * j + 2, sa)
        sel_consume(2 * j + 1, sb)
        return carry

    sel_scores(0, sa)
    lax.fori_loop(0, (n_kt + 1) // 2, pair_step, 0)
    acc_s = sel_acc[...]

    g = g_ref[...]
    g_hi = g.astype(BF16)
    g_lo = (g - g_hi.astype(F32)).astype(BF16)
    n_tiles = 3 * (GQA_GROUP // 2)
    src = jnp.bitwise_and(lax.broadcasted_iota(jnp.int32, (2 * LANES, n_tiles * LANES), 0), LANES - 1)
    dst = lax.broadcasted_iota(jnp.int32, (2 * LANES, n_tiles * LANES), 1)
    dst_tile, dst_head = jnp.right_shift(dst, 7), jnp.bitwise_and(jnp.right_shift(dst, 6), 1)
    pick = jnp.where(src == 2 * dst_tile + dst_head, 1.0, 0.0).astype(BF16)
    gates = _dot(jnp.concatenate([g_hi, g_lo], axis=1), pick)

    def pair_tile(acc, pair, normalise):
        a, b = acc[head_rows[2 * pair]], acc[head_rows[2 * pair + 1]]
        num = jnp.where(left, a, pltpu.roll(b, HEAD_DIM, 1))
        return num / jnp.where(left, pltpu.roll(a, HEAD_DIM, 1), b) if normalise else num

    for pair in range(GQA_GROUP // 2):
        out = None
        for br, (acc, normalise) in enumerate(((o_cmp, False), (acc_s, True), (acc_w, True))):
            tile = br * (GQA_GROUP // 2) + pair
            term = gates[:, tile * LANES:(tile + 1) * LANES] * pair_tile(acc, pair, normalise)
            out = term if out is None else out + term
        o_ref[:, pair * LANES:(pair + 1) * LANES] = out.astype(BF16)


def _nsa(q, cmp_kv, ksel, vsel, kwin, vwin, ng, bsz, seq):
    tq = Q_TILE
    n_q = seq // tq
    n_chunks = cmp_kv.shape[2]
    rows = GQA_GROUP * tq
    assert (WINDOW + tq) % K_TILE == 0 and seq % K_TILE == 0
    qrow = lambda b, h, i: (b * n_q + i, h)
    kv_spec = pl.BlockSpec((1, seq, LANES), lambda b, h, i: (b, 0, h))

    def branch_state():
        return (pltpu.VMEM((rows, K_TILE), F32), pltpu.VMEM((rows, K_TILE), F32), pltpu.VMEM((rows, K_TILE), BF16),
                pltpu.VMEM((rows, LANES), F32), pltpu.VMEM((rows, LANES), F32), pltpu.VMEM((rows, LANES), F32))
    return pl.pallas_call(
        functools.partial(_nsa_kernel, seq=seq),
        grid=(bsz, N_KV_HEADS, n_q),
        in_specs=[pl.BlockSpec((tq, 2 * LANES), qrow),
                  pl.BlockSpec((1, 1, n_chunks, LANES), lambda b, h, i: (0, b * N_KV_HEADS + h, 0, 0)),
                  pl.BlockSpec((1, 1, n_chunks, LANES), lambda b, h, i: (1, b * N_KV_HEADS + h, 0, 0)),
                  kv_spec, kv_spec, kv_spec, kv_spec,
                  pl.BlockSpec((tq, LANES), qrow)],
        out_specs=pl.BlockSpec((tq, 2 * LANES), qrow),
        out_shape=jax.ShapeDtypeStruct((bsz * seq, ATTN_WIDTH), BF16),
        scratch_shapes=[branch_state(), pltpu.VMEM((rows, WINDOW + tq), BF16), pltpu.VMEM((rows, n_chunks), BF16)],
        compiler_params=_cparams(3),
        name="nsa_attention",
    )(q, cmp_kv, cmp_kv, ksel, vsel, kwin, vwin, ng)


def _merge_kernel(x_ref, ys_ref, yp_ref, yn_ref, bg_ref, wb_ref, wo_ref, g_ref, b_ref, o_ref, *, alpha):
    merged = None
    for k, y_ref in enumerate((ys_ref, yp_ref, yn_ref)):
        term = bg_ref[:, k * D_MODEL:(k + 1) * D_MODEL].astype(F32) * _dot(y_ref[...], wb_ref[k].astype(BF16))
        merged = term if merged is None else merged + term
    r = alpha * x_ref[...] + _dot(merged.astype(BF16), wo_ref[...].astype(BF16))
    o_ref[...] = _layer_norm(r, g_ref[...], b_ref[...])


def _merge(xr, ys, yp, yn, bg, wb_stack, wo_stack, layer, g, b, alpha):
    rows = xr.shape[0]
    tile = ROW_TILE
    row = lambda i: (i, 0)
    return pl.pallas_call(
        functools.partial(_merge_kernel, alpha=alpha),
        grid=(rows // tile,),
        in_specs=[pl.BlockSpec((tile, D_MODEL), row),
                  pl.BlockSpec((tile, SSM_WIDTH), row), pl.BlockSpec((tile, POOL_WIDTH), row),
                  pl.BlockSpec((tile, ATTN_WIDTH), row), pl.BlockSpec((tile, N_BRANCH * D_MODEL), row),
                  _resident_layer(wb_stack, layer), _resident_layer(wo_stack, layer),
                  _resident(g.shape), _resident(b.shape)],
        out_specs=pl.BlockSpec((tile, D_MODEL), row),
        out_shape=jax.ShapeDtypeStruct((rows, D_MODEL), F32),
        compiler_params=_cparams(1),
        name="branch_merge",
    )(xr, ys, yp, yn, bg, wb_stack, wo_stack, g, b)


def _ffn_kernel(x_ref, wi_ref, wo_ref, g_ref, b_ref, o_ref, *, alpha):
    x = x_ref[...]
    xb = x.astype(BF16)
    acc = None
    for c in range(FF_HIDDEN // FF_CHUNK):
        c0 = c * FF_CHUNK
        hg = _dot(xb, wi_ref[:, c0:c0 + FF_CHUNK].astype(BF16))
        hu = _dot(xb, wi_ref[:, FF_HIDDEN + c0:FF_HIDDEN + c0 + FF_CHUNK].astype(BF16))
        act = (hg * jax.nn.sigmoid(hg) * hu).astype(BF16)
        part = _dot(act, wo_ref[c0:c0 + FF_CHUNK, :].astype(BF16))
        acc = part if acc is None else acc + part
    o_ref[...] = _layer_norm(alpha * x + acc, g_ref[...], b_ref[...])


def _ffn(xr, wi_stack, wo_stack, layer, g, b, alpha):
    rows = xr.shape[0]
    tile = ROW_TILE
    row = lambda i: (i, 0)
    return pl.pallas_call(
        functools.partial(_ffn_kernel, alpha=alpha),
        grid=(rows // tile,),
        in_specs=[pl.BlockSpec((tile, D_MODEL), row), _resident_layer(wi_stack, layer),
                  _resident_layer(wo_stack, layer), _resident(g.shape), _resident(b.shape)],
        out_specs=pl.BlockSpec((tile, D_MODEL), row),
        out_shape=jax.ShapeDtypeStruct((rows, D_MODEL), F32),
        compiler_params=_cparams(1),
        name="swiglu_ffn",
    )(xr, wi_stack, wo_stack, g, b)


def _pack_s5(a_re, a_im, log_dt, b_re, b_im, c_re, c_im):
    depth = a_re.shape[0]
    a_re, a_im = a_re.astype(F32), a_im.astype(F32)
    dt = jnp.exp(log_dt.astype(F32))[..., None]
    mag = jnp.exp(a_re * dt)
    lbar_re, lbar_im = mag * jnp.cos(a_im * dt), mag * jnp.sin(a_im * dt)
    den = a_re * a_re + a_im * a_im
    coef_re = ((lbar_re - 1.0) * a_re + lbar_im * a_im) / den
    coef_im = (lbar_im * a_re - (lbar_re - 1.0) * a_im) / den
    b_re, b_im = b_re.astype(F32), b_im.astype(F32)
    bbar_re = coef_re[..., None] * b_re - coef_im[..., None] * b_im
    bbar_im = coef_re[..., None] * b_im + coef_im[..., None] * b_re
    gpc = S5_LANE_CHUNK // SSM_STATE
    n_chunk = SSM_GROUPS // gpc
    eye = jnp.eye(gpc, dtype=F32)

    def b_block(part):
        v = part.transpose(0, 1, 3, 2).reshape(depth, n_chunk, gpc, SSM_GROUP, SSM_STATE)
        return jnp.einsum('xy,dqxcp->dqxcyp', eye, v).reshape(depth, n_chunk, gpc * SSM_GROUP, gpc * SSM_STATE)

    def c_block(part):
        v = part.reshape(depth, n_chunk, gpc, SSM_GROUP, SSM_STATE)
        return jnp.einsum('xy,dqxcp->dqxpyc', eye, v).reshape(depth, n_chunk, gpc * SSM_STATE, gpc * SSM_GROUP)

    wb = jnp.concatenate([b_block(bbar_re), b_block(bbar_im)], axis=-1).astype(BF16)
    wc = jnp.concatenate([c_block(c_re.astype(F32)), c_block(-c_im.astype(F32))], axis=-2).astype(BF16)
    state_w = SSM_GROUPS * SSM_STATE
    lre = jnp.broadcast_to(lbar_re.reshape(depth, 1, state_w), (depth, SUBLANES, state_w))
    lim = jnp.broadcast_to(lbar_im.reshape(depth, 1, state_w), (depth, SUBLANES, state_w))
    return lre, lim, wb, wc


def kernel(x, positions, w_in, ssm_a_re, ssm_a_im, ssm_log_dt, ssm_b_re, ssm_b_im, ssm_c_re, ssm_c_im,
           ssm_d, ssm_w_glu, pool_w, pool_scale, cmp_pos, cmp_w1, cmp_b1, cmp_w2,
           w_branch, w_out, ln_g, ln_b, ffn_w_in, ffn_w_out):
    bsz, seq, _ = x.shape
    depth = w_in.shape[0]
    rows = bsz * seq
    assert bsz == SUBLANES and seq % ROW_TILE == 0 and seq >= WINDOW + Q_TILE
    alpha = (2 * depth) ** 0.25

    assert w_in.shape[-1] == IN_RAW
    lre, lim, s5_wb, s5_wc = _pack_s5(ssm_a_re, ssm_a_im, ssm_log_dt, ssm_b_re, ssm_b_im, ssm_c_re, ssm_c_im)
    s5_d = ssm_d.astype(F32).reshape(depth, 1, SSM_WIDTH)
    pool_wb = pool_w.astype(BF16)
    pool_sc = pool_scale.astype(F32).reshape(depth, 1, POOL_WIDTH)
    half = CMP_STRIDE * HEAD_DIM
    cmp_posr = jnp.broadcast_to(cmp_pos.astype(F32).reshape(depth, 2, 1, CMP_BLOCK * HEAD_DIM),
                                (depth, 2, SUBLANES, CMP_BLOCK * HEAD_DIM))
    cmp_b1r = cmp_b1.astype(F32).reshape(depth, 2, 1, CMP_HIDDEN)
    cmp_w2dup = jnp.concatenate([cmp_w2, cmp_w2 * jnp.array([1.0, 0.0], cmp_w2.dtype).reshape(1, 2, 1, 1)],
                                axis=-1).astype(BF16)
    lng = ln_g.astype(F32).reshape(depth, 2, 1, D_MODEL)
    lnb = ln_b.astype(F32).reshape(depth, 2, 1, D_MODEL)

    def pack_w_in(w):
        w_gate = w[:, COL_NG:COL_NG + N_GATE].reshape(D_MODEL, 3, N_KV_HEADS, GQA_GROUP)
        w_gate = w_gate.transpose(0, 2, 1, 3).reshape(D_MODEL, N_KV_HEADS, 3 * GQA_GROUP)
        w_gate = jnp.pad(w_gate, ((0, 0), (0, 0), (0, LANES - 3 * GQA_GROUP))).reshape(D_MODEL, N_KV_HEADS * LANES)
        return jnp.concatenate([w[:, :COL_NG], w_gate, w[:, COL_NG + N_GATE:]], axis=-1).astype(BF16)

    cos, sin = _rope_tables(positions)
    xr = x.astype(F32).reshape(rows, D_MODEL)
    for l in range(depth):
        (u_ssm, u_pool, q, cmp_in, ksel, vsel, kwin, vwin, ng, bg) = _inproj(xr, pack_w_in(w_in[l]), cos, sin, bsz, seq)
        y_ssm = _s5(u_ssm.reshape(bsz, seq, SSM_WIDTH), lre[l], lim[l], s5_wb[l], s5_wc[l], s5_d[l], ssm_w_glu, l)
        y_pool = _pool(u_pool.reshape(bsz, seq, POOL_WIDTH), pool_wb[l], pool_sc[l])
        cmp_z = cmp_in.reshape(2, bsz * N_KV_HEADS, seq // CMP_STRIDE, CMP_STRIDE * HEAD_DIM)
        cmp_kv = _compress(cmp_z, cmp_w1, l, cmp_posr[l], cmp_b1r[l], cmp_w2dup[l])
        y_nsa = _nsa(q, cmp_kv, ksel, vsel, kwin, vwin, ng, bsz, seq)
        x1 = _merge(xr, y_ssm.reshape(rows, SSM_WIDTH), y_pool.reshape(rows, POOL_WIDTH), y_nsa, bg,
                    w_branch, w_out, l, lng[l, 0], lnb[l, 0], alpha)
        xr = _ffn(x1, ffn_w_in, ffn_w_out, l, lng[l, 1], lnb[l, 1], alpha)
    return xr.reshape(bsz, seq, D_MODEL).astype(x.dtype)
```

```python
import functools
import math

import jax
import jax.numpy as jnp
from jax import lax
from jax.experimental import pallas as pl
from jax.experimental.pallas import tpu as pltpu

F32 = jnp.float32
BF16 = jnp.bfloat16

D_MODEL = 1024
SSM_WIDTH = 512
SSM_GROUP = 16
SSM_GROUPS = 32
SSM_STATE = 64
POOL_WIDTH = 512
POOL_WINDOWS = (2, 4, 8, 16)
POOL_GROUP = 128
HEAD_DIM = 64
N_HEADS = 8
N_KV_HEADS = 2
GQA_GROUP = 4
ATTN_WIDTH = 512
KV_WIDTH = 128
N_BRANCH = 3
CMP_BLOCK = 32
CMP_STRIDE = 16
CMP_HIDDEN = 256
SEL_BLOCK = 64
SEL_TOP = 16
WINDOW = 512
ROPE_THETA = 10000.0
FF_HIDDEN = 2816
LN_EPS = 1e-5
NEG = -1e30
N_GATE = 3 * N_HEADS
IN_RAW = 3 * 512 + 6 * KV_WIDTH + N_GATE + N_BRANCH * D_MODEL

LANES = 128
SUBLANES = 8
VMEM_LIMIT_BYTES = 56 * 1024 * 1024

COL_SSM = 0
COL_POOL = 512
COL_Q = 1024
COL_KV = 1536
COL_NG = 2304
COL_BG = COL_NG + N_KV_HEADS * LANES
IN_PACKED = COL_BG + N_BRANCH * D_MODEL
LOG2E = 1.4426950408889634

ROW_TILE = 512
PROJ_TILE = 1024
CHUNK_PERM_ROWS = 512
S5_STEPS = 64
S5_LANE_CHUNK = 512
POOL_STEPS = 64
POOL_HALO = 16
Q_TILE = 256
K_TILE = 256
FF_CHUNK = 256


def _cparams(n_axes):
    return pltpu.CompilerParams(dimension_semantics=("arbitrary",) * n_axes,
                                vmem_limit_bytes=VMEM_LIMIT_BYTES)


def _resident(shape):
    nd = len(shape)
    return pl.BlockSpec(shape, lambda *_: (0,) * nd, pipeline_mode=pl.Buffered(1))


def _resident_layer(stacked, layer):
    nd = stacked.ndim
    return pl.BlockSpec((None,) + stacked.shape[1:], lambda *_: (layer,) + (0,) * (nd - 1),
                        pipeline_mode=pl.Buffered(1))


def _gelu_tanh(x):
    return x * (0.5 * (1.0 + jnp.tanh(math.sqrt(2.0 / math.pi) * (x + 0.044715 * (x * x * x)))))


def _layer_norm(r, g, b):
    mu = jnp.mean(r, axis=-1, keepdims=True)
    c = r - mu
    var = jnp.mean(c * c, axis=-1, keepdims=True)
    return c * lax.rsqrt(var + LN_EPS) * g + b


def _dot(a, b):
    return jnp.dot(a, b, preferred_element_type=F32)


def _dot_nt(a, b):
    return lax.dot_general(a, b, (((1,), (1,)), ((), ())), preferred_element_type=F32)


def _rope_table_kernel(pos_ref, inv_ref, cos_ref, sin_ref):
    ang = pos_ref[...] * inv_ref[...]
    lane = lax.broadcasted_iota(jnp.int32, ang.shape, 1)
    first_half = jnp.bitwise_and(lane, HEAD_DIM - 1) < HEAD_DIM // 2
    cos_ref[...] = jnp.cos(ang)
    sin_ref[...] = jnp.where(first_half, -jnp.sin(ang), jnp.sin(ang))


def _rope_tables(positions):
    rows = positions.size
    inv = ROPE_THETA ** (-jnp.arange(0, HEAD_DIM, 2, dtype=F32) / HEAD_DIM)
    inv = jnp.tile(inv, LANES // (HEAD_DIM // 2)).reshape(1, LANES)
    pos = positions.astype(F32).reshape(rows, 1)
    tile = ROW_TILE
    return pl.pallas_call(
        _rope_table_kernel,
        grid=(rows // tile,),
        in_specs=[pl.BlockSpec((tile, 1), lambda i: (i, 0)),
                  pl.BlockSpec((1, LANES), lambda i: (0, 0))],
        out_specs=[pl.BlockSpec((tile, LANES), lambda i: (i, 0))] * 2,
        out_shape=[jax.ShapeDtypeStruct((rows, LANES), F32)] * 2,
        compiler_params=_cparams(1),
        name="rope_tables",
    )(pos, inv)


def _inproj_kernel(x_ref, w_ref, cos_ref, sin_ref, chunk_perm_ref,
                   ussm_ref, upool_ref, q_ref, cmp_ref, ksel_ref, vsel_ref, kwin_ref, vwin_ref,
                   ng_ref, bg_ref):
    xb = x_ref[...].astype(BF16)
    cos = cos_ref[...]
    sin = sin_ref[...]
    lane = lax.broadcasted_iota(jnp.int32, cos.shape, 1)
    first_half = jnp.bitwise_and(lane, HEAD_DIM - 1) < HEAD_DIM // 2
    left = lane < HEAD_DIM

    def proj(c0, width):
        return _dot(xb, w_ref[:, c0:c0 + width])

    def rope(t):
        swapped = jnp.where(first_half, pltpu.roll(t, LANES - HEAD_DIM // 2, 1),
                            pltpu.roll(t, HEAD_DIM // 2, 1))
        return t * cos + swapped * sin

    def dup(t):
        r = pltpu.roll(t, HEAD_DIM, 1)
        return jnp.where(left, t, r), jnp.where(left, r, t)

    def with_ones(t):
        return jnp.where(left, t, 1.0), jnp.where(left, pltpu.roll(t, HEAD_DIM, 1), 1.0)

    ussm_ref[...] = proj(COL_SSM, SSM_WIDTH).astype(BF16)
    upool_ref[...] = proj(COL_POOL, POOL_WIDTH).astype(BF16)
    scale = HEAD_DIM ** -0.5 * LOG2E
    for j in range(ATTN_WIDTH // LANES):
        t = rope(proj(COL_Q + j * LANES, LANES))
        q_ref[:, j * LANES:(j + 1) * LANES] = (t * scale).astype(BF16)

    kv = proj(COL_KV, 6 * KV_WIDTH)
    perm_rows = chunk_perm_ref.shape[0]
    n_chunks = perm_rows // CMP_STRIDE
    left_c = lax.broadcasted_iota(jnp.int32, (n_chunks, LANES), 1) < HEAD_DIM
    left_p = lax.broadcasted_iota(jnp.int32, (perm_rows, LANES), 1) < HEAD_DIM
    for j, roped in enumerate((True, False)):
        t = kv[:, j * LANES:(j + 1) * LANES]
        t = (rope(t) if roped else t).astype(BF16)
        for part in range(x_ref.shape[0] // perm_rows):
            by_pos = _dot(chunk_perm_ref[...], t[part * perm_rows:(part + 1) * perm_rows])
            rolled = pltpu.roll(by_pos, HEAD_DIM, 1)
            for h, dup_h in enumerate((jnp.where(left_p, by_pos, rolled), jnp.where(left_p, rolled, by_pos))):
                for i in range(CMP_STRIDE // 2):
                    even = dup_h[(2 * i) * n_chunks:(2 * i + 1) * n_chunks]
                    odd = dup_h[(2 * i + 1) * n_chunks:(2 * i + 2) * n_chunks]
                    cmp_ref[j, 0, h, part * n_chunks:(part + 1) * n_chunks, i * LANES:(i + 1) * LANES] = (
                        jnp.where(left_c, even, odd).astype(BF16))
    for j, (ref, is_key) in enumerate(((ksel_ref, True), (vsel_ref, False), (kwin_ref, True), (vwin_ref, False))):
        t = kv[:, (2 + j) * LANES:(3 + j) * LANES]
        a, b = dup(rope(t)) if is_key else with_ones(t)
        ref[0, :, 0:LANES] = a.astype(BF16)
        ref[0, :, LANES:2 * LANES] = b.astype(BF16)

    ng_ref[...] = jax.nn.sigmoid(proj(COL_NG, N_KV_HEADS * LANES))
    for k in range(N_BRANCH):
        bg_ref[:, k * D_MODEL:(k + 1) * D_MODEL] = jax.nn.sigmoid(proj(COL_BG + k * D_MODEL, D_MODEL)).astype(BF16)


def _inproj(xr, w, cos, sin, bsz, seq):
    rows = bsz * seq
    tile = PROJ_TILE
    n_s = seq // tile
    row = lambda i: (i, 0)
    dup_spec = pl.BlockSpec((1, tile, 2 * LANES), lambda i: (i // n_s, i % n_s, 0))
    chunk_w = CMP_STRIDE * HEAD_DIM
    r = jnp.arange(CHUNK_PERM_ROWS)
    n_c = CHUNK_PERM_ROWS // CMP_STRIDE
    chunk_perm = (r[None, :] == (r[:, None] % n_c) * CMP_STRIDE + r[:, None] // n_c).astype(BF16)
    out_shape = [
        jax.ShapeDtypeStruct((rows, SSM_WIDTH), BF16),
        jax.ShapeDtypeStruct((rows, POOL_WIDTH), BF16),
        jax.ShapeDtypeStruct((rows, ATTN_WIDTH), BF16),
        jax.ShapeDtypeStruct((2, bsz, N_KV_HEADS, seq // CMP_STRIDE, chunk_w), BF16),
        jax.ShapeDtypeStruct((bsz, seq, 2 * LANES), BF16),
        jax.ShapeDtypeStruct((bsz, seq, 2 * LANES), BF16),
        jax.ShapeDtypeStruct((bsz, seq, 2 * LANES), BF16),
        jax.ShapeDtypeStruct((bsz, seq, 2 * LANES), BF16),
        jax.ShapeDtypeStruct((rows, N_KV_HEADS * LANES), F32),
        jax.ShapeDtypeStruct((rows, N_BRANCH * D_MODEL), BF16),
    ]
    out_specs = [
        pl.BlockSpec((tile, SSM_WIDTH), row),
        pl.BlockSpec((tile, POOL_WIDTH), row),
        pl.BlockSpec((tile, ATTN_WIDTH), row),
        pl.BlockSpec((2, 1, N_KV_HEADS, tile // CMP_STRIDE, chunk_w), lambda i: (0, i // n_s, 0, i % n_s, 0)),
        dup_spec, dup_spec, dup_spec, dup_spec,
        pl.BlockSpec((tile, N_KV_HEADS * LANES), row),
        pl.BlockSpec((tile, N_BRANCH * D_MODEL), row),
    ]
    return pl.pallas_call(
        _inproj_kernel,
        grid=(rows // tile,),
        in_specs=[pl.BlockSpec((tile, D_MODEL), row), _resident(w.shape),
                  pl.BlockSpec((tile, LANES), row), pl.BlockSpec((tile, LANES), row), _resident(chunk_perm.shape)],
        out_specs=out_specs,
        out_shape=out_shape,
        compiler_params=_cparams(1),
        name="in_projection",
    )(xr, w, cos, sin, chunk_perm)


def _time_major_perms(bsz, steps):
    r = jnp.arange(bsz * steps)
    to_tm = (r[None, :] == (r[:, None] % bsz) * steps + r[:, None] // bsz).astype(BF16)
    return to_tm, to_tm.T


def _s5_kernel(u_ref, tm_ref, bm_ref, lre_ref, lim_ref, wb_ref, wc_ref, d_ref, wglu_ref, o_ref,
               bu_ref, st_ref, y_ref):
    bsz, n_steps, _ = u_ref.shape
    n_rows = bsz * n_steps
    n_chunk = wb_ref.shape[0]
    cw = S5_LANE_CHUNK

    @pl.when(pl.program_id(0) == 0)
    def _():
        st_ref[...] = jnp.zeros(st_ref.shape, F32)

    u = _dot(tm_ref[...], u_ref[...].reshape(n_rows, SSM_WIDTH)).astype(BF16)
    for c in range(n_chunk):
        bu_ref[c] = _dot(u[:, c * LANES:(c + 1) * LANES], wb_ref[c])

    for c in range(n_chunk):
        lre = lre_ref[:, c * cw:(c + 1) * cw]
        lim = lim_ref[:, c * cw:(c + 1) * cw]
        hre = st_ref[0, :, c * cw:(c + 1) * cw]
        him = st_ref[1, :, c * cw:(c + 1) * cw]
        for t in range(n_steps):
            rows = slice(t * SUBLANES, (t + 1) * SUBLANES)
            hre, him = (lre * hre - lim * him + bu_ref[c, rows, 0:cw],
                        lre * him + lim * hre + bu_ref[c, rows, cw:2 * cw])
            bu_ref[c, rows, 0:cw] = hre
            bu_ref[c, rows, cw:2 * cw] = him
        st_ref[0, :, c * cw:(c + 1) * cw] = hre
        st_ref[1, :, c * cw:(c + 1) * cw] = him
        y_ref[:, c * LANES:(c + 1) * LANES] = _dot(bu_ref[c].astype(BF16), wc_ref[c])
    y = y_ref[...] + d_ref[...] * u.astype(F32)
    z = _dot(_gelu_tanh(y).astype(BF16), wglu_ref[...].astype(BF16))
    out = (z[:, :SSM_WIDTH] * jax.nn.sigmoid(z[:, SSM_WIDTH:])).astype(BF16)
    o_ref[...] = _dot(bm_ref[...], out).astype(BF16).reshape(bsz, n_steps, SSM_WIDTH)


def _s5(u, lre, lim, wb, wc, d, wglu_stack, layer):
    bsz, seq, _ = u.shape
    steps = S5_STEPS
    tile = steps * bsz
    state_w = lre.shape[1]
    to_tm, to_bm = _time_major_perms(bsz, steps)
    blk = pl.BlockSpec((bsz, steps, SSM_WIDTH), lambda i: (0, i, 0))
    return pl.pallas_call(
        _s5_kernel,
        grid=(seq // steps,),
        in_specs=[blk, _resident(to_tm.shape), _resident(to_bm.shape),
                  _resident(lre.shape), _resident(lim.shape), _resident(wb.shape), _resident(wc.shape),
                  _resident(d.shape), _resident_layer(wglu_stack, layer)],
        out_specs=blk,
        out_shape=jax.ShapeDtypeStruct(u.shape, BF16),
        scratch_shapes=[pltpu.VMEM((wb.shape[0], tile, 2 * S5_LANE_CHUNK), F32),
                        pltpu.VMEM((2, SUBLANES, state_w), F32),
                        pltpu.VMEM((tile, SSM_WIDTH), F32)],
        compiler_params=_cparams(1),
        name="s5_mixer",
    )(u, to_tm, to_bm, lre, lim, wb, wc, d, wglu_stack)


def _pool_kernel(u_ref, tm_ref, bm_ref, w_ref, sc_ref, o_ref, ext_ref, y_ref):
    bsz, n_steps, _ = u_ref.shape
    n_rows = bsz * n_steps
    halo = POOL_HALO * SUBLANES
    i = pl.program_id(0)

    @pl.when(i == 0)
    def _():
        ext_ref[0:halo, :] = jnp.zeros((halo, POOL_WIDTH), F32)

    ext_ref[halo:halo + n_rows, :] = _dot(tm_ref[...], u_ref[...].reshape(n_rows, POOL_WIDTH))
    row = lax.broadcasted_iota(jnp.int32, (n_rows, POOL_GROUP), 0)
    t = i * (n_rows // SUBLANES) + jnp.right_shift(row, 3)
    for gi, w in enumerate(POOL_WINDOWS):
        c0 = gi * POOL_GROUP
        cur = ext_ref[halo:halo + n_rows, c0:c0 + POOL_GROUP]
        acc = cur
        for k in range(1, w):
            acc = acc + ext_ref[halo - k * SUBLANES:halo - k * SUBLANES + n_rows, c0:c0 + POOL_GROUP]
        cnt = jnp.minimum(t + 1, w).astype(F32)
        pooled = acc / cnt - cur
        mixed = _dot(pooled.astype(BF16), w_ref[gi]) * sc_ref[:, c0:c0 + POOL_GROUP]
        y_ref[:, c0:c0 + POOL_GROUP] = mixed.astype(BF16)
    ext_ref[0:halo, :] = ext_ref[n_rows:n_rows + halo, :]
    o_ref[...] = _dot(bm_ref[...], y_ref[...]).astype(BF16).reshape(bsz, n_steps, POOL_WIDTH)


def _pool(u, w, sc):
    bsz, seq, _ = u.shape
    steps = POOL_STEPS
    tile = steps * bsz
    to_tm, to_bm = _time_major_perms(bsz, steps)
    blk = pl.BlockSpec((bsz, steps, POOL_WIDTH), lambda i: (0, i, 0))
    return pl.pallas_call(
        _pool_kernel,
        grid=(seq // steps,),
        in_specs=[blk, _resident(to_tm.shape), _resident(to_bm.shape), _resident(w.shape), _resident(sc.shape)],
        out_specs=blk,
        out_shape=jax.ShapeDtypeStruct(u.shape, BF16),
        scratch_shapes=[pltpu.VMEM((tile + POOL_HALO * SUBLANES, POOL_WIDTH), F32),
                        pltpu.VMEM((tile, POOL_WIDTH), BF16)],
        compiler_params=_cparams(1),
        name="pool_mixer",
    )(u, to_tm, to_bm, w, sc)


def _compress_kernel(z_ref, w1_ref, pos_ref, b1_ref, w2_ref, o_ref):
    half = CMP_STRIDE * HEAD_DIM
    _, n_bh, n_chunks, width = z_ref.shape
    z = z_ref[0].reshape(n_bh * n_chunks, width)
    w_top = w1_ref[0, :half, :].astype(BF16)
    w_bot = w1_ref[0, half:, :].astype(BF16)
    bottom_next = pltpu.roll(_dot(z, w_bot), n_bh * n_chunks - 1, 0)
    pos = pos_ref[0].astype(BF16)
    cst = _dot(pos[:, :half], w_top) + _dot(pos[:, half:], w_bot)
    hid = _gelu_tanh(_dot(z, w_top) + bottom_next + cst[0:1, :] + b1_ref[0])
    out = _dot(hid.astype(BF16), w2_ref[0])
    lane = lax.broadcasted_iota(jnp.int32, out.shape, 1)
    is_value = pl.program_id(0) == 1
    out = jnp.where((lane >= HEAD_DIM) & is_value, 1.0, out).astype(BF16)
    o_ref[0] = out.reshape(n_bh, n_chunks, LANES)


def _compress(z, w1_stack, layer, pos, b1, w2dup):
    _, n_bh, n_chunks, width = z.shape
    return pl.pallas_call(
        _compress_kernel,
        grid=(2,),
        in_specs=[pl.BlockSpec((1, n_bh, n_chunks, width), lambda j: (j, 0, 0, 0)),
                  pl.BlockSpec((None, 1) + w1_stack.shape[2:], lambda j: (layer, j, 0, 0)),
                  pl.BlockSpec((1,) + pos.shape[1:], lambda j: (j, 0, 0)),
                  pl.BlockSpec((1,) + b1.shape[1:], lambda j: (j, 0, 0)),
                  pl.BlockSpec((1,) + w2dup.shape[1:], lambda j: (j, 0, 0))],
        out_specs=pl.BlockSpec((1, n_bh, n_chunks, LANES), lambda j: (j, 0, 0, 0)),
        out_shape=jax.ShapeDtypeStruct((2, n_bh, n_chunks, LANES), BF16),
        compiler_params=_cparams(1),
        name="compress_mlp",
    )(z, w1_stack, pos, b1, w2dup)


def _nsa_kernel(q_ref, kc_ref, vc_ref, ks_ref, vs_ref, kw_ref, vw_ref, g_ref, o_ref,
                sel_state, pw_ref, pc_ref, *, seq):
    tq = q_ref.shape[0]
    tk = sel_state[0].shape[1]
    n_sel = seq // SEL_BLOCK
    n_top = min(SEL_TOP, n_sel)
    n_cmp = (seq - CMP_BLOCK) // CMP_STRIDE + 1
    t0 = pl.program_id(2) * tq
    head_rows = [slice(g * tq, (g + 1) * tq) for g in range(GQA_GROUP)]

    lane = lax.broadcasted_iota(jnp.int32, (tq, LANES), 1)
    left = lane < HEAD_DIM
    zero = jnp.zeros((tq, LANES), BF16)
    pairs = (q_ref[:, 0:LANES], q_ref[:, LANES:2 * LANES])
    q4 = jnp.concatenate([jnp.where(left, pairs[0], zero), jnp.where(left, zero, pairs[0]),
                          jnp.where(left, pairs[1], zero), jnp.where(left, zero, pairs[1])], axis=0)
    tq_col = t0 + lax.broadcasted_iota(jnp.int32, (tq, 1), 0)

    def online_branch(state, k_ref, v_ref, tile_start, tile_bias):
        buf_a, buf_b, p_ref, m_ref, a_ref, acc_ref = state
        m_ref[...] = jnp.full(m_ref.shape, NEG, F32)
        acc_ref[...] = jnp.zeros(acc_ref.shape, F32)

        def scores(kt, dst_ref):
            bias = tile_bias(kt)
            s = _dot_nt(q4, k_ref[0, pl.ds(tile_start(kt), tk), :])
            for rows in head_rows:
                dst_ref[rows, :] = s[rows] + bias

        def consume(kt, src_ref):
            for rows in head_rows:
                s = src_ref[rows, :]
                m_old = m_ref[rows, :]
                m_new = jnp.maximum(m_old, jnp.max(s, axis=-1, keepdims=True))
                a_ref[rows, :] = jnp.exp2(m_old - m_new)
                m_ref[rows, :] = m_new
                p_ref[rows, :] = jnp.exp2(s - jnp.concatenate([m_new] * (tk // LANES), axis=1)).astype(BF16)
            acc_ref[...] = a_ref[...] * acc_ref[...] + _dot(p_ref[...], v_ref[0, pl.ds(tile_start(kt), tk), :])

        return buf_a, buf_b, scores, consume, acc_ref

    wk = pw_ref.shape[1]
    w0 = pl.multiple_of(jnp.maximum(t0 - WINDOW, 0), tq)
    dist = tq_col - (w0 + lax.broadcasted_iota(jnp.int32, (1, wk), 1))
    bias_w = jnp.where((dist >= 0) & (dist < WINDOW), 0.0, NEG)
    s_w = _dot_nt(q4, kw_ref[0, pl.ds(w0, wk), :])
    for rows in head_rows:
        s = s_w[rows] + bias_w
        pw_ref[rows, :] = jnp.exp2(s - jnp.max(s, axis=-1, keepdims=True)).astype(BF16)
    acc_w = _dot(pw_ref[...], vw_ref[0, pl.ds(w0, wk), :])

    n_idx = lax.broadcasted_iota(jnp.int32, (1, kc_ref.shape[2]), 1)
    cmp_valid = (n_idx * CMP_STRIDE + (CMP_BLOCK - 1) <= tq_col) & (n_idx < n_cmp)
    s_c = _dot_nt(q4, kc_ref[0, 0])
    p_sum = None
    for rows in head_rows:
        s = jnp.where(cmp_valid, s_c[rows], NEG)
        e = jnp.where(cmp_valid, jnp.exp2(s - jnp.max(s, axis=-1, keepdims=True)), 0.0)
        l = jnp.sum(e, axis=-1, keepdims=True)
        p = e / jnp.where(l > 0.0, l, 1.0)
        p_sum = p if p_sum is None else p_sum + p
        pc_ref[rows, :] = p.astype(BF16)
    o_cmp = _dot(pc_ref[...], vc_ref[0, 0])

    oj = lax.broadcasted_iota(jnp.int32, (n_sel, kc_ref.shape[2]), 0)
    on = lax.broadcasted_iota(jnp.int32, (n_sel, kc_ref.shape[2]), 1)
    overlap = jnp.clip(jnp.minimum(on * CMP_STRIDE + CMP_BLOCK, oj * SEL_BLOCK + SEL_BLOCK)
                       - jnp.maximum(on * CMP_STRIDE, oj * SEL_BLOCK), 0, CMP_BLOCK).astype(F32) * (1.0 / CMP_BLOCK)
    overlap = jnp.where(on < n_cmp, overlap, 0.0)
    imp = lax.dot_general(overlap, p_sum, (((1,), (1,)), ((), ())), precision=lax.Precision.HIGHEST,
                          preferred_element_type=F32)
    jb = lax.broadcasted_iota(jnp.int32, (n_sel, tq), 0)
    tt = t0 + lax.broadcasted_iota(jnp.int32, (n_sel, tq), 1)
    cur = jnp.right_shift(tt, 6)
    forced = (jb == 0) | (jb == cur) | (jb == cur - 1)
    causal = jb * SEL_BLOCK <= tt
    score = jnp.where(forced, 1e30, jnp.where(causal, imp, NEG))
    rank = jnp.zeros((n_sel, tq), F32)
    for a in range(n_sel):
        sa = score[a:a + 1, :]
        ahead = (sa > score) | ((sa == score) & (jb > a))
        rank = rank + jnp.where(ahead, 1.0, 0.0)
    chosen = (rank < float(n_top)) & causal
    sel_bias = jnp.where(chosen, 0.0, NEG).T.astype(BF16)

    def sel_bias_tile(kt):
        k0 = kt * tk
        expand = (lax.broadcasted_iota(jnp.int32, (n_sel, tk), 0)
                  == jnp.right_shift(k0 + lax.broadcasted_iota(jnp.int32, (n_sel, tk), 1), 6))
        bias = _dot(sel_bias, jnp.where(expand, 1.0, 0.0).astype(BF16))
        return jnp.where(k0 + lax.broadcasted_iota(jnp.int32, (1, tk), 1) <= tq_col, bias, NEG)

    sa, sb, sel_scores, sel_consume, sel_acc = online_branch(
        sel_state, ks_ref, vs_ref, lambda kt: pl.multiple_of(jnp.minimum(kt * tk, seq - tk), tk), sel_bias_tile)

    n_kt = (t0 + tq + tk - 1) // tk

    def pair_step(j, carry):
        sel_scores(2 * j + 1, sb)
        sel_consume(2 * j, sa)
        sel_scores(2 * j + 2, sa)
        sel_consume(2 * j + 1, sb)
        return carry

    sel_scores(0, sa)
    lax.fori_loop(0, (n_kt + 1) // 2, pair_step, 0)
    acc_s = sel_acc[...]

    g = g_ref[...]
    g_hi = g.astype(BF16)
    g_lo = (g - g_hi.astype(F32)).astype(BF16)
    n_tiles = 3 * (GQA_GROUP // 2)
    src = jnp.bitwise_and(lax.broadcasted_iota(jnp.int32, (2 * LANES, n_tiles * LANES), 0), LANES - 1)
    dst = lax.broadcasted_iota(jnp.int32, (2 * LANES, n_tiles * LANES), 1)
    dst_tile, dst_head = jnp.right_shift(dst, 7), jnp.bitwise_and(jnp.right_shift(dst, 6), 1)
    pick = jnp.where(src == 2 * dst_tile + dst_head, 1.0, 0.0).astype(BF16)
    gates = _dot(jnp.concatenate([g_hi, g_lo], axis=1), pick)

    def pair_tile(acc, pair, normalise):
        a, b = acc[head_rows[2 * pair]], acc[head_rows[2 * pair + 1]]
        num = jnp.where(left, a, pltpu.roll(b, HEAD_DIM, 1))
        return num / jnp.where(left, pltpu.roll(a, HEAD_DIM, 1), b) if normalise else num

    for pair in range(GQA_GROUP // 2):
        out = None
        for br, (acc, normalise) in enumerate(((o_cmp, False), (acc_s, True), (acc_w, True))):
            tile = br * (GQA_GROUP // 2) + pair
            term = gates[:, tile * LANES:(tile + 1) * LANES] * pair_tile(acc, pair, normalise)
            out = term if out is None else out + term
        o_ref[:, pair * LANES:(pair + 1) * LANES] = out.astype(BF16)


def _nsa(q, cmp_kv, ksel, vsel, kwin, vwin, ng, bsz, seq):
    tq = Q_TILE
    n_q = seq // tq
    n_chunks = cmp_kv.shape[2]
    rows = GQA_GROUP * tq
    assert (WINDOW + tq) % K_TILE == 0 and seq % K_TILE == 0
    qrow = lambda b, h, i: (b * n_q + i, h)
    kv_spec = pl.BlockSpec((1, seq, LANES), lambda b, h, i: (b, 0, h))

    def branch_state():
        return (pltpu.VMEM((rows, K_TILE), F32), pltpu.VMEM((rows, K_TILE), F32), pltpu.VMEM((rows, K_TILE), BF16),
                pltpu.VMEM((rows, LANES), F32), pltpu.VMEM((rows, LANES), F32), pltpu.VMEM((rows, LANES), F32))
    return pl.pallas_call(
        functools.partial(_nsa_kernel, seq=seq),
        grid=(bsz, N_KV_HEADS, n_q),
        in_specs=[pl.BlockSpec((tq, 2 * LANES), qrow),
                  pl.BlockSpec((1, 1, n_chunks, LANES), lambda b, h, i: (0, b * N_KV_HEADS + h, 0, 0)),
                  pl.BlockSpec((1, 1, n_chunks, LANES), lambda b, h, i: (1, b * N_KV_HEADS + h, 0, 0)),
                  kv_spec, kv_spec, kv_spec, kv_spec,
                  pl.BlockSpec((tq, LANES), qrow)],
        out_specs=pl.BlockSpec((tq, 2 * LANES), qrow),
        out_shape=jax.ShapeDtypeStruct((bsz * seq, ATTN_WIDTH), BF16),
        scratch_shapes=[branch_state(), pltpu.VMEM((rows, WINDOW + tq), BF16), pltpu.VMEM((rows, n_chunks), BF16)],
        compiler_params=_cparams(3),
        name="nsa_attention",
    )(q, cmp_kv, cmp_kv, ksel, vsel, kwin, vwin, ng)


def _merge_kernel(x_ref, ys_ref, yp_ref, yn_ref, bg_ref, wb_ref, wo_ref, g_ref, b_ref, o_ref, *, alpha):
    merged = None
    for k, y_ref in enumerate((ys_ref, yp_ref, yn_ref)):
        term = bg_ref[:, k * D_MODEL:(k + 1) * D_MODEL].astype(F32) * _dot(y_ref[...], wb_ref[k].astype(BF16))
        merged = term if merged is None else merged + term
    r = alpha * x_ref[...] + _dot(merged.astype(BF16), wo_ref[...].astype(BF16))
    o_ref[...] = _layer_norm(r, g_ref[...], b_ref[...])


def _merge(xr, ys, yp, yn, bg, wb_stack, wo_stack, layer, g, b, alpha):
    rows = xr.shape[0]
    tile = PROJ_TILE
    row = lambda i: (i, 0)
    return pl.pallas_call(
        functools.partial(_merge_kernel, alpha=alpha),
        grid=(rows // tile,),
        in_specs=[pl.BlockSpec((tile, D_MODEL), row),
                  pl.BlockSpec((tile, SSM_WIDTH), row), pl.BlockSpec((tile, POOL_WIDTH), row),
                  pl.BlockSpec((tile, ATTN_WIDTH), row), pl.BlockSpec((tile, N_BRANCH * D_MODEL), row),
                  _resident_layer(wb_stack, layer), _resident_layer(wo_stack, layer),
                  _resident(g.shape), _resident(b.shape)],
        out_specs=pl.BlockSpec((tile, D_MODEL), row),
        out_shape=jax.ShapeDtypeStruct((rows, D_MODEL), F32),
        compiler_params=_cparams(1),
        name="branch_merge",
    )(xr, ys, yp, yn, bg, wb_stack, wo_stack, g, b)


def _ffn_kernel(x_ref, wi_ref, wo_ref, g_ref, b_ref, o_ref, *, alpha):
    x = x_ref[...]
    xb = x.astype(BF16)
    acc = None
    for c in range(FF_HIDDEN // FF_CHUNK):
        c0 = c * FF_CHUNK
        hg = _dot(xb, wi_ref[:, c0:c0 + FF_CHUNK].astype(BF16))
        hu = _dot(xb, wi_ref[:, FF_HIDDEN + c0:FF_HIDDEN + c0 + FF_CHUNK].astype(BF16))
        act = (hg * jax.nn.sigmoid(hg) * hu).astype(BF16)
        part = _dot(act, wo_ref[c0:c0 + FF_CHUNK, :].astype(BF16))
        acc = part if acc is None else acc + part
    o_ref[...] = _layer_norm(alpha * x + acc, g_ref[...], b_ref[...])


def _ffn(xr, wi_stack, wo_stack, layer, g, b, alpha):
    rows = xr.shape[0]
    tile = ROW_TILE
    row = lambda i: (i, 0)
    return pl.pallas_call(
        functools.partial(_ffn_kernel, alpha=alpha),
        grid=(rows // tile,),
        in_specs=[pl.BlockSpec((tile, D_MODEL), row), _resident_layer(wi_stack, layer),
                  _resident_layer(wo_stack, layer), _resident(g.shape), _resident(b.shape)],
        out_specs=pl.BlockSpec((tile, D_MODEL), row),
        out_shape=jax.ShapeDtypeStruct((rows, D_MODEL), F32),
        compiler_params=_cparams(1),
        name="swiglu_ffn",
    )(xr, wi_stack, wo_stack, g, b)


def _pack_s5(a_re, a_im, log_dt, b_re, b_im, c_re, c_im):
    depth = a_re.shape[0]
    a_re, a_im = a_re.astype(F32), a_im.astype(F32)
    dt = jnp.exp(log_dt.astype(F32))[..., None]
    mag = jnp.exp(a_re * dt)
    lbar_re, lbar_im = mag * jnp.cos(a_im * dt), mag * jnp.sin(a_im * dt)
    den = a_re * a_re + a_im * a_im
    coef_re = ((lbar_re - 1.0) * a_re + lbar_im * a_im) / den
    coef_im = (lbar_im * a_re - (lbar_re - 1.0) * a_im) / den
    b_re, b_im = b_re.astype(F32), b_im.astype(F32)
    bbar_re = coef_re[..., None] * b_re - coef_im[..., None] * b_im
    bbar_im = coef_re[..., None] * b_im + coef_im[..., None] * b_re
    gpc = S5_LANE_CHUNK // SSM_STATE
    n_chunk = SSM_GROUPS // gpc
    eye = jnp.eye(gpc, dtype=F32)

    def b_block(part):
        v = part.transpose(0, 1, 3, 2).reshape(depth, n_chunk, gpc, SSM_GROUP, SSM_STATE)
        return jnp.einsum('xy,dqxcp->dqxcyp', eye, v).reshape(depth, n_chunk, gpc * SSM_GROUP, gpc * SSM_STATE)

    def c_block(part):
        v = part.reshape(depth, n_chunk, gpc, SSM_GROUP, SSM_STATE)
        return jnp.einsum('xy,dqxcp->dqxpyc', eye, v).reshape(depth, n_chunk, gpc * SSM_STATE, gpc * SSM_GROUP)

    wb = jnp.concatenate([b_block(bbar_re), b_block(bbar_im)], axis=-1).astype(BF16)
    wc = jnp.concatenate([c_block(c_re.astype(F32)), c_block(-c_im.astype(F32))], axis=-2).astype(BF16)
    state_w = SSM_GROUPS * SSM_STATE
    lre = jnp.broadcast_to(lbar_re.reshape(depth, 1, state_w), (depth, SUBLANES, state_w))
    lim = jnp.broadcast_to(lbar_im.reshape(depth, 1, state_w), (depth, SUBLANES, state_w))
    return lre, lim, wb, wc


def kernel(x, positions, w_in, ssm_a_re, ssm_a_im, ssm_log_dt, ssm_b_re, ssm_b_im, ssm_c_re, ssm_c_im,
           ssm_d, ssm_w_glu, pool_w, pool_scale, cmp_pos, cmp_w1, cmp_b1, cmp_w2,
           w_branch, w_out, ln_g, ln_b, ffn_w_in, ffn_w_out):
    bsz, seq, _ = x.shape
    depth = w_in.shape[0]
    rows = bsz * seq
    assert bsz == SUBLANES and seq % PROJ_TILE == 0 and seq >= WINDOW + Q_TILE
    alpha = (2 * depth) ** 0.25

    assert w_in.shape[-1] == IN_RAW
    lre, lim, s5_wb, s5_wc = _pack_s5(ssm_a_re, ssm_a_im, ssm_log_dt, ssm_b_re, ssm_b_im, ssm_c_re, ssm_c_im)
    s5_d = ssm_d.astype(F32).reshape(depth, 1, SSM_WIDTH)
    pool_wb = pool_w.astype(BF16)
    pool_sc = pool_scale.astype(F32).reshape(depth, 1, POOL_WIDTH)
    half = CMP_STRIDE * HEAD_DIM
    cmp_posr = jnp.broadcast_to(cmp_pos.astype(F32).reshape(depth, 2, 1, CMP_BLOCK * HEAD_DIM),
                                (depth, 2, SUBLANES, CMP_BLOCK * HEAD_DIM))
    cmp_b1r = cmp_b1.astype(F32).reshape(depth, 2, 1, CMP_HIDDEN)
    cmp_w2dup = jnp.concatenate([cmp_w2, cmp_w2 * jnp.array([1.0, 0.0], cmp_w2.dtype).reshape(1, 2, 1, 1)],
                                axis=-1).astype(BF16)
    lng = ln_g.astype(F32).reshape(depth, 2, 1, D_MODEL)
    lnb = ln_b.astype(F32).reshape(depth, 2, 1, D_MODEL)

    def pack_w_in(w):
        w_gate = w[:, COL_NG:COL_NG + N_GATE].reshape(D_MODEL, 3, N_KV_HEADS, GQA_GROUP)
        w_gate = w_gate.transpose(0, 2, 1, 3).reshape(D_MODEL, N_KV_HEADS, 3 * GQA_GROUP)
        w_gate = jnp.pad(w_gate, ((0, 0), (0, 0), (0, LANES - 3 * GQA_GROUP))).reshape(D_MODEL, N_KV_HEADS * LANES)
        return jnp.concatenate([w[:, :COL_NG], w_gate, w[:, COL_NG + N_GATE:]], axis=-1).astype(BF16)

    cos, sin = _rope_tables(positions)
    xr = x.astype(F32).reshape(rows, D_MODEL)
    for l in range(depth):
        (u_ssm, u_pool, q, cmp_in, ksel, vsel, kwin, vwin, ng, bg) = _inproj(xr, pack_w_in(w_in[l]), cos, sin, bsz, seq)
        y_ssm = _s5(u_ssm.reshape(bsz, seq, SSM_WIDTH), lre[l], lim[l], s5_wb[l], s5_wc[l], s5_d[l], ssm_w_glu, l)
        y_pool = _pool(u_pool.reshape(bsz, seq, POOL_WIDTH), pool_wb[l], pool_sc[l])
        cmp_z = cmp_in.reshape(2, bsz * N_KV_HEADS, seq // CMP_STRIDE, CMP_STRIDE * HEAD_DIM)
        cmp_kv = _compress(cmp_z, cmp_w1, l, cmp_posr[l], cmp_b1r[l], cmp_w2dup[l])
        y_nsa = _nsa(q, cmp_kv, ksel, vsel, kwin, vwin, ng, bsz, seq)
        x1 = _merge(xr, y_ssm.reshape(rows, SSM_WIDTH), y_pool.reshape(rows, POOL_WIDTH), y_nsa, bg,
                    w_branch, w_out, l, lng[l, 0], lnb[l, 0], alpha)
        xr = _ffn(x1, ffn_w_in, ffn_w_out, l, lng[l, 1], lnb[l, 1], alpha)
    return xr.reshape(bsz, seq, D_MODEL).astype(x.dtype)
```

```python
import functools
import math

import jax
import jax.numpy as jnp
from jax import lax
from jax.experimental import pallas as pl
from jax.experimental.pallas import tpu as pltpu

F32 = jnp.float32
BF16 = jnp.bfloat16

D_MODEL = 1024
SSM_WIDTH = 512
SSM_GROUP = 16
SSM_GROUPS = 32
SSM_STATE = 64
POOL_WIDTH = 512
POOL_WINDOWS = (2, 4, 8, 16)
POOL_GROUP = 128
HEAD_DIM = 64
N_HEADS = 8
N_KV_HEADS = 2
GQA_GROUP = 4
ATTN_WIDTH = 512
KV_WIDTH = 128
N_BRANCH = 3
CMP_BLOCK = 32
CMP_STRIDE = 16
CMP_HIDDEN = 256
SEL_BLOCK = 64
SEL_TOP = 16
WINDOW = 512
ROPE_THETA = 10000.0
FF_HIDDEN = 2816
LN_EPS = 1e-5
NEG = -1e30
N_GATE = 3 * N_HEADS
IN_RAW = 3 * 512 + 6 * KV_WIDTH + N_GATE + N_BRANCH * D_MODEL

LANES = 128
SUBLANES = 8
VMEM_LIMIT_BYTES = 58 * 1024 * 1024

COL_SSM = 0
COL_POOL = 512
COL_Q = 1024
COL_KV = 1536
COL_NG = 2304
COL_BG = COL_NG + N_KV_HEADS * LANES
IN_PACKED = COL_BG + N_BRANCH * D_MODEL
LOG2E = 1.4426950408889634

ROW_TILE = 512
PROJ_TILE = 1024
CHUNK_PERM_ROWS = 512
S5_STEPS = 64
S5_LANE_CHUNK = 512
POOL_STEPS = 64
POOL_HALO = 16
Q_TILE = 256
K_TILE = 256
FF_CHUNK = 256


def _cparams(n_axes):
    return pltpu.CompilerParams(dimension_semantics=("arbitrary",) * n_axes,
                                vmem_limit_bytes=VMEM_LIMIT_BYTES)


def _resident(shape):
    nd = len(shape)
    return pl.BlockSpec(shape, lambda *_: (0,) * nd, pipeline_mode=pl.Buffered(1))


def _resident_layer(stacked, layer):
    nd = stacked.ndim
    return pl.BlockSpec((None,) + stacked.shape[1:], lambda *_: (layer,) + (0,) * (nd - 1),
                        pipeline_mode=pl.Buffered(1))


def _gelu_tanh(x):
    return x * (0.5 * (1.0 + jnp.tanh(math.sqrt(2.0 / math.pi) * (x + 0.044715 * (x * x * x)))))


def _layer_norm(r, g, b):
    mu = jnp.mean(r, axis=-1, keepdims=True)
    c = r - mu
    var = jnp.mean(c * c, axis=-1, keepdims=True)
    return c * lax.rsqrt(var + LN_EPS) * g + b


def _dot(a, b):
    return jnp.dot(a, b, preferred_element_type=F32)


def _dot_nt(a, b):
    return lax.dot_general(a, b, (((1,), (1,)), ((), ())), preferred_element_type=F32)


def _rope_table_kernel(pos_ref, inv_ref, cos_ref, sin_ref):
    ang = pos_ref[...] * inv_ref[...]
    lane = lax.broadcasted_iota(jnp.int32, ang.shape, 1)
    first_half = jnp.bitwise_and(lane, HEAD_DIM - 1) < HEAD_DIM // 2
    cos_ref[...] = jnp.cos(ang)
    sin_ref[...] = jnp.where(first_half, -jnp.sin(ang), jnp.sin(ang))


def _rope_tables(positions):
    rows = positions.size
    inv = ROPE_THETA ** (-jnp.arange(0, HEAD_DIM, 2, dtype=F32) / HEAD_DIM)
    inv = jnp.tile(inv, LANES // (HEAD_DIM // 2)).reshape(1, LANES)
    pos = positions.astype(F32).reshape(rows, 1)
    tile = ROW_TILE
    return pl.pallas_call(
        _rope_table_kernel,
        grid=(rows // tile,),
        in_specs=[pl.BlockSpec((tile, 1), lambda i: (i, 0)),
                  pl.BlockSpec((1, LANES), lambda i: (0, 0))],
        out_specs=[pl.BlockSpec((tile, LANES), lambda i: (i, 0))] * 2,
        out_shape=[jax.ShapeDtypeStruct((rows, LANES), F32)] * 2,
        compiler_params=_cparams(1),
        name="rope_tables",
    )(pos, inv)


def _inproj_kernel(x_ref, w_head_ref, w_tail_ref, cos_ref, sin_ref, chunk_perm_ref,
                   ussm_ref, upool_ref, q_ref, cmp_ref, ksel_ref, vsel_ref, kwin_ref, vwin_ref,
                   ng_ref, bg_ref):
    xb = x_ref[...].astype(BF16)
    cos = cos_ref[...]
    sin = sin_ref[...]
    lane = lax.broadcasted_iota(jnp.int32, cos.shape, 1)
    first_half = jnp.bitwise_and(lane, HEAD_DIM - 1) < HEAD_DIM // 2
    left = lane < HEAD_DIM

    def proj(c0, width):
        if c0 < COL_NG:
            return _dot(xb, w_head_ref[:, c0:c0 + width].astype(BF16))
        return _dot(xb, w_tail_ref[:, c0 - COL_NG:c0 - COL_NG + width])

    def rope(t):
        swapped = jnp.where(first_half, pltpu.roll(t, LANES - HEAD_DIM // 2, 1),
                            pltpu.roll(t, HEAD_DIM // 2, 1))
        return t * cos + swapped * sin

    def dup(t):
        r = pltpu.roll(t, HEAD_DIM, 1)
        return jnp.where(left, t, r), jnp.where(left, r, t)

    def with_ones(t):
        return jnp.where(left, t, 1.0), jnp.where(left, pltpu.roll(t, HEAD_DIM, 1), 1.0)

    ussm_ref[...] = proj(COL_SSM, SSM_WIDTH).astype(BF16)
    upool_ref[...] = proj(COL_POOL, POOL_WIDTH).astype(BF16)
    scale = HEAD_DIM ** -0.5 * LOG2E
    for j in range(ATTN_WIDTH // LANES):
        t = rope(proj(COL_Q + j * LANES, LANES))
        q_ref[:, j * LANES:(j + 1) * LANES] = (t * scale).astype(BF16)

    kv = proj(COL_KV, 6 * KV_WIDTH)
    perm_rows = chunk_perm_ref.shape[0]
    n_chunks = perm_rows // CMP_STRIDE
    left_c = lax.broadcasted_iota(jnp.int32, (n_chunks, LANES), 1) < HEAD_DIM
    left_p = lax.broadcasted_iota(jnp.int32, (perm_rows, LANES), 1) < HEAD_DIM
    for j, roped in enumerate((True, False)):
        t = kv[:, j * LANES:(j + 1) * LANES]
        t = (rope(t) if roped else t).astype(BF16)
        for part in range(x_ref.shape[0] // perm_rows):
            by_pos = _dot(chunk_perm_ref[...], t[part * perm_rows:(part + 1) * perm_rows])
            rolled = pltpu.roll(by_pos, HEAD_DIM, 1)
            for h, dup_h in enumerate((jnp.where(left_p, by_pos, rolled), jnp.where(left_p, rolled, by_pos))):
                for i in range(CMP_STRIDE // 2):
                    even = dup_h[(2 * i) * n_chunks:(2 * i + 1) * n_chunks]
                    odd = dup_h[(2 * i + 1) * n_chunks:(2 * i + 2) * n_chunks]
                    cmp_ref[j, 0, h, part * n_chunks:(part + 1) * n_chunks, i * LANES:(i + 1) * LANES] = (
                        jnp.where(left_c, even, odd).astype(BF16))
    for j, (ref, is_key) in enumerate(((ksel_ref, True), (vsel_ref, False), (kwin_ref, True), (vwin_ref, False))):
        t = kv[:, (2 + j) * LANES:(3 + j) * LANES]
        a, b = dup(rope(t)) if is_key else with_ones(t)
        ref[0, :, 0:LANES] = a.astype(BF16)
        ref[0, :, LANES:2 * LANES] = b.astype(BF16)

    ng_ref[...] = jax.nn.sigmoid(proj(COL_NG, N_KV_HEADS * LANES))
    for k in range(N_BRANCH):
        bg_ref[:, k * D_MODEL:(k + 1) * D_MODEL] = jax.nn.sigmoid(proj(COL_BG + k * D_MODEL, D_MODEL)).astype(BF16)


def _inproj(xr, w_stack, layer, w_tail, cos, sin, bsz, seq):
    rows = bsz * seq
    head_spec = pl.BlockSpec((None, D_MODEL, COL_NG), lambda i: (layer, 0, 0), pipeline_mode=pl.Buffered(1))
    tile = PROJ_TILE
    n_s = seq // tile
    row = lambda i: (i, 0)
    dup_spec = pl.BlockSpec((1, tile, 2 * LANES), lambda i: (i // n_s, i % n_s, 0))
    chunk_w = CMP_STRIDE * HEAD_DIM
    r = jnp.arange(CHUNK_PERM_ROWS)
    n_c = CHUNK_PERM_ROWS // CMP_STRIDE
    chunk_perm = (r[None, :] == (r[:, None] % n_c) * CMP_STRIDE + r[:, None] // n_c).astype(BF16)
    out_shape = [
        jax.ShapeDtypeStruct((rows, SSM_WIDTH), BF16),
        jax.ShapeDtypeStruct((rows, POOL_WIDTH), BF16),
        jax.ShapeDtypeStruct((rows, ATTN_WIDTH), BF16),
        jax.ShapeDtypeStruct((2, bsz, N_KV_HEADS, seq // CMP_STRIDE, chunk_w), BF16),
        jax.ShapeDtypeStruct((bsz, seq, 2 * LANES), BF16),
        jax.ShapeDtypeStruct((bsz, seq, 2 * LANES), BF16),
        jax.ShapeDtypeStruct((bsz, seq, 2 * LANES), BF16),
        jax.ShapeDtypeStruct((bsz, seq, 2 * LANES), BF16),
        jax.ShapeDtypeStruct((rows, N_KV_HEADS * LANES), F32),
        jax.ShapeDtypeStruct((rows, N_BRANCH * D_MODEL), BF16),
    ]
    out_specs = [
        pl.BlockSpec((tile, SSM_WIDTH), row),
        pl.BlockSpec((tile, POOL_WIDTH), row),
        pl.BlockSpec((tile, ATTN_WIDTH), row),
        pl.BlockSpec((2, 1, N_KV_HEADS, tile // CMP_STRIDE, chunk_w), lambda i: (0, i // n_s, 0, i % n_s, 0)),
        dup_spec, dup_spec, dup_spec, dup_spec,
        pl.BlockSpec((tile, N_KV_HEADS * LANES), row),
        pl.BlockSpec((tile, N_BRANCH * D_MODEL), row),
    ]
    return pl.pallas_call(
        _inproj_kernel,
        grid=(rows // tile,),
        in_specs=[pl.BlockSpec((tile, D_MODEL), row), head_spec, _resident(w_tail.shape),
                  pl.BlockSpec((tile, LANES), row), pl.BlockSpec((tile, LANES), row), _resident(chunk_perm.shape)],
        out_specs=out_specs,
        out_shape=out_shape,
        compiler_params=_cparams(1),
        name="in_projection",
    )(xr, w_stack, w_tail, cos, sin, chunk_perm)


def _time_major_perms(bsz, steps):
    r = jnp.arange(bsz * steps)
    to_tm = (r[None, :] == (r[:, None] % bsz) * steps + r[:, None] // bsz).astype(BF16)
    return to_tm, to_tm.T


def _s5_kernel(u_ref, tm_ref, bm_ref, lre_ref, lim_ref, wb_ref, wc_ref, d_ref, wglu_ref, o_ref,
               bu_ref, st_ref, y_ref):
    bsz, n_steps, _ = u_ref.shape
    n_rows = bsz * n_steps
    n_chunk = wb_ref.shape[0]
    cw = S5_LANE_CHUNK

    @pl.when(pl.program_id(0) == 0)
    def _():
        st_ref[...] = jnp.zeros(st_ref.shape, F32)

    u = _dot(tm_ref[...], u_ref[...].reshape(n_rows, SSM_WIDTH)).astype(BF16)
    for c in range(n_chunk):
        bu_ref[c] = _dot(u[:, c * LANES:(c + 1) * LANES], wb_ref[c])

    for c in range(n_chunk):
        lre = lre_ref[:, c * cw:(c + 1) * cw]
        lim = lim_ref[:, c * cw:(c + 1) * cw]
        hre = st_ref[0, :, c * cw:(c + 1) * cw]
        him = st_ref[1, :, c * cw:(c + 1) * cw]
        for t in range(n_steps):
            rows = slice(t * SUBLANES, (t + 1) * SUBLANES)
            hre, him = (lre * hre - lim * him + bu_ref[c, rows, 0:cw],
                        lre * him + lim * hre + bu_ref[c, rows, cw:2 * cw])
            bu_ref[c, rows, 0:cw] = hre
            bu_ref[c, rows, cw:2 * cw] = him
        st_ref[0, :, c * cw:(c + 1) * cw] = hre
        st_ref[1, :, c * cw:(c + 1) * cw] = him
        y_ref[:, c * LANES:(c + 1) * LANES] = _dot(bu_ref[c].astype(BF16), wc_ref[c])
    y = y_ref[...] + d_ref[...] * u.astype(F32)
    z = _dot(_gelu_tanh(y).astype(BF16), wglu_ref[...].astype(BF16))
    out = (z[:, :SSM_WIDTH] * jax.nn.sigmoid(z[:, SSM_WIDTH:])).astype(BF16)
    o_ref[...] = _dot(bm_ref[...], out).astype(BF16).reshape(bsz, n_steps, SSM_WIDTH)


def _s5(u, lre, lim, wb, wc, d, wglu_stack, layer):
    bsz, seq, _ = u.shape
    steps = S5_STEPS
    tile = steps * bsz
    state_w = lre.shape[1]
    to_tm, to_bm = _time_major_perms(bsz, steps)
    blk = pl.BlockSpec((bsz, steps, SSM_WIDTH), lambda i: (0, i, 0))
    return pl.pallas_call(
        _s5_kernel,
        grid=(seq // steps,),
        in_specs=[blk, _resident(to_tm.shape), _resident(to_bm.shape),
                  _resident(lre.shape), _resident(lim.shape), _resident(wb.shape), _resident(wc.shape),
                  _resident(d.shape), _resident_layer(wglu_stack, layer)],
        out_specs=blk,
        out_shape=jax.ShapeDtypeStruct(u.shape, BF16),
        scratch_shapes=[pltpu.VMEM((wb.shape[0], tile, 2 * S5_LANE_CHUNK), F32),
                        pltpu.VMEM((2, SUBLANES, state_w), F32),
                        pltpu.VMEM((tile, SSM_WIDTH), F32)],
        compiler_params=_cparams(1),
        name="s5_mixer",
    )(u, to_tm, to_bm, lre, lim, wb, wc, d, wglu_stack)


def _pool_kernel(u_ref, tm_ref, bm_ref, w_ref, sc_ref, o_ref, ext_ref, y_ref):
    bsz, n_steps, _ = u_ref.shape
    n_rows = bsz * n_steps
    halo = POOL_HALO * SUBLANES
    i = pl.program_id(0)

    @pl.when(i == 0)
    def _():
        ext_ref[0:halo, :] = jnp.zeros((halo, POOL_WIDTH), F32)

    ext_ref[halo:halo + n_rows, :] = _dot(tm_ref[...], u_ref[...].reshape(n_rows, POOL_WIDTH))
    row = lax.broadcasted_iota(jnp.int32, (n_rows, POOL_GROUP), 0)
    t = i * (n_rows // SUBLANES) + jnp.right_shift(row, 3)
    for gi, w in enumerate(POOL_WINDOWS):
        c0 = gi * POOL_GROUP
        cur = ext_ref[halo:halo + n_rows, c0:c0 + POOL_GROUP]
        acc = cur
        for k in range(1, w):
            acc = acc + ext_ref[halo - k * SUBLANES:halo - k * SUBLANES + n_rows, c0:c0 + POOL_GROUP]
        cnt = jnp.minimum(t + 1, w).astype(F32)
        pooled = acc / cnt - cur
        mixed = _dot(pooled.astype(BF16), w_ref[gi]) * sc_ref[:, c0:c0 + POOL_GROUP]
        y_ref[:, c0:c0 + POOL_GROUP] = mixed.astype(BF16)
    ext_ref[0:halo, :] = ext_ref[n_rows:n_rows + halo, :]
    o_ref[...] = _dot(bm_ref[...], y_ref[...]).astype(BF16).reshape(bsz, n_steps, POOL_WIDTH)


def _pool(u, w, sc):
    bsz, seq, _ = u.shape
    steps = POOL_STEPS
    tile = steps * bsz
    to_tm, to_bm = _time_major_perms(bsz, steps)
    blk = pl.BlockSpec((bsz, steps, POOL_WIDTH), lambda i: (0, i, 0))
    return pl.pallas_call(
        _pool_kernel,
        grid=(seq // steps,),
        in_specs=[blk, _resident(to_tm.shape), _resident(to_bm.shape), _resident(w.shape), _resident(sc.shape)],
        out_specs=blk,
        out_shape=jax.ShapeDtypeStruct(u.shape, BF16),
        scratch_shapes=[pltpu.VMEM((tile + POOL_HALO * SUBLANES, POOL_WIDTH), F32),
                        pltpu.VMEM((tile, POOL_WIDTH), BF16)],
        compiler_params=_cparams(1),
        name="pool_mixer",
    )(u, to_tm, to_bm, w, sc)


def _compress_kernel(z_ref, w1_ref, pos_ref, b1_ref, w2_ref, o_ref):
    half = CMP_STRIDE * HEAD_DIM
    _, n_bh, n_chunks, width = z_ref.shape
    z = z_ref[0].reshape(n_bh * n_chunks, width)
    w_top = w1_ref[0, :half, :].astype(BF16)
    w_bot = w1_ref[0, half:, :].astype(BF16)
    bottom_next = pltpu.roll(_dot(z, w_bot), n_bh * n_chunks - 1, 0)
    pos = pos_ref[0].astype(BF16)
    cst = _dot(pos[:, :half], w_top) + _dot(pos[:, half:], w_bot)
    hid = _gelu_tanh(_dot(z, w_top) + bottom_next + cst[0:1, :] + b1_ref[0])
    out = _dot(hid.astype(BF16), w2_ref[0])
    lane = lax.broadcasted_iota(jnp.int32, out.shape, 1)
    is_value = pl.program_id(0) == 1
    out = jnp.where((lane >= HEAD_DIM) & is_value, 1.0, out).astype(BF16)
    o_ref[0] = out.reshape(n_bh, n_chunks, LANES)


def _compress(z, w1_stack, layer, pos, b1, w2dup):
    _, n_bh, n_chunks, width = z.shape
    return pl.pallas_call(
        _compress_kernel,
        grid=(2,),
        in_specs=[pl.BlockSpec((1, n_bh, n_chunks, width), lambda j: (j, 0, 0, 0)),
                  pl.BlockSpec((None, 1) + w1_stack.shape[2:], lambda j: (layer, j, 0, 0)),
                  pl.BlockSpec((1,) + pos.shape[1:], lambda j: (j, 0, 0)),
                  pl.BlockSpec((1,) + b1.shape[1:], lambda j: (j, 0, 0)),
                  pl.BlockSpec((1,) + w2dup.shape[1:], lambda j: (j, 0, 0))],
        out_specs=pl.BlockSpec((1, n_bh, n_chunks, LANES), lambda j: (j, 0, 0, 0)),
        out_shape=jax.ShapeDtypeStruct((2, n_bh, n_chunks, LANES), BF16),
        compiler_params=_cparams(1),
        name="compress_mlp",
    )(z, w1_stack, pos, b1, w2dup)


def _nsa_kernel(q_ref, kc_ref, vc_ref, ks_ref, vs_ref, kw_ref, vw_ref, g_ref,
                win_bias_ref, overlap_ref, expand_ref, pick_ref, o_ref,
                sel_state, pw_ref, pc_ref, *, seq):
    tq = q_ref.shape[0]
    tk = sel_state[0].shape[1]
    n_sel = seq // SEL_BLOCK
    n_top = min(SEL_TOP, n_sel)
    n_cmp = (seq - CMP_BLOCK) // CMP_STRIDE + 1
    t0 = pl.program_id(2) * tq
    head_rows = [slice(g * tq, (g + 1) * tq) for g in range(GQA_GROUP)]

    lane = lax.broadcasted_iota(jnp.int32, (tq, LANES), 1)
    left = lane < HEAD_DIM
    zero = jnp.zeros((tq, LANES), BF16)
    pairs = (q_ref[:, 0:LANES], q_ref[:, LANES:2 * LANES])
    q4 = jnp.concatenate([jnp.where(left, pairs[0], zero), jnp.where(left, zero, pairs[0]),
                          jnp.where(left, pairs[1], zero), jnp.where(left, zero, pairs[1])], axis=0)
    tq_col = t0 + lax.broadcasted_iota(jnp.int32, (tq, 1), 0)

    def online_branch(state, k_ref, v_ref, tile_start, tile_bias):
        buf_a, buf_b, p_ref, m_ref, a_ref, acc_ref = state
        m_ref[...] = jnp.full(m_ref.shape, NEG, F32)
        acc_ref[...] = jnp.zeros(acc_ref.shape, F32)

        def scores(kt, dst_ref):
            bias = tile_bias(kt)
            s = _dot_nt(q4, k_ref[0, pl.ds(tile_start(kt), tk), :])
            for rows in head_rows:
                dst_ref[rows, :] = s[rows] + bias

        def consume(kt, src_ref):
            for rows in head_rows:
                s = src_ref[rows, :]
                m_old = m_ref[rows, :]
                m_new = jnp.maximum(m_old, jnp.max(s, axis=-1, keepdims=True))
                a_ref[rows, :] = jnp.exp2(m_old - m_new)
                m_ref[rows, :] = m_new
                p_ref[rows, :] = jnp.exp2(s - jnp.concatenate([m_new] * (tk // LANES), axis=1)).astype(BF16)
            acc_ref[...] = a_ref[...] * acc_ref[...] + _dot(p_ref[...], v_ref[0, pl.ds(tile_start(kt), tk), :])

        return buf_a, buf_b, scores, consume, acc_ref


    wk = pw_ref.shape[1]
    w0 = pl.multiple_of(jnp.maximum(t0 - WINDOW, 0), tq)
    bias_w = win_bias_ref[jnp.minimum(pl.program_id(2), win_bias_ref.shape[0] - 1)]
    s_w = _dot_nt(q4, kw_ref[0, pl.ds(w0, wk), :])
    for rows in head_rows:
        s = s_w[rows] + bias_w
        pw_ref[rows, :] = jnp.exp2(s - jnp.max(s, axis=-1, keepdims=True)).astype(BF16)
    acc_w = _dot(pw_ref[...], vw_ref[0, pl.ds(w0, wk), :])

    n_idx = lax.broadcasted_iota(jnp.int32, (1, kc_ref.shape[2]), 1)
    cmp_valid = (n_idx * CMP_STRIDE + (CMP_BLOCK - 1) <= tq_col) & (n_idx < n_cmp)
    s_c = _dot_nt(q4, kc_ref[0, 0])
    p_sum = None
    for rows in head_rows:
        s = jnp.where(cmp_valid, s_c[rows], NEG)
        e = jnp.where(cmp_valid, jnp.exp2(s - jnp.max(s, axis=-1, keepdims=True)), 0.0)
        l = jnp.sum(e, axis=-1, keepdims=True)
        p = e / jnp.where(l > 0.0, l, 1.0)
        p_sum = p if p_sum is None else p_sum + p
        pc_ref[rows, :] = p.astype(BF16)
    o_cmp = _dot(pc_ref[...], vc_ref[0, 0])

    imp = lax.dot_general(overlap_ref[...], p_sum, (((1,), (1,)), ((), ())), precision=lax.Precision.HIGHEST,
                          preferred_element_type=F32)
    jb = lax.broadcasted_iota(jnp.int32, (n_sel, tq), 0)
    tt = t0 + lax.broadcasted_iota(jnp.int32, (n_sel, tq), 1)
    cur = jnp.right_shift(tt, 6)
    forced = (jb == 0) | (jb == cur) | (jb == cur - 1)
    causal = jb * SEL_BLOCK <= tt
    score = jnp.where(forced, 1e30, jnp.where(causal, imp, NEG))
    rank = jnp.zeros((n_sel, tq), F32)
    for a in range(n_sel):
        sa = score[a:a + 1, :]
        ahead = (sa > score) | ((sa == score) & (jb > a))
        rank = rank + jnp.where(ahead, 1.0, 0.0)
    chosen = (rank < float(n_top)) & causal
    sel_bias = jnp.where(chosen, 0.0, NEG).T.astype(BF16)

    def sel_bias_tile(kt):
        k0 = kt * tk
        expand = expand_ref[:, pl.ds(tile_start(kt), tk)]
        bias = _dot(sel_bias, expand)
        return jnp.where(k0 + lax.broadcasted_iota(jnp.int32, (1, tk), 1) <= tq_col, bias, NEG)

    def tile_start(kt):
        return pl.multiple_of(jnp.minimum(kt * tk, seq - tk), tk)

    sa, sb, sel_scores, sel_consume, sel_acc = online_branch(
        sel_state, ks_ref, vs_ref, tile_start, sel_bias_tile)

    n_kt = (t0 + tq + tk - 1) // tk

    def pair_step(j, carry):
        sel_scores(2 * j + 1, sb)
        sel_consume(2 * j, sa)
        sel_scores(2 * j + 2, sa)
        sel_consume(2 * j + 1, sb)
        return carry

    sel_scores(0, sa)
    lax.fori_loop(0, (n_kt + 1) // 2, pair_step, 0)
    acc_s = sel_acc[...]

    g = g_ref[...]
    g_hi = g.astype(BF16)
    g_lo = (g - g_hi.astype(F32)).astype(BF16)
    gates = _dot(jnp.concatenate([g_hi, g_lo], axis=1), pick_ref[...])

    def pair_tile(acc, pair, normalise):
        a, b = acc[head_rows[2 * pair]], acc[head_rows[2 * pair + 1]]
        num = jnp.where(left, a, pltpu.roll(b, HEAD_DIM, 1))
        return num / jnp.where(left, pltpu.roll(a, HEAD_DIM, 1), b) if normalise else num

    for pair in range(GQA_GROUP // 2):
        out = None
        for br, (acc, normalise) in enumerate(((o_cmp, False), (acc_s, True), (acc_w, True))):
            tile = br * (GQA_GROUP // 2) + pair
            term = gates[:, tile * LANES:(tile + 1) * LANES] * pair_tile(acc, pair, normalise)
            out = term if out is None else out + term
        o_ref[:, pair * LANES:(pair + 1) * LANES] = out.astype(BF16)


def _nsa(q, cmp_kv, ksel, vsel, kwin, vwin, ng, bsz, seq):
    tq = Q_TILE
    n_q = seq // tq
    n_chunks = cmp_kv.shape[2]
    rows = GQA_GROUP * tq
    assert (WINDOW + tq) % K_TILE == 0 and seq % K_TILE == 0
    qrow = lambda b, h, i: (b * n_q + i, h)
    kv_spec = pl.BlockSpec((1, seq, LANES), lambda b, h, i: (b, 0, h))

    def branch_state():
        return (pltpu.VMEM((rows, K_TILE), F32), pltpu.VMEM((rows, K_TILE), F32), pltpu.VMEM((rows, K_TILE), BF16),
                pltpu.VMEM((rows, LANES), F32), pltpu.VMEM((rows, LANES), F32), pltpu.VMEM((rows, LANES), F32))

    wk = WINDOW + tq
    case = jnp.arange(WINDOW // tq + 1)[:, None, None]
    dist = jnp.minimum(case * tq, WINDOW) + jnp.arange(tq)[None, :, None] - jnp.arange(wk)[None, None, :]
    win_bias = jnp.where((dist >= 0) & (dist < WINDOW), 0.0, NEG).astype(F32)
    n_sel = seq // SEL_BLOCK
    n_cmp = (seq - CMP_BLOCK) // CMP_STRIDE + 1
    cn = jnp.arange(n_chunks)[None, :] * CMP_STRIDE
    sj = jnp.arange(n_sel)[:, None] * SEL_BLOCK
    overlap = jnp.clip(jnp.minimum(cn + CMP_BLOCK, sj + SEL_BLOCK) - jnp.maximum(cn, sj), 0, None).astype(F32) / CMP_BLOCK
    overlap = jnp.where(jnp.arange(n_chunks)[None, :] < n_cmp, overlap, 0.0)
    expand = (jnp.arange(n_sel)[:, None] == jnp.arange(seq)[None, :] // SEL_BLOCK).astype(BF16)
    n_tiles = 3 * (GQA_GROUP // 2)
    src = jnp.arange(2 * LANES)[:, None] % LANES
    dst = jnp.arange(n_tiles * LANES)[None, :]
    pick = (src == 2 * (dst // LANES) + (dst // HEAD_DIM) % 2).astype(BF16)
    return pl.pallas_call(
        functools.partial(_nsa_kernel, seq=seq),
        grid=(bsz, N_KV_HEADS, n_q),
        in_specs=[pl.BlockSpec((tq, 2 * LANES), qrow),
                  pl.BlockSpec((1, 1, n_chunks, LANES), lambda b, h, i: (0, b * N_KV_HEADS + h, 0, 0)),
                  pl.BlockSpec((1, 1, n_chunks, LANES), lambda b, h, i: (1, b * N_KV_HEADS + h, 0, 0)),
                  kv_spec, kv_spec, kv_spec, kv_spec,
                  pl.BlockSpec((tq, LANES), qrow),
                  _resident(win_bias.shape), _resident(overlap.shape), _resident(expand.shape), _resident(pick.shape)],
        out_specs=pl.BlockSpec((tq, 2 * LANES), qrow),
        out_shape=jax.ShapeDtypeStruct((bsz * seq, ATTN_WIDTH), BF16),
        scratch_shapes=[branch_state(), pltpu.VMEM((rows, WINDOW + tq), BF16), pltpu.VMEM((rows, n_chunks), BF16)],
        compiler_params=_cparams(3),
        name="nsa_attention",
    )(q, cmp_kv, cmp_kv, ksel, vsel, kwin, vwin, ng, win_bias, overlap, expand, pick)


def _merge_kernel(x_ref, ys_ref, yp_ref, yn_ref, bg_ref, wb_ref, wo_ref, g_ref, b_ref, o_ref, *, alpha):
    merged = None
    for k, y_ref in enumerate((ys_ref, yp_ref, yn_ref)):
        term = bg_ref[:, k * D_MODEL:(k + 1) * D_MODEL].astype(F32) * _dot(y_ref[...], wb_ref[k].astype(BF16))
        merged = term if merged is None else merged + term
    r = alpha * x_ref[...] + _dot(merged.astype(BF16), wo_ref[...].astype(BF16))
    o_ref[...] = _layer_norm(r, g_ref[...], b_ref[...])


def _merge(xr, ys, yp, yn, bg, wb_stack, wo_stack, layer, g, b, alpha):
    rows = xr.shape[0]
    tile = PROJ_TILE
    row = lambda i: (i, 0)
    return pl.pallas_call(
        functools.partial(_merge_kernel, alpha=alpha),
        grid=(rows // tile,),
        in_specs=[pl.BlockSpec((tile, D_MODEL), row),
                  pl.BlockSpec((tile, SSM_WIDTH), row), pl.BlockSpec((tile, POOL_WIDTH), row),
                  pl.BlockSpec((tile, ATTN_WIDTH), row), pl.BlockSpec((tile, N_BRANCH * D_MODEL), row),
                  _resident_layer(wb_stack, layer), _resident_layer(wo_stack, layer),
                  _resident(g.shape), _resident(b.shape)],
        out_specs=pl.BlockSpec((tile, D_MODEL), row),
        out_shape=jax.ShapeDtypeStruct((rows, D_MODEL), F32),
        compiler_params=_cparams(1),
        name="branch_merge",
    )(xr, ys, yp, yn, bg, wb_stack, wo_stack, g, b)


def _ffn_kernel(x_ref, wi_ref, wo_ref, g_ref, b_ref, o_ref, *, alpha):
    x = x_ref[...]
    xb = x.astype(BF16)
    acc = None
    for c in range(FF_HIDDEN // FF_CHUNK):
        c0 = c * FF_CHUNK
        hg = _dot(xb, wi_ref[:, c0:c0 + FF_CHUNK].astype(BF16))
        hu = _dot(xb, wi_ref[:, FF_HIDDEN + c0:FF_HIDDEN + c0 + FF_CHUNK].astype(BF16))
        act = (hg * jax.nn.sigmoid(hg) * hu).astype(BF16)
        part = _dot(act, wo_ref[c0:c0 + FF_CHUNK, :].astype(BF16))
        acc = part if acc is None else acc + part
    o_ref[...] = _layer_norm(alpha * x + acc, g_ref[...], b_ref[...])


def _ffn(xr, wi_stack, wo_stack, layer, g, b, alpha):
    rows = xr.shape[0]
    tile = ROW_TILE
    row = lambda i: (i, 0)
    return pl.pallas_call(
        functools.partial(_ffn_kernel, alpha=alpha),
        grid=(rows // tile,),
        in_specs=[pl.BlockSpec((tile, D_MODEL), row), _resident_layer(wi_stack, layer),
                  _resident_layer(wo_stack, layer), _resident(g.shape), _resident(b.shape)],
        out_specs=pl.BlockSpec((tile, D_MODEL), row),
        out_shape=jax.ShapeDtypeStruct((rows, D_MODEL), F32),
        compiler_params=_cparams(1),
        name="swiglu_ffn",
    )(xr, wi_stack, wo_stack, g, b)


def _pack_s5(a_re, a_im, log_dt, b_re, b_im, c_re, c_im):
    depth = a_re.shape[0]
    a_re, a_im = a_re.astype(F32), a_im.astype(F32)
    dt = jnp.exp(log_dt.astype(F32))[..., None]
    mag = jnp.exp(a_re * dt)
    lbar_re, lbar_im = mag * jnp.cos(a_im * dt), mag * jnp.sin(a_im * dt)
    den = a_re * a_re + a_im * a_im
    coef_re = ((lbar_re - 1.0) * a_re + lbar_im * a_im) / den
    coef_im = (lbar_im * a_re - (lbar_re - 1.0) * a_im) / den
    b_re, b_im = b_re.astype(F32), b_im.astype(F32)
    bbar_re = coef_re[..., None] * b_re - coef_im[..., None] * b_im
    bbar_im = coef_re[..., None] * b_im + coef_im[..., None] * b_re
    gpc = S5_LANE_CHUNK // SSM_STATE
    n_chunk = SSM_GROUPS // gpc
    eye = jnp.eye(gpc, dtype=F32)

    def b_block(part):
        v = part.transpose(0, 1, 3, 2).reshape(depth, n_chunk, gpc, SSM_GROUP, SSM_STATE)
        return jnp.einsum('xy,dqxcp->dqxcyp', eye, v).reshape(depth, n_chunk, gpc * SSM_GROUP, gpc * SSM_STATE)

    def c_block(part):
        v = part.reshape(depth, n_chunk, gpc, SSM_GROUP, SSM_STATE)
        return jnp.einsum('xy,dqxcp->dqxpyc', eye, v).reshape(depth, n_chunk, gpc * SSM_STATE, gpc * SSM_GROUP)

    wb = jnp.concatenate([b_block(bbar_re), b_block(bbar_im)], axis=-1).astype(BF16)
    wc = jnp.concatenate([c_block(c_re.astype(F32)), c_block(-c_im.astype(F32))], axis=-2).astype(BF16)
    state_w = SSM_GROUPS * SSM_STATE
    lre = jnp.broadcast_to(lbar_re.reshape(depth, 1, state_w), (depth, SUBLANES, state_w))
    lim = jnp.broadcast_to(lbar_im.reshape(depth, 1, state_w), (depth, SUBLANES, state_w))
    return lre, lim, wb, wc


def kernel(x, positions, w_in, ssm_a_re, ssm_a_im, ssm_log_dt, ssm_b_re, ssm_b_im, ssm_c_re, ssm_c_im,
           ssm_d, ssm_w_glu, pool_w, pool_scale, cmp_pos, cmp_w1, cmp_b1, cmp_w2,
           w_branch, w_out, ln_g, ln_b, ffn_w_in, ffn_w_out):
    bsz, seq, _ = x.shape
    depth = w_in.shape[0]
    rows = bsz * seq
    assert bsz == SUBLANES and seq % PROJ_TILE == 0 and seq >= WINDOW + Q_TILE
    alpha = (2 * depth) ** 0.25

    assert w_in.shape[-1] == IN_RAW
    lre, lim, s5_wb, s5_wc = _pack_s5(ssm_a_re, ssm_a_im, ssm_log_dt, ssm_b_re, ssm_b_im, ssm_c_re, ssm_c_im)
    s5_d = ssm_d.astype(F32).reshape(depth, 1, SSM_WIDTH)
    pool_wb = pool_w.astype(BF16)
    pool_sc = pool_scale.astype(F32).reshape(depth, 1, POOL_WIDTH)
    half = CMP_STRIDE * HEAD_DIM
    cmp_posr = jnp.broadcast_to(cmp_pos.astype(F32).reshape(depth, 2, 1, CMP_BLOCK * HEAD_DIM),
                                (depth, 2, SUBLANES, CMP_BLOCK * HEAD_DIM))
    cmp_b1r = cmp_b1.astype(F32).reshape(depth, 2, 1, CMP_HIDDEN)
    cmp_w2dup = jnp.concatenate([cmp_w2, cmp_w2 * jnp.array([1.0, 0.0], cmp_w2.dtype).reshape(1, 2, 1, 1)],
                                axis=-1).astype(BF16)
    lng = ln_g.astype(F32).reshape(depth, 2, 1, D_MODEL)
    lnb = ln_b.astype(F32).reshape(depth, 2, 1, D_MODEL)

    def pack_w_tail(w):
        w_gate = w[:, COL_NG:COL_NG + N_GATE].reshape(D_MODEL, 3, N_KV_HEADS, GQA_GROUP)
        w_gate = w_gate.transpose(0, 2, 1, 3).reshape(D_MODEL, N_KV_HEADS, 3 * GQA_GROUP)
        w_gate = jnp.pad(w_gate, ((0, 0), (0, 0), (0, LANES - 3 * GQA_GROUP))).reshape(D_MODEL, N_KV_HEADS * LANES)
        return jnp.concatenate([w_gate, w[:, COL_NG + N_GATE:]], axis=-1).astype(BF16)

    cos, sin = _rope_tables(positions)
    xr = x.astype(F32).reshape(rows, D_MODEL)
    for l in range(depth):
        (u_ssm, u_pool, q, cmp_in, ksel, vsel, kwin, vwin, ng, bg) = _inproj(
            xr, w_in, l, pack_w_tail(w_in[l]), cos, sin, bsz, seq)
        y_ssm = _s5(u_ssm.reshape(bsz, seq, SSM_WIDTH), lre[l], lim[l], s5_wb[l], s5_wc[l], s5_d[l], ssm_w_glu, l)
        y_pool = _pool(u_pool.reshape(bsz, seq, POOL_WIDTH), pool_wb[l], pool_sc[l])
        cmp_z = cmp_in.reshape(2, bsz * N_KV_HEADS, seq // CMP_STRIDE, CMP_STRIDE * HEAD_DIM)
        cmp_kv = _compress(cmp_z, cmp_w1, l, cmp_posr[l], cmp_b1r[l], cmp_w2dup[l])
        y_nsa = _nsa(q, cmp_kv, ksel, vsel, kwin, vwin, ng, bsz, seq)
        x1 = _merge(xr, y_ssm.reshape(rows, SSM_WIDTH), y_pool.reshape(rows, POOL_WIDTH), y_nsa, bg,
                    w_branch, w_out, l, lng[l, 0], lnb[l, 0], alpha)
        xr = _ffn(x1, ffn_w_in, ffn_w_out, l, lng[l, 1], lnb[l, 1], alpha)
    return xr.reshape(bsz, seq, D_MODEL).astype(x.dtype)
```

```python
import functools
import math

import jax
import jax.numpy as jnp
from jax import lax
from jax.experimental import pallas as pl
from jax.experimental.pallas import tpu as pltpu

F32 = jnp.float32
BF16 = jnp.bfloat16

D_MODEL = 1024
SSM_WIDTH = 512
SSM_GROUP = 16
SSM_GROUPS = 32
SSM_STATE = 64
POOL_WIDTH = 512
POOL_WINDOWS = (2, 4, 8, 16)
POOL_GROUP = 128
HEAD_DIM = 64
N_HEADS = 8
N_KV_HEADS = 2
GQA_GROUP = 4
ATTN_WIDTH = 512
KV_WIDTH = 128
N_BRANCH = 3
CMP_BLOCK = 32
CMP_STRIDE = 16
CMP_HIDDEN = 256
SEL_BLOCK = 64
SEL_TOP = 16
WINDOW = 512
ROPE_THETA = 10000.0
FF_HIDDEN = 2816
LN_EPS = 1e-5
NEG = -1e30
N_GATE = 3 * N_HEADS
IN_RAW = 3 * 512 + 6 * KV_WIDTH + N_GATE + N_BRANCH * D_MODEL

LANES = 128
SUBLANES = 8
VMEM_LIMIT_BYTES = 58 * 1024 * 1024

COL_SSM = 0
COL_POOL = 512
COL_Q = 1024
COL_KV = 1536
COL_NG = 2304
COL_BG = COL_NG + N_KV_HEADS * LANES
IN_PACKED = COL_BG + N_BRANCH * D_MODEL
LOG2E = 1.4426950408889634

ROW_TILE = 512
PROJ_TILE = 1024
CHUNK_PERM_ROWS = 512
S5_STEPS = 64
S5_LANE_CHUNK = 512
POOL_STEPS = 64
POOL_HALO = 16
Q_TILE = 256
K_TILE = 256
FF_CHUNK = 256


def _cparams(n_axes):
    return pltpu.CompilerParams(dimension_semantics=("arbitrary",) * n_axes,
                                vmem_limit_bytes=VMEM_LIMIT_BYTES)


def _resident(shape):
    nd = len(shape)
    return pl.BlockSpec(shape, lambda *_: (0,) * nd, pipeline_mode=pl.Buffered(1))


def _resident_layer(stacked, layer):
    nd = stacked.ndim
    return pl.BlockSpec((None,) + stacked.shape[1:], lambda *_: (layer,) + (0,) * (nd - 1),
                        pipeline_mode=pl.Buffered(1))


def _gelu_tanh(x):
    return x * (0.5 * (1.0 + jnp.tanh(math.sqrt(2.0 / math.pi) * (x + 0.044715 * (x * x * x)))))


def _layer_norm(r, g, b):
    mu = jnp.mean(r, axis=-1, keepdims=True)
    c = r - mu
    var = jnp.mean(c * c, axis=-1, keepdims=True)
    return c * lax.rsqrt(var + LN_EPS) * g + b


def _dot(a, b):
    return jnp.dot(a, b, preferred_element_type=F32)


def _dot_nt(a, b):
    return lax.dot_general(a, b, (((1,), (1,)), ((), ())), preferred_element_type=F32)


def _rope_table_kernel(pos_ref, inv_ref, cos_ref, sin_ref):
    ang = pos_ref[...] * inv_ref[...]
    lane = lax.broadcasted_iota(jnp.int32, ang.shape, 1)
    first_half = jnp.bitwise_and(lane, HEAD_DIM - 1) < HEAD_DIM // 2
    cos_ref[...] = jnp.cos(ang)
    sin_ref[...] = jnp.where(first_half, -jnp.sin(ang), jnp.sin(ang))


def _rope_tables(positions):
    rows = positions.size
    inv = ROPE_THETA ** (-jnp.arange(0, HEAD_DIM, 2, dtype=F32) / HEAD_DIM)
    inv = jnp.tile(inv, LANES // (HEAD_DIM // 2)).reshape(1, LANES)
    pos = positions.astype(F32).reshape(rows, 1)
    tile = ROW_TILE
    return pl.pallas_call(
        _rope_table_kernel,
        grid=(rows // tile,),
        in_specs=[pl.BlockSpec((tile, 1), lambda i: (i, 0)),
                  pl.BlockSpec((1, LANES), lambda i: (0, 0))],
        out_specs=[pl.BlockSpec((tile, LANES), lambda i: (i, 0))] * 2,
        out_shape=[jax.ShapeDtypeStruct((rows, LANES), F32)] * 2,
        compiler_params=_cparams(1),
        name="rope_tables",
    )(pos, inv)


def _inproj_kernel(x_ref, w_head_ref, w_tail_ref, cos_ref, sin_ref, chunk_perm_ref,
                   ussm_ref, upool_ref, q_ref, cmp_ref, ksel_ref, vsel_ref, kwin_ref, vwin_ref,
                   ng_ref, bg_ref):
    xb = x_ref[...].astype(BF16)
    cos = cos_ref[...]
    sin = sin_ref[...]
    lane = lax.broadcasted_iota(jnp.int32, cos.shape, 1)
    first_half = jnp.bitwise_and(lane, HEAD_DIM - 1) < HEAD_DIM // 2
    left = lane < HEAD_DIM

    def proj(c0, width):
        if c0 < COL_NG:
            return _dot(xb, w_head_ref[:, c0:c0 + width].astype(BF16))
        return _dot(xb, w_tail_ref[:, c0 - COL_NG:c0 - COL_NG + width])

    def rope(t):
        swapped = jnp.where(first_half, pltpu.roll(t, LANES - HEAD_DIM // 2, 1),
                            pltpu.roll(t, HEAD_DIM // 2, 1))
        return t * cos + swapped * sin

    def dup(t):
        r = pltpu.roll(t, HEAD_DIM, 1)
        return jnp.where(left, t, r), jnp.where(left, r, t)

    def with_ones(t):
        return jnp.where(left, t, 1.0), jnp.where(left, pltpu.roll(t, HEAD_DIM, 1), 1.0)

    ussm_ref[...] = proj(COL_SSM, SSM_WIDTH).astype(BF16)
    upool_ref[...] = proj(COL_POOL, POOL_WIDTH).astype(BF16)
    scale = HEAD_DIM ** -0.5 * LOG2E
    for j in range(ATTN_WIDTH // LANES):
        t = rope(proj(COL_Q + j * LANES, LANES))
        q_ref[:, j * LANES:(j + 1) * LANES] = (t * scale).astype(BF16)

    kv = proj(COL_KV, 6 * KV_WIDTH)
    perm_rows = chunk_perm_ref.shape[0]
    n_chunks = perm_rows // CMP_STRIDE
    left_c = lax.broadcasted_iota(jnp.int32, (n_chunks, LANES), 1) < HEAD_DIM
    left_p = lax.broadcasted_iota(jnp.int32, (perm_rows, LANES), 1) < HEAD_DIM
    for j, roped in enumerate((True, False)):
        t = kv[:, j * LANES:(j + 1) * LANES]
        t = (rope(t) if roped else t).astype(BF16)
        for part in range(x_ref.shape[0] // perm_rows):
            by_pos = _dot(chunk_perm_ref[...], t[part * perm_rows:(part + 1) * perm_rows])
            rolled = pltpu.roll(by_pos, HEAD_DIM, 1)
            for h, dup_h in enumerate((jnp.where(left_p, by_pos, rolled), jnp.where(left_p, rolled, by_pos))):
                for i in range(CMP_STRIDE // 2):
                    even = dup_h[(2 * i) * n_chunks:(2 * i + 1) * n_chunks]
                    odd = dup_h[(2 * i + 1) * n_chunks:(2 * i + 2) * n_chunks]
                    cmp_ref[j, 0, h, part * n_chunks:(part + 1) * n_chunks, i * LANES:(i + 1) * LANES] = (
                        jnp.where(left_c, even, odd).astype(BF16))
    for j, (ref, is_key) in enumerate(((ksel_ref, True), (vsel_ref, False), (kwin_ref, True), (vwin_ref, False))):
        t = kv[:, (2 + j) * LANES:(3 + j) * LANES]
        a, b = dup(rope(t)) if is_key else with_ones(t)
        ref[0, :, 0:LANES] = a.astype(BF16)
        ref[0, :, LANES:2 * LANES] = b.astype(BF16)

    ng_ref[...] = jax.nn.sigmoid(proj(COL_NG, N_KV_HEADS * LANES))
    for k in range(N_BRANCH):
        bg_ref[:, k * D_MODEL:(k + 1) * D_MODEL] = jax.nn.sigmoid(proj(COL_BG + k * D_MODEL, D_MODEL)).astype(BF16)


def _inproj(xr, w_stack, layer, w_tail, cos, sin, bsz, seq):
    rows = bsz * seq
    head_spec = pl.BlockSpec((None, D_MODEL, COL_NG), lambda i: (layer, 0, 0), pipeline_mode=pl.Buffered(1))
    tile = PROJ_TILE
    n_s = seq // tile
    row = lambda i: (i, 0)
    dup_spec = pl.BlockSpec((1, tile, 2 * LANES), lambda i: (i // n_s, i % n_s, 0))
    chunk_w = CMP_STRIDE * HEAD_DIM
    r = jnp.arange(CHUNK_PERM_ROWS)
    n_c = CHUNK_PERM_ROWS // CMP_STRIDE
    chunk_perm = (r[None, :] == (r[:, None] % n_c) * CMP_STRIDE + r[:, None] // n_c).astype(BF16)
    out_shape = [
        jax.ShapeDtypeStruct((rows, SSM_WIDTH), BF16),
        jax.ShapeDtypeStruct((rows, POOL_WIDTH), BF16),
        jax.ShapeDtypeStruct((rows, ATTN_WIDTH), BF16),
        jax.ShapeDtypeStruct((2, bsz, N_KV_HEADS, seq // CMP_STRIDE, chunk_w), BF16),
        jax.ShapeDtypeStruct((bsz, seq, 2 * LANES), BF16),
        jax.ShapeDtypeStruct((bsz, seq, 2 * LANES), BF16),
        jax.ShapeDtypeStruct((bsz, seq, 2 * LANES), BF16),
        jax.ShapeDtypeStruct((bsz, seq, 2 * LANES), BF16),
        jax.ShapeDtypeStruct((rows, N_KV_HEADS * LANES), F32),
        jax.ShapeDtypeStruct((rows, N_BRANCH * D_MODEL), BF16),
    ]
    out_specs = [
        pl.BlockSpec((tile, SSM_WIDTH), row),
        pl.BlockSpec((tile, POOL_WIDTH), row),
        pl.BlockSpec((tile, ATTN_WIDTH), row),
        pl.BlockSpec((2, 1, N_KV_HEADS, tile // CMP_STRIDE, chunk_w), lambda i: (0, i // n_s, 0, i % n_s, 0)),
        dup_spec, dup_spec, dup_spec, dup_spec,
        pl.BlockSpec((tile, N_KV_HEADS * LANES), row),
        pl.BlockSpec((tile, N_BRANCH * D_MODEL), row),
    ]
    return pl.pallas_call(
        _inproj_kernel,
        grid=(rows // tile,),
        in_specs=[pl.BlockSpec((tile, D_MODEL), row), head_spec, _resident(w_tail.shape),
                  pl.BlockSpec((tile, LANES), row), pl.BlockSpec((tile, LANES), row), _resident(chunk_perm.shape)],
        out_specs=out_specs,
        out_shape=out_shape,
        compiler_params=_cparams(1),
        name="in_projection",
    )(xr, w_stack, w_tail, cos, sin, chunk_perm)


def _time_major_perms(bsz, steps):
    r = jnp.arange(bsz * steps)
    to_tm = (r[None, :] == (r[:, None] % bsz) * steps + r[:, None] // bsz).astype(BF16)
    return to_tm, to_tm.T


def _s5_kernel(u_ref, tm_ref, bm_ref, lre_ref, lim_ref, wb_ref, wc_ref, d_ref, wglu_ref, o_ref,
               bu_ref, st_ref, y_ref):
    bsz, n_steps, _ = u_ref.shape
    n_rows = bsz * n_steps
    n_chunk = wb_ref.shape[0]
    cw = S5_LANE_CHUNK

    @pl.when(pl.program_id(0) == 0)
    def _():
        st_ref[...] = jnp.zeros(st_ref.shape, F32)

    u = _dot(tm_ref[...], u_ref[...].reshape(n_rows, SSM_WIDTH)).astype(BF16)
    for c in range(n_chunk):
        bu_ref[c] = _dot(u[:, c * LANES:(c + 1) * LANES], wb_ref[c])

    for c in range(n_chunk):
        lre = lre_ref[:, c * cw:(c + 1) * cw]
        lim = lim_ref[:, c * cw:(c + 1) * cw]
        hre = st_ref[0, :, c * cw:(c + 1) * cw]
        him = st_ref[1, :, c * cw:(c + 1) * cw]
        for t in range(n_steps):
            rows = slice(t * SUBLANES, (t + 1) * SUBLANES)
            hre, him = (lre * hre - lim * him + bu_ref[c, rows, 0:cw],
                        lre * him + lim * hre + bu_ref[c, rows, cw:2 * cw])
            bu_ref[c, rows, 0:cw] = hre
            bu_ref[c, rows, cw:2 * cw] = him
        st_ref[0, :, c * cw:(c + 1) * cw] = hre
        st_ref[1, :, c * cw:(c + 1) * cw] = him
        y_ref[:, c * LANES:(c + 1) * LANES] = _dot(bu_ref[c].astype(BF16), wc_ref[c])
    y = y_ref[...] + d_ref[...] * u.astype(F32)
    z = _dot(_gelu_tanh(y).astype(BF16), wglu_ref[...].astype(BF16))
    out = (z[:, :SSM_WIDTH] * jax.nn.sigmoid(z[:, SSM_WIDTH:])).astype(BF16)
    o_ref[...] = _dot(bm_ref[...], out).astype(BF16).reshape(bsz, n_steps, SSM_WIDTH)


def _s5(u, lre, lim, wb, wc, d, wglu_stack, layer):
    bsz, seq, _ = u.shape
    steps = S5_STEPS
    tile = steps * bsz
    state_w = lre.shape[1]
    to_tm, to_bm = _time_major_perms(bsz, steps)
    blk = pl.BlockSpec((bsz, steps, SSM_WIDTH), lambda i: (0, i, 0))
    return pl.pallas_call(
        _s5_kernel,
        grid=(seq // steps,),
        in_specs=[blk, _resident(to_tm.shape), _resident(to_bm.shape),
                  _resident(lre.shape), _resident(lim.shape), _resident(wb.shape), _resident(wc.shape),
                  _resident(d.shape), _resident_layer(wglu_stack, layer)],
        out_specs=blk,
        out_shape=jax.ShapeDtypeStruct(u.shape, BF16),
        scratch_shapes=[pltpu.VMEM((wb.shape[0], tile, 2 * S5_LANE_CHUNK), F32),
                        pltpu.VMEM((2, SUBLANES, state_w), F32),
                        pltpu.VMEM((tile, SSM_WIDTH), F32)],
        compiler_params=_cparams(1),
        name="s5_mixer",
    )(u, to_tm, to_bm, lre, lim, wb, wc, d, wglu_stack)


def _pool_kernel(u_ref, tm_ref, bm_ref, w_ref, sc_ref, o_ref, ext_ref, y_ref):
    bsz, n_steps, _ = u_ref.shape
    n_rows = bsz * n_steps
    halo = POOL_HALO * SUBLANES
    i = pl.program_id(0)

    @pl.when(i == 0)
    def _():
        ext_ref[0:halo, :] = jnp.zeros((halo, POOL_WIDTH), F32)

    ext_ref[halo:halo + n_rows, :] = _dot(tm_ref[...], u_ref[...].reshape(n_rows, POOL_WIDTH))
    row = lax.broadcasted_iota(jnp.int32, (n_rows, POOL_GROUP), 0)
    t = i * (n_rows // SUBLANES) + jnp.right_shift(row, 3)
    for gi, w in enumerate(POOL_WINDOWS):
        c0 = gi * POOL_GROUP
        cur = ext_ref[halo:halo + n_rows, c0:c0 + POOL_GROUP]
        acc = cur
        for k in range(1, w):
            acc = acc + ext_ref[halo - k * SUBLANES:halo - k * SUBLANES + n_rows, c0:c0 + POOL_GROUP]
        cnt = jnp.minimum(t + 1, w).astype(F32)
        pooled = acc / cnt - cur
        mixed = _dot(pooled.astype(BF16), w_ref[gi]) * sc_ref[:, c0:c0 + POOL_GROUP]
        y_ref[:, c0:c0 + POOL_GROUP] = mixed.astype(BF16)
    ext_ref[0:halo, :] = ext_ref[n_rows:n_rows + halo, :]
    o_ref[...] = _dot(bm_ref[...], y_ref[...]).astype(BF16).reshape(bsz, n_steps, POOL_WIDTH)


def _pool(u, w, sc):
    bsz, seq, _ = u.shape
    steps = POOL_STEPS
    tile = steps * bsz
    to_tm, to_bm = _time_major_perms(bsz, steps)
    blk = pl.BlockSpec((bsz, steps, POOL_WIDTH), lambda i: (0, i, 0))
    return pl.pallas_call(
        _pool_kernel,
        grid=(seq // steps,),
        in_specs=[blk, _resident(to_tm.shape), _resident(to_bm.shape), _resident(w.shape), _resident(sc.shape)],
        out_specs=blk,
        out_shape=jax.ShapeDtypeStruct(u.shape, BF16),
        scratch_shapes=[pltpu.VMEM((tile + POOL_HALO * SUBLANES, POOL_WIDTH), F32),
                        pltpu.VMEM((tile, POOL_WIDTH), BF16)],
        compiler_params=_cparams(1),
        name="pool_mixer",
    )(u, to_tm, to_bm, w, sc)


def _compress_kernel(z_ref, w1_ref, pos_ref, b1_ref, w2_ref, o_ref):
    half = CMP_STRIDE * HEAD_DIM
    _, n_bh, n_chunks, width = z_ref.shape
    z = z_ref[0].reshape(n_bh * n_chunks, width)
    w_top = w1_ref[0, :half, :].astype(BF16)
    w_bot = w1_ref[0, half:, :].astype(BF16)
    bottom_next = pltpu.roll(_dot(z, w_bot), n_bh * n_chunks - 1, 0)
    pos = pos_ref[0].astype(BF16)
    cst = _dot(pos[:, :half], w_top) + _dot(pos[:, half:], w_bot)
    hid = _gelu_tanh(_dot(z, w_top) + bottom_next + cst[0:1, :] + b1_ref[0])
    out = _dot(hid.astype(BF16), w2_ref[0])
    lane = lax.broadcasted_iota(jnp.int32, out.shape, 1)
    is_value = pl.program_id(0) == 1
    out = jnp.where((lane >= HEAD_DIM) & is_value, 1.0, out).astype(BF16)
    o_ref[0] = out.reshape(n_bh, n_chunks, LANES)


def _compress(z, w1_stack, layer, pos, b1, w2dup):
    _, n_bh, n_chunks, width = z.shape
    return pl.pallas_call(
        _compress_kernel,
        grid=(2,),
        in_specs=[pl.BlockSpec((1, n_bh, n_chunks, width), lambda j: (j, 0, 0, 0)),
                  pl.BlockSpec((None, 1) + w1_stack.shape[2:], lambda j: (layer, j, 0, 0)),
                  pl.BlockSpec((1,) + pos.shape[1:], lambda j: (j, 0, 0)),
                  pl.BlockSpec((1,) + b1.shape[1:], lambda j: (j, 0, 0)),
                  pl.BlockSpec((1,) + w2dup.shape[1:], lambda j: (j, 0, 0))],
        out_specs=pl.BlockSpec((1, n_bh, n_chunks, LANES), lambda j: (j, 0, 0, 0)),
        out_shape=jax.ShapeDtypeStruct((2, n_bh, n_chunks, LANES), BF16),
        compiler_params=_cparams(1),
        name="compress_mlp",
    )(z, w1_stack, pos, b1, w2dup)


def _nsa_kernel(q_ref, kc_ref, vc_ref, ks_ref, vs_ref, kw_ref, vw_ref, g_ref,
                win_bias_ref, overlap_ref, expand_ref, pick_ref, o_ref,
                sel_state, pw_ref, pc_ref, *, seq):
    tq = q_ref.shape[0]
    tk = sel_state[0].shape[1]
    n_sel = seq // SEL_BLOCK
    n_top = min(SEL_TOP, n_sel)
    n_cmp = (seq - CMP_BLOCK) // CMP_STRIDE + 1
    t0 = pl.program_id(2) * tq
    head_rows = [slice(g * tq, (g + 1) * tq) for g in range(GQA_GROUP)]

    lane = lax.broadcasted_iota(jnp.int32, (tq, LANES), 1)
    left = lane < HEAD_DIM
    zero = jnp.zeros((tq, LANES), BF16)
    pairs = (q_ref[:, 0:LANES], q_ref[:, LANES:2 * LANES])
    q4 = jnp.concatenate([jnp.where(left, pairs[0], zero), jnp.where(left, zero, pairs[0]),
                          jnp.where(left, pairs[1], zero), jnp.where(left, zero, pairs[1])], axis=0)
    tq_col = t0 + lax.broadcasted_iota(jnp.int32, (tq, 1), 0)

    def online_branch(state, k_ref, v_ref, tile_start, tile_bias):
        buf_a, buf_b, p_ref, m_ref, a_ref, acc_ref = state
        m_ref[...] = jnp.full(m_ref.shape, NEG, F32)
        acc_ref[...] = jnp.zeros(acc_ref.shape, F32)

        def scores(kt, dst_ref):
            bias = tile_bias(kt)
            s = _dot_nt(q4, k_ref[0, pl.ds(tile_start(kt), tk), :])
            for rows in head_rows:
                dst_ref[rows, :] = s[rows] + bias

        def consume(kt, src_ref):
            for rows in head_rows:
                s = src_ref[rows, :]
                m_old = m_ref[rows, :]
                m_new = jnp.maximum(m_old, jnp.max(s, axis=-1, keepdims=True))
                a_ref[rows, :] = jnp.exp2(m_old - m_new)
                m_ref[rows, :] = m_new
                p_ref[rows, :] = jnp.exp2(s - jnp.concatenate([m_new] * (tk // LANES), axis=1)).astype(BF16)
            acc_ref[...] = a_ref[...] * acc_ref[...] + _dot(p_ref[...], v_ref[0, pl.ds(tile_start(kt), tk), :])

        return buf_a, buf_b, scores, consume, acc_ref


    wk = pw_ref.shape[1]
    w0 = pl.multiple_of(jnp.maximum(t0 - WINDOW, 0), tq)
    bias_w = win_bias_ref[jnp.minimum(pl.program_id(2), win_bias_ref.shape[0] - 1)]
    s_w = _dot_nt(q4, kw_ref[0, pl.ds(w0, wk), :])
    for rows in head_rows:
        s = s_w[rows] + bias_w
        pw_ref[rows, :] = jnp.exp2(s - jnp.max(s, axis=-1, keepdims=True)).astype(BF16)
    acc_w = _dot(pw_ref[...], vw_ref[0, pl.ds(w0, wk), :])

    n_idx = lax.broadcasted_iota(jnp.int32, (1, kc_ref.shape[2]), 1)
    cmp_valid = (n_idx * CMP_STRIDE + (CMP_BLOCK - 1) <= tq_col) & (n_idx < n_cmp)
    s_c = _dot_nt(q4, kc_ref[0, 0])
    p_sum = None
    for rows in head_rows:
        s = jnp.where(cmp_valid, s_c[rows], NEG)
        e = jnp.where(cmp_valid, jnp.exp2(s - jnp.max(s, axis=-1, keepdims=True)), 0.0)
        l = jnp.sum(e, axis=-1, keepdims=True)
        p = e / jnp.where(l > 0.0, l, 1.0)
        p_sum = p if p_sum is None else p_sum + p
        pc_ref[rows, :] = p.astype(BF16)
    o_cmp = _dot(pc_ref[...], vc_ref[0, 0])

    imp = lax.dot_general(overlap_ref[...], p_sum, (((1,), (1,)), ((), ())), precision=lax.Precision.HIGHEST,
                          preferred_element_type=F32)
    jb = lax.broadcasted_iota(jnp.int32, (n_sel, tq), 0)
    tt = t0 + lax.broadcasted_iota(jnp.int32, (n_sel, tq), 1)
    cur = jnp.right_shift(tt, 6)
    forced = (jb == 0) | (jb == cur) | (jb == cur - 1)
    causal = jb * SEL_BLOCK <= tt
    score = jnp.where(forced, 1e30, jnp.where(causal, imp, NEG))
    rank = jnp.zeros((n_sel, tq), F32)
    for a in range(n_sel):
        sa = score[a:a + 1, :]
        ahead = (sa > score) | ((sa == score) & (jb > a))
        rank = rank + jnp.where(ahead, 1.0, 0.0)
    chosen = (rank < float(n_top)) & causal
    sel_bias = jnp.where(chosen, 0.0, NEG).T.astype(BF16)

    def sel_bias_tile(kt):
        k0 = kt * tk
        expand = expand_ref[:, pl.ds(tile_start(kt), tk)]
        bias = _dot(sel_bias, expand)
        return jnp.where(k0 + lax.broadcasted_iota(jnp.int32, (1, tk), 1) <= tq_col, bias, NEG)

    def tile_start(kt):
        return pl.multiple_of(jnp.minimum(kt * tk, seq - tk), tk)

    sa, sb, sel_scores, sel_consume, sel_acc = online_branch(
        sel_state, ks_ref, vs_ref, tile_start, sel_bias_tile)

    n_kt = (t0 + tq + tk - 1) // tk

    def pair_step(j, carry):
        sel_scores(2 * j + 1, sb)
        sel_consume(2 * j, sa)
        sel_scores(2 * j + 2, sa)
        sel_consume(2 * j + 1, sb)
        return carry

    sel_scores(0, sa)
    lax.fori_loop(0, (n_kt + 1) // 2, pair_step, 0)
    acc_s = sel_acc[...]

    g = g_ref[...]
    g_hi = g.astype(BF16)
    g_lo = (g - g_hi.astype(F32)).astype(BF16)
    gates = _dot(jnp.concatenate([g_hi, g_lo], axis=1), pick_ref[...])

    def pair_tile(acc, pair, normalise):
        a, b = acc[head_rows[2 * pair]], acc[head_rows[2 * pair + 1]]
        num = jnp.where(left, a, pltpu.roll(b, HEAD_DIM, 1))
        return num / jnp.where(left, pltpu.roll(a, HEAD_DIM, 1), b) if normalise else num

    for pair in range(GQA_GROUP // 2):
        out = None
        for br, (acc, normalise) in enumerate(((o_cmp, False), (acc_s, True), (acc_w, True))):
            tile = br * (GQA_GROUP // 2) + pair
            term = gates[:, tile * LANES:(tile + 1) * LANES] * pair_tile(acc, pair, normalise)
            out = term if out is None else out + term
        o_ref[:, pair * LANES:(pair + 1) * LANES] = out.astype(BF16)


def _nsa(q, cmp_kv, ksel, vsel, kwin, vwin, ng, bsz, seq):
    tq = Q_TILE
    n_q = seq // tq
    n_chunks = cmp_kv.shape[2]
    rows = GQA_GROUP * tq
    assert (WINDOW + tq) % K_TILE == 0 and seq % K_TILE == 0
    qrow = lambda b, h, i: (b * n_q + i, h)
    kv_spec = pl.BlockSpec((1, seq, LANES), lambda b, h, i: (b, 0, h))

    def branch_state():
        return (pltpu.VMEM((rows, K_TILE), F32), pltpu.VMEM((rows, K_TILE), F32), pltpu.VMEM((rows, K_TILE), BF16),
                pltpu.VMEM((rows, LANES), F32), pltpu.VMEM((rows, LANES), F32), pltpu.VMEM((rows, LANES), F32))

    wk = WINDOW + tq
    case = jnp.arange(WINDOW // tq + 1)[:, None, None]
    dist = jnp.minimum(case * tq, WINDOW) + jnp.arange(tq)[None, :, None] - jnp.arange(wk)[None, None, :]
    win_bias = jnp.where((dist >= 0) & (dist < WINDOW), 0.0, NEG).astype(F32)
    n_sel = seq // SEL_BLOCK
    n_cmp = (seq - CMP_BLOCK) // CMP_STRIDE + 1
    cn = jnp.arange(n_chunks)[None, :] * CMP_STRIDE
    sj = jnp.arange(n_sel)[:, None] * SEL_BLOCK
    overlap = jnp.clip(jnp.minimum(cn + CMP_BLOCK, sj + SEL_BLOCK) - jnp.maximum(cn, sj), 0, None).astype(F32) / CMP_BLOCK
    overlap = jnp.where(jnp.arange(n_chunks)[None, :] < n_cmp, overlap, 0.0)
    expand = (jnp.arange(n_sel)[:, None] == jnp.arange(seq)[None, :] // SEL_BLOCK).astype(BF16)
    n_tiles = 3 * (GQA_GROUP // 2)
    src = jnp.arange(2 * LANES)[:, None] % LANES
    dst = jnp.arange(n_tiles * LANES)[None, :]
    pick = (src == 2 * (dst // LANES) + (dst // HEAD_DIM) % 2).astype(BF16)
    return pl.pallas_call(
        functools.partial(_nsa_kernel, seq=seq),
        grid=(bsz, N_KV_HEADS, n_q),
        in_specs=[pl.BlockSpec((tq, 2 * LANES), qrow),
                  pl.BlockSpec((1, 1, n_chunks, LANES), lambda b, h, i: (0, b * N_KV_HEADS + h, 0, 0)),
                  pl.BlockSpec((1, 1, n_chunks, LANES), lambda b, h, i: (1, b * N_KV_HEADS + h, 0, 0)),
                  kv_spec, kv_spec, kv_spec, kv_spec,
                  pl.BlockSpec((tq, LANES), qrow),
                  _resident(win_bias.shape), _resident(overlap.shape), _resident(expand.shape), _resident(pick.shape)],
        out_specs=pl.BlockSpec((tq, 2 * LANES), qrow),
        out_shape=jax.ShapeDtypeStruct((bsz * seq, ATTN_WIDTH), BF16),
        scratch_shapes=[branch_state(), pltpu.VMEM((rows, WINDOW + tq), BF16), pltpu.VMEM((rows, n_chunks), BF16)],
        compiler_params=_cparams(3),
        name="nsa_attention",
    )(q, cmp_kv, cmp_kv, ksel, vsel, kwin, vwin, ng, win_bias, overlap, expand, pick)


def _merge_kernel(x_ref, ys_ref, yp_ref, yn_ref, bg_ref, wb_ref, wo_ref, g_ref, b_ref, o_ref, *, alpha):
    merged = None
    for k, y_ref in enumerate((ys_ref, yp_ref, yn_ref)):
        term = bg_ref[:, k * D_MODEL:(k + 1) * D_MODEL].astype(F32) * _dot(y_ref[...], wb_ref[k].astype(BF16))
        merged = term if merged is None else merged + term
    r = alpha * x_ref[...] + _dot(merged.astype(BF16), wo_ref[...].astype(BF16))
    o_ref[...] = _layer_norm(r, g_ref[...], b_ref[...])


def _merge(xr, ys, yp, yn, bg, wb_stack, wo_stack, layer, g, b, alpha):
    rows = xr.shape[0]
    tile = PROJ_TILE
    row = lambda i: (i, 0)
    return pl.pallas_call(
        functools.partial(_merge_kernel, alpha=alpha),
        grid=(rows // tile,),
        in_specs=[pl.BlockSpec((tile, D_MODEL), row),
                  pl.BlockSpec((tile, SSM_WIDTH), row), pl.BlockSpec((tile, POOL_WIDTH), row),
                  pl.BlockSpec((tile, ATTN_WIDTH), row), pl.BlockSpec((tile, N_BRANCH * D_MODEL), row),
                  _resident_layer(wb_stack, layer), _resident_layer(wo_stack, layer),
                  _resident(g.shape), _resident(b.shape)],
        out_specs=pl.BlockSpec((tile, D_MODEL), row),
        out_shape=jax.ShapeDtypeStruct((rows, D_MODEL), F32),
        compiler_params=_cparams(1),
        name="branch_merge",
    )(xr, ys, yp, yn, bg, wb_stack, wo_stack, g, b)


def _ffn_kernel(x_ref, wi_ref, wo_ref, g_ref, b_ref, o_ref, *, alpha):
    x = x_ref[...]
    xb = x.astype(BF16)
    acc = None
    for c in range(FF_HIDDEN // FF_CHUNK):
        c0 = c * FF_CHUNK
        hg = _dot(xb, wi_ref[:, c0:c0 + FF_CHUNK].astype(BF16))
        hu = _dot(xb, wi_ref[:, FF_HIDDEN + c0:FF_HIDDEN + c0 + FF_CHUNK].astype(BF16))
        act = (hg * jax.nn.sigmoid(hg) * hu).astype(BF16)
        part = _dot(act, wo_ref[c0:c0 + FF_CHUNK, :].astype(BF16))
        acc = part if acc is None else acc + part
    o_ref[...] = _layer_norm(alpha * x + acc, g_ref[...], b_ref[...])


def _ffn(xr, wi_stack, wo_stack, layer, g, b, alpha):
    rows = xr.shape[0]
    tile = ROW_TILE
    row = lambda i: (i, 0)
    return pl.pallas_call(
        functools.partial(_ffn_kernel, alpha=alpha),
        grid=(rows // tile,),
        in_specs=[pl.BlockSpec((tile, D_MODEL), row), _resident_layer(wi_stack, layer),
                  _resident_layer(wo_stack, layer), _resident(g.shape), _resident(b.shape)],
        out_specs=pl.BlockSpec((tile, D_MODEL), row),
        out_shape=jax.ShapeDtypeStruct((rows, D_MODEL), F32),
        compiler_params=_cparams(1),
        name="swiglu_ffn",
    )(xr, wi_stack, wo_stack, g, b)


def _pack_s5(a_re, a_im, log_dt, b_re, b_im, c_re, c_im):
    depth = a_re.shape[0]
    a_re, a_im = a_re.astype(F32), a_im.astype(F32)
    dt = jnp.exp(log_dt.astype(F32))[..., None]
    mag = jnp.exp(a_re * dt)
    lbar_re, lbar_im = mag * jnp.cos(a_im * dt), mag * jnp.sin(a_im * dt)
    den = a_re * a_re + a_im * a_im
    coef_re = ((lbar_re - 1.0) * a_re + lbar_im * a_im) / den
    coef_im = (lbar_im * a_re - (lbar_re - 1.0) * a_im) / den
    b_re, b_im = b_re.astype(F32), b_im.astype(F32)
    bbar_re = coef_re[..., None] * b_re - coef_im[..., None] * b_im
    bbar_im = coef_re[..., None] * b_im + coef_im[..., None] * b_re
    gpc = S5_LANE_CHUNK // SSM_STATE
    n_chunk = SSM_GROUPS // gpc
    eye = jnp.eye(gpc, dtype=F32)

    def b_block(part):
        v = part.transpose(0, 1, 3, 2).reshape(depth, n_chunk, gpc, SSM_GROUP, SSM_STATE)
        return jnp.einsum('xy,dqxcp->dqxcyp', eye, v).reshape(depth, n_chunk, gpc * SSM_GROUP, gpc * SSM_STATE)

    def c_block(part):
        v = part.reshape(depth, n_chunk, gpc, SSM_GROUP, SSM_STATE)
        return jnp.einsum('xy,dqxcp->dqxpyc', eye, v).reshape(depth, n_chunk, gpc * SSM_STATE, gpc * SSM_GROUP)

    wb = jnp.concatenate([b_block(bbar_re), b_block(bbar_im)], axis=-1).astype(BF16)
    wc = jnp.concatenate([c_block(c_re.astype(F32)), c_block(-c_im.astype(F32))], axis=-2).astype(BF16)
    state_w = SSM_GROUPS * SSM_STATE
    lre = jnp.broadcast_to(lbar_re.reshape(depth, 1, state_w), (depth, SUBLANES, state_w))
    lim = jnp.broadcast_to(lbar_im.reshape(depth, 1, state_w), (depth, SUBLANES, state_w))
    return lre, lim, wb, wc


def kernel(x, positions, w_in, ssm_a_re, ssm_a_im, ssm_log_dt, ssm_b_re, ssm_b_im, ssm_c_re, ssm_c_im,
           ssm_d, ssm_w_glu, pool_w, pool_scale, cmp_pos, cmp_w1, cmp_b1, cmp_w2,
           w_branch, w_out, ln_g, ln_b, ffn_w_in, ffn_w_out):
    bsz, seq, _ = x.shape
    depth = w_in.shape[0]
    rows = bsz * seq
    assert bsz == SUBLANES and seq % PROJ_TILE == 0 and seq >= WINDOW + Q_TILE
    alpha = (2 * depth) ** 0.25

    assert w_in.shape[-1] == IN_RAW
    lre, lim, s5_wb, s5_wc = _pack_s5(ssm_a_re, ssm_a_im, ssm_log_dt, ssm_b_re, ssm_b_im, ssm_c_re, ssm_c_im)
    s5_d = ssm_d.astype(F32).reshape(depth, 1, SSM_WIDTH)
    pool_wb = pool_w.astype(BF16)
    pool_sc = pool_scale.astype(F32).reshape(depth, 1, POOL_WIDTH)
    half = CMP_STRIDE * HEAD_DIM
    cmp_posr = jnp.broadcast_to(cmp_pos.astype(F32).reshape(depth, 2, 1, CMP_BLOCK * HEAD_DIM),
                                (depth, 2, SUBLANES, CMP_BLOCK * HEAD_DIM))
    cmp_b1r = cmp_b1.astype(F32).reshape(depth, 2, 1, CMP_HIDDEN)
    cmp_w2dup = jnp.concatenate([cmp_w2, cmp_w2 * jnp.array([1.0, 0.0], cmp_w2.dtype).reshape(1, 2, 1, 1)],
                                axis=-1).astype(BF16)
    lng = ln_g.astype(F32).reshape(depth, 2, 1, D_MODEL)
    lnb = ln_b.astype(F32).reshape(depth, 2, 1, D_MODEL)

    def pack_w_tail(w):
        w_gate = w[:, COL_NG:COL_NG + N_GATE].reshape(D_MODEL, 3, N_KV_HEADS, GQA_GROUP)
        w_gate = w_gate.transpose(0, 2, 1, 3).reshape(D_MODEL, N_KV_HEADS, 3 * GQA_GROUP)
        w_gate = jnp.pad(w_gate, ((0, 0), (0, 0), (0, LANES - 3 * GQA_GROUP))).reshape(D_MODEL, N_KV_HEADS * LANES)
        return jnp.concatenate([w_gate, w[:, COL_NG + N_GATE:]], axis=-1).astype(BF16)

    cos, sin = _rope_tables(positions)
    xr = x.astype(F32).reshape(rows, D_MODEL)
    w_in_head = w_in[:, :, :COL_NG]
    for l in range(depth):
        (u_ssm, u_pool, q, cmp_in, ksel, vsel, kwin, vwin, ng, bg) = _inproj(
            xr, w_in_head, l, pack_w_tail(w_in[l]), cos, sin, bsz, seq)
        y_ssm = _s5(u_ssm.reshape(bsz, seq, SSM_WIDTH), lre[l], lim[l], s5_wb[l], s5_wc[l], s5_d[l], ssm_w_glu, l)
        y_pool = _pool(u_pool.reshape(bsz, seq, POOL_WIDTH), pool_wb[l], pool_sc[l])
        cmp_z = cmp_in.reshape(2, bsz * N_KV_HEADS, seq // CMP_STRIDE, CMP_STRIDE * HEAD_DIM)
        cmp_kv = _compress(cmp_z, cmp_w1, l, cmp_posr[l], cmp_b1r[l], cmp_w2dup[l])
        y_nsa = _nsa(q, cmp_kv, ksel, vsel, kwin, vwin, ng, bsz, seq)
        x1 = _merge(xr, y_ssm.reshape(rows, SSM_WIDTH), y_pool.reshape(rows, POOL_WIDTH), y_nsa, bg,
                    w_branch, w_out, l, lng[l, 0], lnb[l, 0], alpha)
        xr = _ffn(x1, ffn_w_in, ffn_w_out, l, lng[l, 1], lnb[l, 1], alpha)
    return xr.reshape(bsz, seq, D_MODEL).astype(x.dtype)
```

```python
import functools
import math

import jax
import jax.numpy as jnp
from jax import lax
from jax.experimental import pallas as pl
from jax.experimental.pallas import tpu as pltpu

F32 = jnp.float32
BF16 = jnp.bfloat16

D_MODEL = 1024
SSM_WIDTH = 512
SSM_GROUP = 16
SSM_GROUPS = 32
SSM_STATE = 64
POOL_WIDTH = 512
POOL_WINDOWS = (2, 4, 8, 16)
POOL_GROUP = 128
HEAD_DIM = 64
N_HEADS = 8
N_KV_HEADS = 2
GQA_GROUP = 4
ATTN_WIDTH = 512
KV_WIDTH = 128
N_BRANCH = 3
CMP_BLOCK = 32
CMP_STRIDE = 16
CMP_HIDDEN = 256
SEL_BLOCK = 64
SEL_TOP = 16
WINDOW = 512
ROPE_THETA = 10000.0
FF_HIDDEN = 2816
LN_EPS = 1e-5
NEG = -1e30
N_GATE = 3 * N_HEADS
IN_RAW = 3 * 512 + 6 * KV_WIDTH + N_GATE + N_BRANCH * D_MODEL

LANES = 128
SUBLANES = 8
VMEM_LIMIT_BYTES = 58 * 1024 * 1024

COL_SSM = 0
COL_POOL = 512
COL_Q = 1024
COL_KV = 1536
COL_NG = 2304
COL_BG = COL_NG + N_KV_HEADS * LANES
IN_PACKED = COL_BG + N_BRANCH * D_MODEL
LOG2E = 1.4426950408889634

ROW_TILE = 512
PROJ_TILE = 1024
CHUNK_PERM_ROWS = 512
S5_STEPS = 64
S5_LANE_CHUNK = 512
POOL_STEPS = 64
POOL_HALO = 16
Q_TILE = 256
K_TILE = 256
FF_CHUNK = 256


def _cparams(n_axes):
    return pltpu.CompilerParams(dimension_semantics=("arbitrary",) * n_axes,
                                vmem_limit_bytes=VMEM_LIMIT_BYTES)


def _resident(shape):
    nd = len(shape)
    return pl.BlockSpec(shape, lambda *_: (0,) * nd, pipeline_mode=pl.Buffered(1))


def _resident_layer(stacked, layer):
    nd = stacked.ndim
    return pl.BlockSpec((None,) + stacked.shape[1:], lambda *_: (layer,) + (0,) * (nd - 1),
                        pipeline_mode=pl.Buffered(1))


def _gelu_tanh(x):
    return x * (0.5 * (1.0 + jnp.tanh(math.sqrt(2.0 / math.pi) * (x + 0.044715 * (x * x * x)))))


def _layer_norm(r, g, b):
    mu = jnp.mean(r, axis=-1, keepdims=True)
    c = r - mu
    var = jnp.mean(c * c, axis=-1, keepdims=True)
    return c * lax.rsqrt(var + LN_EPS) * g + b


def _dot(a, b):
    return jnp.dot(a, b, preferred_element_type=F32)


def _dot_nt(a, b):
    return lax.dot_general(a, b, (((1,), (1,)), ((), ())), preferred_element_type=F32)


def _rope_table_kernel(pos_ref, inv_ref, cos_ref, sin_ref):
    ang = pos_ref[...] * inv_ref[...]
    lane = lax.broadcasted_iota(jnp.int32, ang.shape, 1)
    first_half = jnp.bitwise_and(lane, HEAD_DIM - 1) < HEAD_DIM // 2
    cos_ref[...] = jnp.cos(ang)
    sin_ref[...] = jnp.where(first_half, -jnp.sin(ang), jnp.sin(ang))


def _rope_tables(positions):
    rows = positions.size
    inv = ROPE_THETA ** (-jnp.arange(0, HEAD_DIM, 2, dtype=F32) / HEAD_DIM)
    inv = jnp.tile(inv, LANES // (HEAD_DIM // 2)).reshape(1, LANES)
    pos = positions.astype(F32).reshape(rows, 1)
    tile = ROW_TILE
    return pl.pallas_call(
        _rope_table_kernel,
        grid=(rows // tile,),
        in_specs=[pl.BlockSpec((tile, 1), lambda i: (i, 0)),
                  pl.BlockSpec((1, LANES), lambda i: (0, 0))],
        out_specs=[pl.BlockSpec((tile, LANES), lambda i: (i, 0))] * 2,
        out_shape=[jax.ShapeDtypeStruct((rows, LANES), F32)] * 2,
        compiler_params=_cparams(1),
        name="rope_tables",
    )(pos, inv)


def _inproj_kernel(x_ref, w_head_ref, w_tail_ref, cos_ref, sin_ref, chunk_perm_ref,
                   ussm_ref, upool_ref, q_ref, cmp_ref, ksel_ref, vsel_ref, kwin_ref, vwin_ref,
                   ng_ref, bg_ref):
    xb = x_ref[...].astype(BF16)
    cos = cos_ref[...]
    sin = sin_ref[...]
    lane = lax.broadcasted_iota(jnp.int32, cos.shape, 1)
    first_half = jnp.bitwise_and(lane, HEAD_DIM - 1) < HEAD_DIM // 2
    left = lane < HEAD_DIM

    def proj(c0, width):
        if c0 < COL_NG:
            return _dot(xb, w_head_ref[:, c0:c0 + width].astype(BF16))
        return _dot(xb, w_tail_ref[:, c0 - COL_NG:c0 - COL_NG + width])

    def rope(t):
        swapped = jnp.where(first_half, pltpu.roll(t, LANES - HEAD_DIM // 2, 1),
                            pltpu.roll(t, HEAD_DIM // 2, 1))
        return t * cos + swapped * sin

    def dup(t):
        r = pltpu.roll(t, HEAD_DIM, 1)
        return jnp.where(left, t, r), jnp.where(left, r, t)

    def with_ones(t):
        return jnp.where(left, t, 1.0), jnp.where(left, pltpu.roll(t, HEAD_DIM, 1), 1.0)

    ussm_ref[...] = proj(COL_SSM, SSM_WIDTH).astype(BF16)
    upool_ref[...] = proj(COL_POOL, POOL_WIDTH).astype(BF16)
    scale = HEAD_DIM ** -0.5 * LOG2E
    for j in range(ATTN_WIDTH // LANES):
        t = rope(proj(COL_Q + j * LANES, LANES))
        q_ref[:, j * LANES:(j + 1) * LANES] = (t * scale).astype(BF16)

    kv = proj(COL_KV, 6 * KV_WIDTH)
    perm_rows = chunk_perm_ref.shape[0]
    n_chunks = perm_rows // CMP_STRIDE
    left_c = lax.broadcasted_iota(jnp.int32, (n_chunks, LANES), 1) < HEAD_DIM
    left_p = lax.broadcasted_iota(jnp.int32, (perm_rows, LANES), 1) < HEAD_DIM
    for j, roped in enumerate((True, False)):
        t = kv[:, j * LANES:(j + 1) * LANES]
        t = (rope(t) if roped else t).astype(BF16)
        for part in range(x_ref.shape[0] // perm_rows):
            by_pos = _dot(chunk_perm_ref[...], t[part * perm_rows:(part + 1) * perm_rows])
            rolled = pltpu.roll(by_pos, HEAD_DIM, 1)
            for h, dup_h in enumerate((jnp.where(left_p, by_pos, rolled), jnp.where(left_p, rolled, by_pos))):
                for i in range(CMP_STRIDE // 2):
                    even = dup_h[(2 * i) * n_chunks:(2 * i + 1) * n_chunks]
                    odd = dup_h[(2 * i + 1) * n_chunks:(2 * i + 2) * n_chunks]
                    cmp_ref[j, 0, h, part * n_chunks:(part + 1) * n_chunks, i * LANES:(i + 1) * LANES] = (
                        jnp.where(left_c, even, odd).astype(BF16))
    for j, (ref, is_key) in enumerate(((ksel_ref, True), (vsel_ref, False), (kwin_ref, True), (vwin_ref, False))):
        t = kv[:, (2 + j) * LANES:(3 + j) * LANES]
        a, b = dup(rope(t)) if is_key else with_ones(t)
        ref[0, :, 0:LANES] = a.astype(BF16)
        ref[0, :, LANES:2 * LANES] = b.astype(BF16)

    ng_ref[...] = jax.nn.sigmoid(proj(COL_NG, N_KV_HEADS * LANES))
    for k in range(N_BRANCH):
        bg_ref[:, k * D_MODEL:(k + 1) * D_MODEL] = jax.nn.sigmoid(proj(COL_BG + k * D_MODEL, D_MODEL)).astype(BF16)


def _inproj(xr, w_stack, layer, w_tail, cos, sin, bsz, seq):
    rows = bsz * seq
    head_spec = pl.BlockSpec((None, D_MODEL, COL_NG), lambda i: (layer, 0, 0), pipeline_mode=pl.Buffered(1))
    tile = PROJ_TILE
    n_s = seq // tile
    row = lambda i: (i, 0)
    dup_spec = pl.BlockSpec((1, tile, 2 * LANES), lambda i: (i // n_s, i % n_s, 0))
    chunk_w = CMP_STRIDE * HEAD_DIM
    r = jnp.arange(CHUNK_PERM_ROWS)
    n_c = CHUNK_PERM_ROWS // CMP_STRIDE
    chunk_perm = (r[None, :] == (r[:, None] % n_c) * CMP_STRIDE + r[:, None] // n_c).astype(BF16)
    out_shape = [
        jax.ShapeDtypeStruct((rows, SSM_WIDTH), BF16),
        jax.ShapeDtypeStruct((rows, POOL_WIDTH), BF16),
        jax.ShapeDtypeStruct((rows, ATTN_WIDTH), BF16),
        jax.ShapeDtypeStruct((2, bsz, N_KV_HEADS, seq // CMP_STRIDE, chunk_w), BF16),
        jax.ShapeDtypeStruct((bsz, seq, 2 * LANES), BF16),
        jax.ShapeDtypeStruct((bsz, seq, 2 * LANES), BF16),
        jax.ShapeDtypeStruct((bsz, seq, 2 * LANES), BF16),
        jax.ShapeDtypeStruct((bsz, seq, 2 * LANES), BF16),
        jax.ShapeDtypeStruct((rows, N_KV_HEADS * LANES), F32),
        jax.ShapeDtypeStruct((rows, N_BRANCH * D_MODEL), BF16),
    ]
    out_specs = [
        pl.BlockSpec((tile, SSM_WIDTH), row),
        pl.BlockSpec((tile, POOL_WIDTH), row),
        pl.BlockSpec((tile, ATTN_WIDTH), row),
        pl.BlockSpec((2, 1, N_KV_HEADS, tile // CMP_STRIDE, chunk_w), lambda i: (0, i // n_s, 0, i % n_s, 0)),
        dup_spec, dup_spec, dup_spec, dup_spec,
        pl.BlockSpec((tile, N_KV_HEADS * LANES), row),
        pl.BlockSpec((tile, N_BRANCH * D_MODEL), row),
    ]
    return pl.pallas_call(
        _inproj_kernel,
        grid=(rows // tile,),
        in_specs=[pl.BlockSpec((tile, D_MODEL), row), head_spec, _resident(w_tail.shape),
                  pl.BlockSpec((tile, LANES), row), pl.BlockSpec((tile, LANES), row), _resident(chunk_perm.shape)],
        out_specs=out_specs,
        out_shape=out_shape,
        compiler_params=_cparams(1),
        name="in_projection",
    )(xr, w_stack, w_tail, cos, sin, chunk_perm)


def _time_major_perms(bsz, steps):
    r = jnp.arange(bsz * steps)
    to_tm = (r[None, :] == (r[:, None] % bsz) * steps + r[:, None] // bsz).astype(BF16)
    return to_tm, to_tm.T


def _s5_kernel(u_ref, tm_ref, bm_ref, lre_ref, lim_ref, wb_ref, wc_ref, d_ref, wglu_ref, o_ref,
               bu_ref, st_ref, y_ref):
    bsz, n_steps, _ = u_ref.shape
    n_rows = bsz * n_steps
    n_chunk = wb_ref.shape[0]
    cw = S5_LANE_CHUNK

    @pl.when(pl.program_id(0) == 0)
    def _():
        st_ref[...] = jnp.zeros(st_ref.shape, F32)

    u = _dot(tm_ref[...], u_ref[...].reshape(n_rows, SSM_WIDTH)).astype(BF16)
    for c in range(n_chunk):
        bu_ref[c] = _dot(u[:, c * LANES:(c + 1) * LANES], wb_ref[c])

    for c in range(n_chunk):
        lre = lre_ref[:, c * cw:(c + 1) * cw]
        lim = lim_ref[:, c * cw:(c + 1) * cw]
        hre = st_ref[0, :, c * cw:(c + 1) * cw]
        him = st_ref[1, :, c * cw:(c + 1) * cw]
        for t in range(n_steps):
            rows = slice(t * SUBLANES, (t + 1) * SUBLANES)
            hre, him = (lre * hre - lim * him + bu_ref[c, rows, 0:cw],
                        lre * him + lim * hre + bu_ref[c, rows, cw:2 * cw])
            bu_ref[c, rows, 0:cw] = hre
            bu_ref[c, rows, cw:2 * cw] = him
        st_ref[0, :, c * cw:(c + 1) * cw] = hre
        st_ref[1, :, c * cw:(c + 1) * cw] = him
        y_ref[:, c * LANES:(c + 1) * LANES] = _dot(bu_ref[c].astype(BF16), wc_ref[c])
    y = y_ref[...] + d_ref[...] * u.astype(F32)
    z = _dot(_gelu_tanh(y).astype(BF16), wglu_ref[...].astype(BF16))
    out = (z[:, :SSM_WIDTH] * jax.nn.sigmoid(z[:, SSM_WIDTH:])).astype(BF16)
    o_ref[...] = _dot(bm_ref[...], out).astype(BF16).reshape(bsz, n_steps, SSM_WIDTH)


def _s5(u, lre, lim, wb, wc, d, wglu_stack, layer):
    bsz, seq, _ = u.shape
    steps = S5_STEPS
    tile = steps * bsz
    state_w = lre.shape[1]
    to_tm, to_bm = _time_major_perms(bsz, steps)
    blk = pl.BlockSpec((bsz, steps, SSM_WIDTH), lambda i: (0, i, 0))
    return pl.pallas_call(
        _s5_kernel,
        grid=(seq // steps,),
        in_specs=[blk, _resident(to_tm.shape), _resident(to_bm.shape),
                  _resident(lre.shape), _resident(lim.shape), _resident(wb.shape), _resident(wc.shape),
                  _resident(d.shape), _resident_layer(wglu_stack, layer)],
        out_specs=blk,
        out_shape=jax.ShapeDtypeStruct(u.shape, BF16),
        scratch_shapes=[pltpu.VMEM((wb.shape[0], tile, 2 * S5_LANE_CHUNK), F32),
                        pltpu.VMEM((2, SUBLANES, state_w), F32),
                        pltpu.VMEM((tile, SSM_WIDTH), F32)],
        compiler_params=_cparams(1),
        name="s5_mixer",
    )(u, to_tm, to_bm, lre, lim, wb, wc, d, wglu_stack)


def _pool_kernel(u_ref, tm_ref, bm_ref, w_ref, sc_ref, o_ref, ext_ref, y_ref):
    bsz, n_steps, _ = u_ref.shape
    n_rows = bsz * n_steps
    halo = POOL_HALO * SUBLANES
    i = pl.program_id(0)

    @pl.when(i == 0)
    def _():
        ext_ref[0:halo, :] = jnp.zeros((halo, POOL_WIDTH), F32)

    ext_ref[halo:halo + n_rows, :] = _dot(tm_ref[...], u_ref[...].reshape(n_rows, POOL_WIDTH))
    row = lax.broadcasted_iota(jnp.int32, (n_rows, POOL_GROUP), 0)
    t = i * (n_rows // SUBLANES) + jnp.right_shift(row, 3)
    for gi, w in enumerate(POOL_WINDOWS):
        c0 = gi * POOL_GROUP
        cur = ext_ref[halo:halo + n_rows, c0:c0 + POOL_GROUP]
        acc = cur
        for k in range(1, w):
            acc = acc + ext_ref[halo - k * SUBLANES:halo - k * SUBLANES + n_rows, c0:c0 + POOL_GROUP]
        cnt = jnp.minimum(t + 1, w).astype(F32)
        pooled = acc / cnt - cur
        mixed = _dot(pooled.astype(BF16), w_ref[gi]) * sc_ref[:, c0:c0 + POOL_GROUP]
        y_ref[:, c0:c0 + POOL_GROUP] = mixed.astype(BF16)
    ext_ref[0:halo, :] = ext_ref[n_rows:n_rows + halo, :]
    o_ref[...] = _dot(bm_ref[...], y_ref[...]).astype(BF16).reshape(bsz, n_steps, POOL_WIDTH)


def _pool(u, w, sc):
    bsz, seq, _ = u.shape
    steps = POOL_STEPS
    tile = steps * bsz
    to_tm, to_bm = _time_major_perms(bsz, steps)
    blk = pl.BlockSpec((bsz, steps, POOL_WIDTH), lambda i: (0, i, 0))
    return pl.pallas_call(
        _pool_kernel,
        grid=(seq // steps,),
        in_specs=[blk, _resident(to_tm.shape), _resident(to_bm.shape), _resident(w.shape), _resident(sc.shape)],
        out_specs=blk,
        out_shape=jax.ShapeDtypeStruct(u.shape, BF16),
        scratch_shapes=[pltpu.VMEM((tile + POOL_HALO * SUBLANES, POOL_WIDTH), F32),
                        pltpu.VMEM((tile, POOL_WIDTH), BF16)],
        compiler_params=_cparams(1),
        name="pool_mixer",
    )(u, to_tm, to_bm, w, sc)


def _compress_kernel(z_ref, w1_ref, pos_ref, b1_ref, w2_ref, o_ref):
    half = CMP_STRIDE * HEAD_DIM
    _, n_bh, n_chunks, width = z_ref.shape
    z = z_ref[0].reshape(n_bh * n_chunks, width)
    w_top = w1_ref[0, :half, :].astype(BF16)
    w_bot = w1_ref[0, half:, :].astype(BF16)
    bottom_next = pltpu.roll(_dot(z, w_bot), n_bh * n_chunks - 1, 0)
    pos = pos_ref[0].astype(BF16)
    cst = _dot(pos[:, :half], w_top) + _dot(pos[:, half:], w_bot)
    hid = _gelu_tanh(_dot(z, w_top) + bottom_next + cst[0:1, :] + b1_ref[0])
    out = _dot(hid.astype(BF16), w2_ref[0])
    lane = lax.broadcasted_iota(jnp.int32, out.shape, 1)
    is_value = pl.program_id(0) == 1
    out = jnp.where((lane >= HEAD_DIM) & is_value, 1.0, out).astype(BF16)
    o_ref[0] = out.reshape(n_bh, n_chunks, LANES)


def _compress(z, w1_stack, layer, pos, b1, w2dup):
    _, n_bh, n_chunks, width = z.shape
    return pl.pallas_call(
        _compress_kernel,
        grid=(2,),
        in_specs=[pl.BlockSpec((1, n_bh, n_chunks, width), lambda j: (j, 0, 0, 0)),
                  pl.BlockSpec((None, 1) + w1_stack.shape[2:], lambda j: (layer, j, 0, 0)),
                  pl.BlockSpec((1,) + pos.shape[1:], lambda j: (j, 0, 0)),
                  pl.BlockSpec((1,) + b1.shape[1:], lambda j: (j, 0, 0)),
                  pl.BlockSpec((1,) + w2dup.shape[1:], lambda j: (j, 0, 0))],
        out_specs=pl.BlockSpec((1, n_bh, n_chunks, LANES), lambda j: (j, 0, 0, 0)),
        out_shape=jax.ShapeDtypeStruct((2, n_bh, n_chunks, LANES), BF16),
        compiler_params=_cparams(1),
        name="compress_mlp",
    )(z, w1_stack, pos, b1, w2dup)


def _nsa_kernel(q_ref, kc_ref, vc_ref, ks_ref, vs_ref, kw_ref, vw_ref, g_ref,
                win_bias_ref, overlap_ref, expand_ref, pick_ref, o_ref, *scratch, seq):
    tile_idx = pl.program_id(1)
    heads = []
    for h in range(N_KV_HEADS):
        pair_cols = slice(h * 2 * LANES, (h + 1) * 2 * LANES)
        cols = slice(h * LANES, (h + 1) * LANES)
        sel_state, pw_ref, pc_ref = scratch[3 * h:3 * h + 3]
        heads.append(_nsa_head(q_ref.at[:, pair_cols], kc_ref.at[0, h], vc_ref.at[0, h],
                               ks_ref.at[0, :, cols], vs_ref.at[0, :, cols], kw_ref.at[0, :, cols],
                               vw_ref.at[0, :, cols], g_ref.at[:, cols],
                               win_bias_ref, overlap_ref, expand_ref, pick_ref, o_ref.at[:, pair_cols],
                               sel_state, pw_ref, pc_ref, seq=seq, tile_idx=tile_idx))
    streams = [next(head) for head in heads]
    for scores, _, buf_a, _, _ in streams:
        scores(0, buf_a)

    for scores, consume, buf_a, buf_b, n_pairs in streams:
        def pair_step(j, carry, scores=scores, consume=consume, buf_a=buf_a, buf_b=buf_b):
            scores(2 * j + 1, buf_b)
            consume(2 * j, buf_a)
            scores(2 * j + 2, buf_a)
            consume(2 * j + 1, buf_b)
            return carry

        lax.fori_loop(0, n_pairs, pair_step, 0)
    for head in heads:
        next(head, None)


def _nsa_head(q_ref, kc_ref, vc_ref, ks_ref, vs_ref, kw_ref, vw_ref, g_ref,
              win_bias_ref, overlap_ref, expand_ref, pick_ref, o_ref,
              sel_state, pw_ref, pc_ref, *, seq, tile_idx):
    tq = q_ref.shape[0]
    tk = sel_state[0].shape[1]
    n_sel = seq // SEL_BLOCK
    n_top = min(SEL_TOP, n_sel)
    n_cmp = (seq - CMP_BLOCK) // CMP_STRIDE + 1
    t0 = tile_idx * tq
    head_rows = [slice(g * tq, (g + 1) * tq) for g in range(GQA_GROUP)]

    lane = lax.broadcasted_iota(jnp.int32, (tq, LANES), 1)
    left = lane < HEAD_DIM
    zero = jnp.zeros((tq, LANES), BF16)
    pairs = (q_ref[:, 0:LANES], q_ref[:, LANES:2 * LANES])
    q4 = jnp.concatenate([jnp.where(left, pairs[0], zero), jnp.where(left, zero, pairs[0]),
                          jnp.where(left, pairs[1], zero), jnp.where(left, zero, pairs[1])], axis=0)
    tq_col = t0 + lax.broadcasted_iota(jnp.int32, (tq, 1), 0)

    def online_branch(state, k_ref, v_ref, tile_start, tile_bias):
        buf_a, buf_b, p_ref, m_ref, a_ref, acc_ref = state
        m_ref[...] = jnp.full(m_ref.shape, NEG, F32)
        acc_ref[...] = jnp.zeros(acc_ref.shape, F32)

        def scores(kt, dst_ref):
            bias = tile_bias(kt)
            s = _dot_nt(q4, k_ref[pl.ds(tile_start(kt), tk), :])
            for rows in head_rows:
                dst_ref[rows, :] = s[rows] + bias

        def consume(kt, src_ref):
            for rows in head_rows:
                s = src_ref[rows, :]
                m_old = m_ref[rows, :]
                m_new = jnp.maximum(m_old, jnp.max(s, axis=-1, keepdims=True))
                a_ref[rows, :] = jnp.exp2(m_old - m_new)
                m_ref[rows, :] = m_new
                p_ref[rows, :] = jnp.exp2(s - jnp.concatenate([m_new] * (tk // LANES), axis=1)).astype(BF16)
            acc_ref[...] = a_ref[...] * acc_ref[...] + _dot(p_ref[...], v_ref[pl.ds(tile_start(kt), tk), :])

        return buf_a, buf_b, scores, consume, acc_ref


    wk = pw_ref.shape[1]
    w0 = pl.multiple_of(jnp.maximum(t0 - WINDOW, 0), tq)
    bias_w = win_bias_ref[jnp.minimum(tile_idx, win_bias_ref.shape[0] - 1)]
    s_w = _dot_nt(q4, kw_ref[pl.ds(w0, wk), :])
    for rows in head_rows:
        s = s_w[rows] + bias_w
        pw_ref[rows, :] = jnp.exp2(s - jnp.max(s, axis=-1, keepdims=True)).astype(BF16)
    acc_w = _dot(pw_ref[...], vw_ref[pl.ds(w0, wk), :])

    n_idx = lax.broadcasted_iota(jnp.int32, (1, kc_ref.shape[0]), 1)
    cmp_valid = (n_idx * CMP_STRIDE + (CMP_BLOCK - 1) <= tq_col) & (n_idx < n_cmp)
    s_c = _dot_nt(q4, kc_ref[...])
    p_sum = None
    for rows in head_rows:
        s = jnp.where(cmp_valid, s_c[rows], NEG)
        e = jnp.where(cmp_valid, jnp.exp2(s - jnp.max(s, axis=-1, keepdims=True)), 0.0)
        l = jnp.sum(e, axis=-1, keepdims=True)
        p = e / jnp.where(l > 0.0, l, 1.0)
        p_sum = p if p_sum is None else p_sum + p
        pc_ref[rows, :] = p.astype(BF16)
    o_cmp = _dot(pc_ref[...], vc_ref[...])

    imp = lax.dot_general(overlap_ref[...], p_sum, (((1,), (1,)), ((), ())), precision=lax.Precision.HIGHEST,
                          preferred_element_type=F32)
    jb = lax.broadcasted_iota(jnp.int32, (n_sel, tq), 0)
    tt = t0 + lax.broadcasted_iota(jnp.int32, (n_sel, tq), 1)
    cur = jnp.right_shift(tt, 6)
    forced = (jb == 0) | (jb == cur) | (jb == cur - 1)
    causal = jb * SEL_BLOCK <= tt
    score = jnp.where(forced, 1e30, jnp.where(causal, imp, NEG))
    rank = jnp.zeros((n_sel, tq), F32)
    for a in range(n_sel):
        sa = score[a:a + 1, :]
        ahead = (sa > score) | ((sa == score) & (jb > a))
        rank = rank + jnp.where(ahead, 1.0, 0.0)
    chosen = (rank < float(n_top)) & causal
    sel_bias = jnp.where(chosen, 0.0, NEG).T.astype(BF16)

    def sel_bias_tile(kt):
        k0 = kt * tk
        expand = expand_ref[:, pl.ds(tile_start(kt), tk)]
        bias = _dot(sel_bias, expand)
        return jnp.where(k0 + lax.broadcasted_iota(jnp.int32, (1, tk), 1) <= tq_col, bias, NEG)

    def tile_start(kt):
        return pl.multiple_of(jnp.minimum(kt * tk, seq - tk), tk)

    sa, sb, sel_scores, sel_consume, sel_acc = online_branch(
        sel_state, ks_ref, vs_ref, tile_start, sel_bias_tile)

    n_kt = (t0 + tq + tk - 1) // tk
    yield sel_scores, sel_consume, sa, sb, (n_kt + 1) // 2
    acc_s = sel_acc[...]

    g = g_ref[...]
    g_hi = g.astype(BF16)
    g_lo = (g - g_hi.astype(F32)).astype(BF16)
    gates = _dot(jnp.concatenate([g_hi, g_lo], axis=1), pick_ref[...])

    def pair_tile(acc, pair, normalise):
        a, b = acc[head_rows[2 * pair]], acc[head_rows[2 * pair + 1]]
        num = jnp.where(left, a, pltpu.roll(b, HEAD_DIM, 1))
        return num / jnp.where(left, pltpu.roll(a, HEAD_DIM, 1), b) if normalise else num

    for pair in range(GQA_GROUP // 2):
        out = None
        for br, (acc, normalise) in enumerate(((o_cmp, False), (acc_s, True), (acc_w, True))):
            tile = br * (GQA_GROUP // 2) + pair
            term = gates[:, tile * LANES:(tile + 1) * LANES] * pair_tile(acc, pair, normalise)
            out = term if out is None else out + term
        o_ref[:, pair * LANES:(pair + 1) * LANES] = out.astype(BF16)


def _nsa(q, cmp_kv, ksel, vsel, kwin, vwin, ng, bsz, seq):
    tq = Q_TILE
    n_q = seq // tq
    n_chunks = cmp_kv.shape[2]
    rows = GQA_GROUP * tq
    assert seq % K_TILE == 0
    qrow = lambda b, i: (b * n_q + i, 0)
    kv_spec = pl.BlockSpec((1, seq, N_KV_HEADS * LANES), lambda b, i: (b, 0, 0))
    cmp_kv = cmp_kv.reshape(2, bsz, N_KV_HEADS, n_chunks, LANES)

    def branch_state():
        return (pltpu.VMEM((rows, K_TILE), F32), pltpu.VMEM((rows, K_TILE), F32), pltpu.VMEM((rows, K_TILE), BF16),
                pltpu.VMEM((rows, LANES), F32), pltpu.VMEM((rows, LANES), F32), pltpu.VMEM((rows, LANES), F32))

    wk = WINDOW + tq
    case = jnp.arange(WINDOW // tq + 1)[:, None, None]
    dist = jnp.minimum(case * tq, WINDOW) + jnp.arange(tq)[None, :, None] - jnp.arange(wk)[None, None, :]
    win_bias = jnp.where((dist >= 0) & (dist < WINDOW), 0.0, NEG).astype(F32)
    n_sel = seq // SEL_BLOCK
    n_cmp = (seq - CMP_BLOCK) // CMP_STRIDE + 1
    cn = jnp.arange(n_chunks)[None, :] * CMP_STRIDE
    sj = jnp.arange(n_sel)[:, None] * SEL_BLOCK
    overlap = jnp.clip(jnp.minimum(cn + CMP_BLOCK, sj + SEL_BLOCK) - jnp.maximum(cn, sj), 0, None).astype(F32) / CMP_BLOCK
    overlap = jnp.where(jnp.arange(n_chunks)[None, :] < n_cmp, overlap, 0.0)
    expand = (jnp.arange(n_sel)[:, None] == jnp.arange(seq)[None, :] // SEL_BLOCK).astype(BF16)
    n_tiles = 3 * (GQA_GROUP // 2)
    src = jnp.arange(2 * LANES)[:, None] % LANES
    dst = jnp.arange(n_tiles * LANES)[None, :]
    pick = (src == 2 * (dst // LANES) + (dst // HEAD_DIM) % 2).astype(BF16)
    return pl.pallas_call(
        functools.partial(_nsa_kernel, seq=seq),
        grid=(bsz, n_q),
        in_specs=[pl.BlockSpec((tq, ATTN_WIDTH), qrow),
                  pl.BlockSpec((None, 1, N_KV_HEADS, n_chunks, LANES), lambda b, i: (0, b, 0, 0, 0)),
                  pl.BlockSpec((None, 1, N_KV_HEADS, n_chunks, LANES), lambda b, i: (1, b, 0, 0, 0)),
                  kv_spec, kv_spec, kv_spec, kv_spec,
                  pl.BlockSpec((tq, N_KV_HEADS * LANES), qrow),
                  _resident(win_bias.shape), _resident(overlap.shape), _resident(expand.shape), _resident(pick.shape)],
        out_specs=pl.BlockSpec((tq, ATTN_WIDTH), qrow),
        out_shape=jax.ShapeDtypeStruct((bsz * seq, ATTN_WIDTH), BF16),
        scratch_shapes=[shape for _ in range(N_KV_HEADS) for shape in
                        (branch_state(), pltpu.VMEM((rows, WINDOW + tq), BF16), pltpu.VMEM((rows, n_chunks), BF16))],
        compiler_params=_cparams(2),
        name="nsa_attention",
    )(q, cmp_kv, cmp_kv, ksel, vsel, kwin, vwin, ng, win_bias, overlap, expand, pick)


def _merge_kernel(x_ref, ys_ref, yp_ref, yn_ref, bg_ref, wb_ref, wo_ref, g_ref, b_ref, o_ref, *, alpha):
    merged = None
    for k, y_ref in enumerate((ys_ref, yp_ref, yn_ref)):
        term = bg_ref[:, k * D_MODEL:(k + 1) * D_MODEL].astype(F32) * _dot(y_ref[...], wb_ref[k].astype(BF16))
        merged = term if merged is None else merged + term
    r = alpha * x_ref[...] + _dot(merged.astype(BF16), wo_ref[...].astype(BF16))
    o_ref[...] = _layer_norm(r, g_ref[...], b_ref[...])


def _merge(xr, ys, yp, yn, bg, wb_stack, wo_stack, layer, g, b, alpha):
    rows = xr.shape[0]
    tile = PROJ_TILE
    row = lambda i: (i, 0)
    return pl.pallas_call(
        functools.partial(_merge_kernel, alpha=alpha),
        grid=(rows // tile,),
        in_specs=[pl.BlockSpec((tile, D_MODEL), row),
                  pl.BlockSpec((tile, SSM_WIDTH), row), pl.BlockSpec((tile, POOL_WIDTH), row),
                  pl.BlockSpec((tile, ATTN_WIDTH), row), pl.BlockSpec((tile, N_BRANCH * D_MODEL), row),
                  _resident_layer(wb_stack, layer), _resident_layer(wo_stack, layer),
                  _resident(g.shape), _resident(b.shape)],
        out_specs=pl.BlockSpec((tile, D_MODEL), row),
        out_shape=jax.ShapeDtypeStruct((rows, D_MODEL), F32),
        compiler_params=_cparams(1),
        name="branch_merge",
    )(xr, ys, yp, yn, bg, wb_stack, wo_stack, g, b)


def _ffn_kernel(x_ref, wi_ref, wo_ref, g_ref, b_ref, o_ref, *, alpha):
    x = x_ref[...]
    xb = x.astype(BF16)
    acc = None
    for c in range(FF_HIDDEN // FF_CHUNK):
        c0 = c * FF_CHUNK
        hg = _dot(xb, wi_ref[:, c0:c0 + FF_CHUNK].astype(BF16))
        hu = _dot(xb, wi_ref[:, FF_HIDDEN + c0:FF_HIDDEN + c0 + FF_CHUNK].astype(BF16))
        act = (hg * jax.nn.sigmoid(hg) * hu).astype(BF16)
        part = _dot(act, wo_ref[c0:c0 + FF_CHUNK, :].astype(BF16))
        acc = part if acc is None else acc + part
    o_ref[...] = _layer_norm(alpha * x + acc, g_ref[...], b_ref[...])


def _ffn(xr, wi_stack, wo_stack, layer, g, b, alpha):
    rows = xr.shape[0]
    tile = ROW_TILE
    row = lambda i: (i, 0)
    return pl.pallas_call(
        functools.partial(_ffn_kernel, alpha=alpha),
        grid=(rows // tile,),
        in_specs=[pl.BlockSpec((tile, D_MODEL), row), _resident_layer(wi_stack, layer),
                  _resident_layer(wo_stack, layer), _resident(g.shape), _resident(b.shape)],
        out_specs=pl.BlockSpec((tile, D_MODEL), row),
        out_shape=jax.ShapeDtypeStruct((rows, D_MODEL), F32),
        compiler_params=_cparams(1),
        name="swiglu_ffn",
    )(xr, wi_stack, wo_stack, g, b)


def _pack_s5(a_re, a_im, log_dt, b_re, b_im, c_re, c_im):
    depth = a_re.shape[0]
    a_re, a_im = a_re.astype(F32), a_im.astype(F32)
    dt = jnp.exp(log_dt.astype(F32))[..., None]
    mag = jnp.exp(a_re * dt)
    lbar_re, lbar_im = mag * jnp.cos(a_im * dt), mag * jnp.sin(a_im * dt)
    den = a_re * a_re + a_im * a_im
    coef_re = ((lbar_re - 1.0) * a_re + lbar_im * a_im) / den
    coef_im = (lbar_im * a_re - (lbar_re - 1.0) * a_im) / den
    b_re, b_im = b_re.astype(F32), b_im.astype(F32)
    bbar_re = coef_re[..., None] * b_re - coef_im[..., None] * b_im
    bbar_im = coef_re[..., None] * b_im + coef_im[..., None] * b_re
    gpc = S5_LANE_CHUNK // SSM_STATE
    n_chunk = SSM_GROUPS // gpc
    eye = jnp.eye(gpc, dtype=F32)

    def b_block(part):
        v = part.transpose(0, 1, 3, 2).reshape(depth, n_chunk, gpc, SSM_GROUP, SSM_STATE)
        return jnp.einsum('xy,dqxcp->dqxcyp', eye, v).reshape(depth, n_chunk, gpc * SSM_GROUP, gpc * SSM_STATE)

    def c_block(part):
        v = part.reshape(depth, n_chunk, gpc, SSM_GROUP, SSM_STATE)
        return jnp.einsum('xy,dqxcp->dqxpyc', eye, v).reshape(depth, n_chunk, gpc * SSM_STATE, gpc * SSM_GROUP)

    wb = jnp.concatenate([b_block(bbar_re), b_block(bbar_im)], axis=-1).astype(BF16)
    wc = jnp.concatenate([c_block(c_re.astype(F32)), c_block(-c_im.astype(F32))], axis=-2).astype(BF16)
    state_w = SSM_GROUPS * SSM_STATE
    lre = jnp.broadcast_to(lbar_re.reshape(depth, 1, state_w), (depth, SUBLANES, state_w))
    lim = jnp.broadcast_to(lbar_im.reshape(depth, 1, state_w), (depth, SUBLANES, state_w))
    return lre, lim, wb, wc


def kernel(x, positions, w_in, ssm_a_re, ssm_a_im, ssm_log_dt, ssm_b_re, ssm_b_im, ssm_c_re, ssm_c_im,
           ssm_d, ssm_w_glu, pool_w, pool_scale, cmp_pos, cmp_w1, cmp_b1, cmp_w2,
           w_branch, w_out, ln_g, ln_b, ffn_w_in, ffn_w_out):
    bsz, seq, _ = x.shape
    depth = w_in.shape[0]
    rows = bsz * seq
    assert bsz == SUBLANES and seq % PROJ_TILE == 0 and seq >= WINDOW + Q_TILE
    alpha = (2 * depth) ** 0.25

    assert w_in.shape[-1] == IN_RAW
    lre, lim, s5_wb, s5_wc = _pack_s5(ssm_a_re, ssm_a_im, ssm_log_dt, ssm_b_re, ssm_b_im, ssm_c_re, ssm_c_im)
    s5_d = ssm_d.astype(F32).reshape(depth, 1, SSM_WIDTH)
    pool_wb = pool_w.astype(BF16)
    pool_sc = pool_scale.astype(F32).reshape(depth, 1, POOL_WIDTH)
    half = CMP_STRIDE * HEAD_DIM
    cmp_posr = jnp.broadcast_to(cmp_pos.astype(F32).reshape(depth, 2, 1, CMP_BLOCK * HEAD_DIM),
                                (depth, 2, SUBLANES, CMP_BLOCK * HEAD_DIM))
    cmp_b1r = cmp_b1.astype(F32).reshape(depth, 2, 1, CMP_HIDDEN)
    cmp_w2dup = jnp.concatenate([cmp_w2, cmp_w2 * jnp.array([1.0, 0.0], cmp_w2.dtype).reshape(1, 2, 1, 1)],
                                axis=-1).astype(BF16)
    lng = ln_g.astype(F32).reshape(depth, 2, 1, D_MODEL)
    lnb = ln_b.astype(F32).reshape(depth, 2, 1, D_MODEL)

    def pack_w_tail(w):
        w_gate = w[:, COL_NG:COL_NG + N_GATE].reshape(D_MODEL, 3, N_KV_HEADS, GQA_GROUP)
        w_gate = w_gate.transpose(0, 2, 1, 3).reshape(D_MODEL, N_KV_HEADS, 3 * GQA_GROUP)
        w_gate = jnp.pad(w_gate, ((0, 0), (0, 0), (0, LANES - 3 * GQA_GROUP))).reshape(D_MODEL, N_KV_HEADS * LANES)
        return jnp.concatenate([w_gate, w[:, COL_NG + N_GATE:]], axis=-1).astype(BF16)

    cos, sin = _rope_tables(positions)
    xr = x.astype(F32).reshape(rows, D_MODEL)
    w_in_head = w_in[:, :, :COL_NG]
    for l in range(depth):
        (u_ssm, u_pool, q, cmp_in, ksel, vsel, kwin, vwin, ng, bg) = _inproj(
            xr, w_in_head, l, pack_w_tail(w_in[l]), cos, sin, bsz, seq)
        y_ssm = _s5(u_ssm.reshape(bsz, seq, SSM_WIDTH), lre[l], lim[l], s5_wb[l], s5_wc[l], s5_d[l], ssm_w_glu, l)
        y_pool = _pool(u_pool.reshape(bsz, seq, POOL_WIDTH), pool_wb[l], pool_sc[l])
        cmp_z = cmp_in.reshape(2, bsz * N_KV_HEADS, seq // CMP_STRIDE, CMP_STRIDE * HEAD_DIM)
        cmp_kv = _compress(cmp_z, cmp_w1, l, cmp_posr[l], cmp_b1r[l], cmp_w2dup[l])
        y_nsa = _nsa(q, cmp_kv, ksel, vsel, kwin, vwin, ng, bsz, seq)
        x1 = _merge(xr, y_ssm.reshape(rows, SSM_WIDTH), y_pool.reshape(rows, POOL_WIDTH), y_nsa, bg,
                    w_branch, w_out, l, lng[l, 0], lnb[l, 0], alpha)
        xr = _ffn(x1, ffn_w_in, ffn_w_out, l, lng[l, 1], lnb[l, 1], alpha)
    return xr.reshape(bsz, seq, D_MODEL).astype(x.dtype)
```

```python
import functools
import math

import jax
import jax.numpy as jnp
from jax import lax
from jax.experimental import pallas as pl
from jax.experimental.pallas import tpu as pltpu

F32 = jnp.float32
BF16 = jnp.bfloat16

D_MODEL = 1024
SSM_WIDTH = 512
SSM_GROUP = 16
SSM_GROUPS = 32
SSM_STATE = 64
POOL_WIDTH = 512
POOL_WINDOWS = (2, 4, 8, 16)
POOL_GROUP = 128
HEAD_DIM = 64
N_HEADS = 8
N_KV_HEADS = 2
GQA_GROUP = 4
ATTN_WIDTH = 512
KV_WIDTH = 128
N_BRANCH = 3
CMP_BLOCK = 32
CMP_STRIDE = 16
CMP_HIDDEN = 256
SEL_BLOCK = 64
SEL_TOP = 16
WINDOW = 512
ROPE_THETA = 10000.0
FF_HIDDEN = 2816
LN_EPS = 1e-5
NEG = -1e30
N_GATE = 3 * N_HEADS
IN_RAW = 3 * 512 + 6 * KV_WIDTH + N_GATE + N_BRANCH * D_MODEL

LANES = 128
SUBLANES = 8
VMEM_LIMIT_BYTES = 58 * 1024 * 1024

COL_SSM = 0
COL_POOL = 512
COL_Q = 1024
COL_KV = 1536
COL_NG = 2304
COL_BG = COL_NG + N_KV_HEADS * LANES
IN_PACKED = COL_BG + N_BRANCH * D_MODEL
LOG2E = 1.4426950408889634

ROW_TILE = 512
PROJ_TILE = 1024
CHUNK_PERM_ROWS = 512
S5_STEPS = 64
S5_LANE_CHUNK = 512
POOL_STEPS = 64
POOL_HALO = 16
Q_TILE = 256
K_TILE = 256
FF_CHUNK = 256


def _cparams(n_axes):
    return pltpu.CompilerParams(dimension_semantics=("arbitrary",) * n_axes,
                                vmem_limit_bytes=VMEM_LIMIT_BYTES)


def _resident(shape):
    nd = len(shape)
    return pl.BlockSpec(shape, lambda *_: (0,) * nd, pipeline_mode=pl.Buffered(1))


def _resident_layer(stacked, layer):
    nd = stacked.ndim
    return pl.BlockSpec((None,) + stacked.shape[1:], lambda *_: (layer,) + (0,) * (nd - 1),
                        pipeline_mode=pl.Buffered(1))


def _gelu_tanh(x):
    return x * (0.5 * (1.0 + jnp.tanh(math.sqrt(2.0 / math.pi) * (x + 0.044715 * (x * x * x)))))


def _layer_norm(r, g, b):
    mu = jnp.mean(r, axis=-1, keepdims=True)
    c = r - mu
    var = jnp.mean(c * c, axis=-1, keepdims=True)
    return c * lax.rsqrt(var + LN_EPS) * g + b


def _dot(a, b):
    return jnp.dot(a, b, preferred_element_type=F32)


def _dot_nt(a, b):
    return lax.dot_general(a, b, (((1,), (1,)), ((), ())), preferred_element_type=F32)


def _rope_table_kernel(pos_ref, inv_ref, cos_ref, sin_ref):
    ang = pos_ref[...] * inv_ref[...]
    lane = lax.broadcasted_iota(jnp.int32, ang.shape, 1)
    first_half = jnp.bitwise_and(lane, HEAD_DIM - 1) < HEAD_DIM // 2
    cos_ref[...] = jnp.cos(ang)
    sin_ref[...] = jnp.where(first_half, -jnp.sin(ang), jnp.sin(ang))


def _rope_tables(positions):
    rows = positions.size
    inv = ROPE_THETA ** (-jnp.arange(0, HEAD_DIM, 2, dtype=F32) / HEAD_DIM)
    inv = jnp.tile(inv, LANES // (HEAD_DIM // 2)).reshape(1, LANES)
    pos = positions.astype(F32).reshape(rows, 1)
    tile = ROW_TILE
    return pl.pallas_call(
        _rope_table_kernel,
        grid=(rows // tile,),
        in_specs=[pl.BlockSpec((tile, 1), lambda i: (i, 0)),
                  pl.BlockSpec((1, LANES), lambda i: (0, 0))],
        out_specs=[pl.BlockSpec((tile, LANES), lambda i: (i, 0))] * 2,
        out_shape=[jax.ShapeDtypeStruct((rows, LANES), F32)] * 2,
        compiler_params=_cparams(1),
        name="rope_tables",
    )(pos, inv)


def _inproj_kernel(x_ref, w_head_ref, w_tail_ref, cos_ref, sin_ref, chunk_perm_ref,
                   ussm_ref, upool_ref, q_ref, cmp_ref, ksel_ref, vsel_ref, kwin_ref, vwin_ref,
                   ng_ref, bg_ref):
    xb = x_ref[...].astype(BF16)
    cos = cos_ref[...]
    sin = sin_ref[...]
    lane = lax.broadcasted_iota(jnp.int32, cos.shape, 1)
    first_half = jnp.bitwise_and(lane, HEAD_DIM - 1) < HEAD_DIM // 2
    left = lane < HEAD_DIM

    def proj(c0, width):
        if c0 < COL_NG:
            return _dot(xb, w_head_ref[:, c0:c0 + width].astype(BF16))
        return _dot(xb, w_tail_ref[:, c0 - COL_NG:c0 - COL_NG + width])

    def rope(t):
        swapped = jnp.where(first_half, pltpu.roll(t, LANES - HEAD_DIM // 2, 1),
                            pltpu.roll(t, HEAD_DIM // 2, 1))
        return t * cos + swapped * sin

    def dup(t):
        r = pltpu.roll(t, HEAD_DIM, 1)
        return jnp.where(left, t, r), jnp.where(left, r, t)

    def with_ones(t):
        return jnp.where(left, t, 1.0), jnp.where(left, pltpu.roll(t, HEAD_DIM, 1), 1.0)

    ussm_ref[...] = proj(COL_SSM, SSM_WIDTH).astype(BF16)
    upool_ref[...] = proj(COL_POOL, POOL_WIDTH).astype(BF16)
    scale = HEAD_DIM ** -0.5 * LOG2E
    for j in range(ATTN_WIDTH // LANES):
        t = rope(proj(COL_Q + j * LANES, LANES))
        q_ref[:, j * LANES:(j + 1) * LANES] = (t * scale).astype(BF16)

    kv = proj(COL_KV, 6 * KV_WIDTH)
    perm_rows = chunk_perm_ref.shape[0]
    n_chunks = perm_rows // CMP_STRIDE
    left_c = lax.broadcasted_iota(jnp.int32, (n_chunks, LANES), 1) < HEAD_DIM
    left_p = lax.broadcasted_iota(jnp.int32, (perm_rows, LANES), 1) < HEAD_DIM
    for j, roped in enumerate((True, False)):
        t = kv[:, j * LANES:(j + 1) * LANES]
        t = (rope(t) if roped else t).astype(BF16)
        for part in range(x_ref.shape[0] // perm_rows):
            by_pos = _dot(chunk_perm_ref[...], t[part * perm_rows:(part + 1) * perm_rows])
            rolled = pltpu.roll(by_pos, HEAD_DIM, 1)
            for h, dup_h in enumerate((jnp.where(left_p, by_pos, rolled), jnp.where(left_p, rolled, by_pos))):
                for i in range(CMP_STRIDE // 2):
                    even = dup_h[(2 * i) * n_chunks:(2 * i + 1) * n_chunks]
                    odd = dup_h[(2 * i + 1) * n_chunks:(2 * i + 2) * n_chunks]
                    cmp_ref[j, 0, h, part * n_chunks:(part + 1) * n_chunks, i * LANES:(i + 1) * LANES] = (
                        jnp.where(left_c, even, odd).astype(BF16))
    for j, (ref, is_key) in enumerate(((ksel_ref, True), (vsel_ref, False), (kwin_ref, True), (vwin_ref, False))):
        t = kv[:, (2 + j) * LANES:(3 + j) * LANES]
        a, b = dup(rope(t)) if is_key else with_ones(t)
        ref[0, :, 0:LANES] = a.astype(BF16)
        ref[0, :, LANES:2 * LANES] = b.astype(BF16)

    ng_ref[...] = jax.nn.sigmoid(proj(COL_NG, N_KV_HEADS * LANES))
    for k in range(N_BRANCH):
        bg_ref[:, k * D_MODEL:(k + 1) * D_MODEL] = jax.nn.sigmoid(proj(COL_BG + k * D_MODEL, D_MODEL)).astype(BF16)


def _inproj(xr, w_stack, layer, w_tail, cos, sin, bsz, seq):
    rows = bsz * seq
    head_spec = pl.BlockSpec((None, D_MODEL, COL_NG), lambda i: (layer, 0, 0), pipeline_mode=pl.Buffered(1))
    tile = PROJ_TILE
    n_s = seq // tile
    row = lambda i: (i, 0)
    dup_spec = pl.BlockSpec((1, tile, 2 * LANES), lambda i: (i // n_s, i % n_s, 0))
    chunk_w = CMP_STRIDE * HEAD_DIM
    r = jnp.arange(CHUNK_PERM_ROWS)
    n_c = CHUNK_PERM_ROWS // CMP_STRIDE
    chunk_perm = (r[None, :] == (r[:, None] % n_c) * CMP_STRIDE + r[:, None] // n_c).astype(BF16)
    out_shape = [
        jax.ShapeDtypeStruct((rows, SSM_WIDTH), BF16),
        jax.ShapeDtypeStruct((rows, POOL_WIDTH), BF16),
        jax.ShapeDtypeStruct((rows, ATTN_WIDTH), BF16),
        jax.ShapeDtypeStruct((2, bsz, N_KV_HEADS, seq // CMP_STRIDE, chunk_w), BF16),
        jax.ShapeDtypeStruct((bsz, seq, 2 * LANES), BF16),
        jax.ShapeDtypeStruct((bsz, seq, 2 * LANES), BF16),
        jax.ShapeDtypeStruct((bsz, seq, 2 * LANES), BF16),
        jax.ShapeDtypeStruct((bsz, seq, 2 * LANES), BF16),
        jax.ShapeDtypeStruct((rows, N_KV_HEADS * LANES), F32),
        jax.ShapeDtypeStruct((rows, N_BRANCH * D_MODEL), BF16),
    ]
    out_specs = [
        pl.BlockSpec((tile, SSM_WIDTH), row),
        pl.BlockSpec((tile, POOL_WIDTH), row),
        pl.BlockSpec((tile, ATTN_WIDTH), row),
        pl.BlockSpec((2, 1, N_KV_HEADS, tile // CMP_STRIDE, chunk_w), lambda i: (0, i // n_s, 0, i % n_s, 0)),
        dup_spec, dup_spec, dup_spec, dup_spec,
        pl.BlockSpec((tile, N_KV_HEADS * LANES), row),
        pl.BlockSpec((tile, N_BRANCH * D_MODEL), row),
    ]
    return pl.pallas_call(
        _inproj_kernel,
        grid=(rows // tile,),
        in_specs=[pl.BlockSpec((tile, D_MODEL), row), head_spec, _resident(w_tail.shape),
                  pl.BlockSpec((tile, LANES), row), pl.BlockSpec((tile, LANES), row), _resident(chunk_perm.shape)],
        out_specs=out_specs,
        out_shape=out_shape,
        compiler_params=_cparams(1),
        name="in_projection",
    )(xr, w_stack, w_tail, cos, sin, chunk_perm)


def _time_major_perms(bsz, steps):
    r = jnp.arange(bsz * steps)
    to_tm = (r[None, :] == (r[:, None] % bsz) * steps + r[:, None] // bsz).astype(BF16)
    return to_tm, to_tm.T


def _s5_kernel(u_ref, tm_ref, bm_ref, lre_ref, lim_ref, wb_ref, wc_ref, d_ref, wglu_ref, o_ref,
               bu_ref, st_ref, y_ref):
    bsz, n_steps, _ = u_ref.shape
    n_rows = bsz * n_steps
    n_chunk = wb_ref.shape[0]
    cw = S5_LANE_CHUNK

    @pl.when(pl.program_id(0) == 0)
    def _():
        st_ref[...] = jnp.zeros(st_ref.shape, F32)

    u = _dot(tm_ref[...], u_ref[...].reshape(n_rows, SSM_WIDTH)).astype(BF16)
    for c in range(n_chunk):
        bu_ref[c] = _dot(u[:, c * LANES:(c + 1) * LANES], wb_ref[c])

    for c in range(n_chunk):
        lre = lre_ref[:, c * cw:(c + 1) * cw]
        lim = lim_ref[:, c * cw:(c + 1) * cw]
        hre = st_ref[0, :, c * cw:(c + 1) * cw]
        him = st_ref[1, :, c * cw:(c + 1) * cw]
        for t in range(n_steps):
            rows = slice(t * SUBLANES, (t + 1) * SUBLANES)
            hre, him = (lre * hre - lim * him + bu_ref[c, rows, 0:cw],
                        lre * him + lim * hre + bu_ref[c, rows, cw:2 * cw])
            bu_ref[c, rows, 0:cw] = hre
            bu_ref[c, rows, cw:2 * cw] = him
        st_ref[0, :, c * cw:(c + 1) * cw] = hre
        st_ref[1, :, c * cw:(c + 1) * cw] = him
        y_ref[:, c * LANES:(c + 1) * LANES] = _dot(bu_ref[c].astype(BF16), wc_ref[c])
    y = y_ref[...] + d_ref[...] * u.astype(F32)
    z = _dot(_gelu_tanh(y).astype(BF16), wglu_ref[...].astype(BF16))
    out = (z[:, :SSM_WIDTH] * jax.nn.sigmoid(z[:, SSM_WIDTH:])).astype(BF16)
    o_ref[...] = _dot(bm_ref[...], out).astype(BF16).reshape(bsz, n_steps, SSM_WIDTH)


def _s5(u, lre, lim, wb, wc, d, wglu_stack, layer):
    bsz, seq, _ = u.shape
    steps = S5_STEPS
    tile = steps * bsz
    state_w = lre.shape[1]
    to_tm, to_bm = _time_major_perms(bsz, steps)
    blk = pl.BlockSpec((bsz, steps, SSM_WIDTH), lambda i: (0, i, 0))
    return pl.pallas_call(
        _s5_kernel,
        grid=(seq // steps,),
        in_specs=[blk, _resident(to_tm.shape), _resident(to_bm.shape),
                  _resident(lre.shape), _resident(lim.shape), _resident(wb.shape), _resident(wc.shape),
                  _resident(d.shape), _resident_layer(wglu_stack, layer)],
        out_specs=blk,
        out_shape=jax.ShapeDtypeStruct(u.shape, BF16),
        scratch_shapes=[pltpu.VMEM((wb.shape[0], tile, 2 * S5_LANE_CHUNK), F32),
                        pltpu.VMEM((2, SUBLANES, state_w), F32),
                        pltpu.VMEM((tile, SSM_WIDTH), F32)],
        compiler_params=_cparams(1),
        name="s5_mixer",
    )(u, to_tm, to_bm, lre, lim, wb, wc, d, wglu_stack)


def _pool_kernel(u_ref, tm_ref, bm_ref, w_ref, sc_ref, o_ref, ext_ref, y_ref):
    bsz, n_steps, _ = u_ref.shape
    n_rows = bsz * n_steps
    halo = POOL_HALO * SUBLANES
    i = pl.program_id(0)

    @pl.when(i == 0)
    def _():
        ext_ref[0:halo, :] = jnp.zeros((halo, POOL_WIDTH), F32)

    ext_ref[halo:halo + n_rows, :] = _dot(tm_ref[...], u_ref[...].reshape(n_rows, POOL_WIDTH))
    row = lax.broadcasted_iota(jnp.int32, (n_rows, POOL_GROUP), 0)
    t = i * (n_rows // SUBLANES) + jnp.right_shift(row, 3)
    for gi, w in enumerate(POOL_WINDOWS):
        c0 = gi * POOL_GROUP
        cur = ext_ref[halo:halo + n_rows, c0:c0 + POOL_GROUP]
        acc = cur
        for k in range(1, w):
            acc = acc + ext_ref[halo - k * SUBLANES:halo - k * SUBLANES + n_rows, c0:c0 + POOL_GROUP]
        cnt = jnp.minimum(t + 1, w).astype(F32)
        pooled = acc / cnt - cur
        mixed = _dot(pooled.astype(BF16), w_ref[gi]) * sc_ref[:, c0:c0 + POOL_GROUP]
        y_ref[:, c0:c0 + POOL_GROUP] = mixed.astype(BF16)
    ext_ref[0:halo, :] = ext_ref[n_rows:n_rows + halo, :]
    o_ref[...] = _dot(bm_ref[...], y_ref[...]).astype(BF16).reshape(bsz, n_steps, POOL_WIDTH)


def _pool(u, w, sc):
    bsz, seq, _ = u.shape
    steps = POOL_STEPS
    tile = steps * bsz
    to_tm, to_bm = _time_major_perms(bsz, steps)
    blk = pl.BlockSpec((bsz, steps, POOL_WIDTH), lambda i: (0, i, 0))
    return pl.pallas_call(
        _pool_kernel,
        grid=(seq // steps,),
        in_specs=[blk, _resident(to_tm.shape), _resident(to_bm.shape), _resident(w.shape), _resident(sc.shape)],
        out_specs=blk,
        out_shape=jax.ShapeDtypeStruct(u.shape, BF16),
        scratch_shapes=[pltpu.VMEM((tile + POOL_HALO * SUBLANES, POOL_WIDTH), F32),
                        pltpu.VMEM((tile, POOL_WIDTH), BF16)],
        compiler_params=_cparams(1),
        name="pool_mixer",
    )(u, to_tm, to_bm, w, sc)


def _compress_kernel(z_ref, w1_ref, pos_ref, b1_ref, w2_ref, o_ref):
    half = CMP_STRIDE * HEAD_DIM
    _, n_bh, n_chunks, width = z_ref.shape
    z = z_ref[0].reshape(n_bh * n_chunks, width)
    w_top = w1_ref[0, :half, :].astype(BF16)
    w_bot = w1_ref[0, half:, :].astype(BF16)
    bottom_next = pltpu.roll(_dot(z, w_bot), n_bh * n_chunks - 1, 0)
    pos = pos_ref[0].astype(BF16)
    cst = _dot(pos[:, :half], w_top) + _dot(pos[:, half:], w_bot)
    hid = _gelu_tanh(_dot(z, w_top) + bottom_next + cst[0:1, :] + b1_ref[0])
    out = _dot(hid.astype(BF16), w2_ref[0])
    lane = lax.broadcasted_iota(jnp.int32, out.shape, 1)
    is_value = pl.program_id(0) == 1
    out = jnp.where((lane >= HEAD_DIM) & is_value, 1.0, out).astype(BF16)
    o_ref[0] = out.reshape(n_bh, n_chunks, LANES)


def _compress(z, w1_stack, layer, pos, b1, w2dup):
    _, n_bh, n_chunks, width = z.shape
    return pl.pallas_call(
        _compress_kernel,
        grid=(2,),
        in_specs=[pl.BlockSpec((1, n_bh, n_chunks, width), lambda j: (j, 0, 0, 0)),
                  pl.BlockSpec((None, 1) + w1_stack.shape[2:], lambda j: (layer, j, 0, 0)),
                  pl.BlockSpec((1,) + pos.shape[1:], lambda j: (j, 0, 0)),
                  pl.BlockSpec((1,) + b1.shape[1:], lambda j: (j, 0, 0)),
                  pl.BlockSpec((1,) + w2dup.shape[1:], lambda j: (j, 0, 0))],
        out_specs=pl.BlockSpec((1, n_bh, n_chunks, LANES), lambda j: (j, 0, 0, 0)),
        out_shape=jax.ShapeDtypeStruct((2, n_bh, n_chunks, LANES), BF16),
        compiler_params=_cparams(1),
        name="compress_mlp",
    )(z, w1_stack, pos, b1, w2dup)


def _nsa_kernel(q_ref, kc_ref, vc_ref, ks_ref, vs_ref, kw_ref, vw_ref, g_ref,
                win_bias_ref, overlap_ref, expand_ref, pick_ref, o_ref, *scratch, seq):
    tile_idx = pl.program_id(1)
    heads = []
    for h in range(N_KV_HEADS):
        pair_cols = slice(h * 2 * LANES, (h + 1) * 2 * LANES)
        cols = slice(h * LANES, (h + 1) * LANES)
        sel_state, pw_ref, pc_ref = scratch[3 * h:3 * h + 3]
        heads.append(_nsa_head(q_ref.at[:, pair_cols], kc_ref.at[0, h], vc_ref.at[0, h],
                               ks_ref.at[0, :, cols], vs_ref.at[0, :, cols], kw_ref.at[0, :, cols],
                               vw_ref.at[0, :, cols], g_ref.at[:, cols],
                               win_bias_ref, overlap_ref, expand_ref, pick_ref, o_ref.at[:, pair_cols],
                               sel_state, pw_ref, pc_ref, seq=seq, tile_idx=tile_idx))
    streams = [next(head) for head in heads]
    def pair_step(j, stream):
        scores, consume, buf_a, buf_b, _ = stream
        scores(2 * j + 1, buf_b)
        consume(2 * j, buf_a)
        scores(2 * j + 2, buf_a)
        consume(2 * j + 1, buf_b)

    for stream in streams:
        stream[0](0, stream[2])
    for stream in streams:
        pair_step(0, stream)

    for stream in streams:
        def body(j, carry, stream=stream):
            pair_step(j, stream)
            return carry

        lax.fori_loop(1, stream[4], body, 0)
    for head in heads:
        next(head, None)


def _nsa_head(q_ref, kc_ref, vc_ref, ks_ref, vs_ref, kw_ref, vw_ref, g_ref,
              win_bias_ref, overlap_ref, expand_ref, pick_ref, o_ref,
              sel_state, pw_ref, pc_ref, *, seq, tile_idx):
    tq = q_ref.shape[0]
    tk = sel_state[0].shape[1]
    n_sel = seq // SEL_BLOCK
    n_top = min(SEL_TOP, n_sel)
    n_cmp = (seq - CMP_BLOCK) // CMP_STRIDE + 1
    t0 = tile_idx * tq
    head_rows = [slice(g * tq, (g + 1) * tq) for g in range(GQA_GROUP)]

    lane = lax.broadcasted_iota(jnp.int32, (tq, LANES), 1)
    left = lane < HEAD_DIM
    zero = jnp.zeros((tq, LANES), BF16)
    pairs = (q_ref[:, 0:LANES], q_ref[:, LANES:2 * LANES])
    q4 = jnp.concatenate([jnp.where(left, pairs[0], zero), jnp.where(left, zero, pairs[0]),
                          jnp.where(left, pairs[1], zero), jnp.where(left, zero, pairs[1])], axis=0)
    tq_col = t0 + lax.broadcasted_iota(jnp.int32, (tq, 1), 0)

    def online_branch(state, k_ref, v_ref, tile_start, tile_bias):
        buf_a, buf_b, p_ref, m_ref, a_ref, acc_ref = state
        m_ref[...] = jnp.full(m_ref.shape, NEG, F32)
        acc_ref[...] = jnp.zeros(acc_ref.shape, F32)

        def scores(kt, dst_ref):
            bias = tile_bias(kt)
            s = _dot_nt(q4, k_ref[pl.ds(tile_start(kt), tk), :])
            for rows in head_rows:
                dst_ref[rows, :] = s[rows] + bias

        def consume(kt, src_ref):
            for rows in head_rows:
                s = src_ref[rows, :]
                m_old = m_ref[rows, :]
                m_new = jnp.maximum(m_old, jnp.max(s, axis=-1, keepdims=True))
                a_ref[rows, :] = jnp.exp2(m_old - m_new)
                m_ref[rows, :] = m_new
                p_ref[rows, :] = jnp.exp2(s - jnp.concatenate([m_new] * (tk // LANES), axis=1)).astype(BF16)
            acc_ref[...] = a_ref[...] * acc_ref[...] + _dot(p_ref[...], v_ref[pl.ds(tile_start(kt), tk), :])

        return buf_a, buf_b, scores, consume, acc_ref


    wk = pw_ref.shape[1]
    w0 = pl.multiple_of(jnp.maximum(t0 - WINDOW, 0), tq)
    bias_w = win_bias_ref[jnp.minimum(tile_idx, win_bias_ref.shape[0] - 1)]
    s_w = _dot_nt(q4, kw_ref[pl.ds(w0, wk), :])
    for rows in head_rows:
        s = s_w[rows] + bias_w
        pw_ref[rows, :] = jnp.exp2(s - jnp.max(s, axis=-1, keepdims=True)).astype(BF16)
    acc_w = _dot(pw_ref[...], vw_ref[pl.ds(w0, wk), :])

    n_idx = lax.broadcasted_iota(jnp.int32, (1, kc_ref.shape[0]), 1)
    cmp_valid = (n_idx * CMP_STRIDE + (CMP_BLOCK - 1) <= tq_col) & (n_idx < n_cmp)
    s_c = _dot_nt(q4, kc_ref[...])
    p_sum = None
    for rows in head_rows:
        s = jnp.where(cmp_valid, s_c[rows], NEG)
        e = jnp.where(cmp_valid, jnp.exp2(s - jnp.max(s, axis=-1, keepdims=True)), 0.0)
        l = jnp.sum(e, axis=-1, keepdims=True)
        p = e / jnp.where(l > 0.0, l, 1.0)
        p_sum = p if p_sum is None else p_sum + p
        pc_ref[rows, :] = p.astype(BF16)
    o_cmp = _dot(pc_ref[...], vc_ref[...])

    imp = lax.dot_general(overlap_ref[...], p_sum, (((1,), (1,)), ((), ())), precision=lax.Precision.HIGHEST,
                          preferred_element_type=F32)
    jb = lax.broadcasted_iota(jnp.int32, (n_sel, tq), 0)
    tt = t0 + lax.broadcasted_iota(jnp.int32, (n_sel, tq), 1)
    cur = jnp.right_shift(tt, 6)
    forced = (jb == 0) | (jb == cur) | (jb == cur - 1)
    causal = jb * SEL_BLOCK <= tt
    score = jnp.where(forced, 1e30, jnp.where(causal, imp, NEG))
    rank = jnp.zeros((n_sel, tq), F32)
    for a in range(n_sel):
        sa = score[a:a + 1, :]
        ahead = (sa > score) | ((sa == score) & (jb > a))
        rank = rank + jnp.where(ahead, 1.0, 0.0)
    chosen = (rank < float(n_top)) & causal
    sel_bias = jnp.where(chosen, 0.0, NEG).T.astype(BF16)

    def sel_bias_tile(kt):
        k0 = kt * tk
        expand = expand_ref[:, pl.ds(tile_start(kt), tk)]
        bias = _dot(sel_bias, expand)
        return jnp.where(k0 + lax.broadcasted_iota(jnp.int32, (1, tk), 1) <= tq_col, bias, NEG)

    def tile_start(kt):
        return pl.multiple_of(jnp.minimum(kt * tk, seq - tk), tk)

    sa, sb, sel_scores, sel_consume, sel_acc = online_branch(
        sel_state, ks_ref, vs_ref, tile_start, sel_bias_tile)

    n_kt = (t0 + tq + tk - 1) // tk
    yield sel_scores, sel_consume, sa, sb, (n_kt + 1) // 2
    acc_s = sel_acc[...]

    g = g_ref[...]
    g_hi = g.astype(BF16)
    g_lo = (g - g_hi.astype(F32)).astype(BF16)
    gates = _dot(jnp.concatenate([g_hi, g_lo], axis=1), pick_ref[...])

    def pair_tile(acc, pair, normalise):
        a, b = acc[head_rows[2 * pair]], acc[head_rows[2 * pair + 1]]
        num = jnp.where(left, a, pltpu.roll(b, HEAD_DIM, 1))
        return num / jnp.where(left, pltpu.roll(a, HEAD_DIM, 1), b) if normalise else num

    for pair in range(GQA_GROUP // 2):
        out = None
        for br, (acc, normalise) in enumerate(((o_cmp, False), (acc_s, True), (acc_w, True))):
            tile = br * (GQA_GROUP // 2) + pair
            term = gates[:, tile * LANES:(tile + 1) * LANES] * pair_tile(acc, pair, normalise)
            out = term if out is None else out + term
        o_ref[:, pair * LANES:(pair + 1) * LANES] = out.astype(BF16)


def _nsa(q, cmp_kv, ksel, vsel, kwin, vwin, ng, bsz, seq):
    tq = Q_TILE
    n_q = seq // tq
    n_chunks = cmp_kv.shape[2]
    rows = GQA_GROUP * tq
    assert seq % K_TILE == 0
    qrow = lambda b, i: (b * n_q + i, 0)
    kv_spec = pl.BlockSpec((1, seq, N_KV_HEADS * LANES), lambda b, i: (b, 0, 0))
    cmp_kv = cmp_kv.reshape(2, bsz, N_KV_HEADS, n_chunks, LANES)

    def branch_state():
        return (pltpu.VMEM((rows, K_TILE), F32), pltpu.VMEM((rows, K_TILE), F32), pltpu.VMEM((rows, K_TILE), BF16),
                pltpu.VMEM((rows, LANES), F32), pltpu.VMEM((rows, LANES), F32), pltpu.VMEM((rows, LANES), F32))

    wk = WINDOW + tq
    case = jnp.arange(WINDOW // tq + 1)[:, None, None]
    dist = jnp.minimum(case * tq, WINDOW) + jnp.arange(tq)[None, :, None] - jnp.arange(wk)[None, None, :]
    win_bias = jnp.where((dist >= 0) & (dist < WINDOW), 0.0, NEG).astype(F32)
    n_sel = seq // SEL_BLOCK
    n_cmp = (seq - CMP_BLOCK) // CMP_STRIDE + 1
    cn = jnp.arange(n_chunks)[None, :] * CMP_STRIDE
    sj = jnp.arange(n_sel)[:, None] * SEL_BLOCK
    overlap = jnp.clip(jnp.minimum(cn + CMP_BLOCK, sj + SEL_BLOCK) - jnp.maximum(cn, sj), 0, None).astype(F32) / CMP_BLOCK
    overlap = jnp.where(jnp.arange(n_chunks)[None, :] < n_cmp, overlap, 0.0)
    expand = (jnp.arange(n_sel)[:, None] == jnp.arange(seq)[None, :] // SEL_BLOCK).astype(BF16)
    n_tiles = 3 * (GQA_GROUP // 2)
    src = jnp.arange(2 * LANES)[:, None] % LANES
    dst = jnp.arange(n_tiles * LANES)[None, :]
    pick = (src == 2 * (dst // LANES) + (dst // HEAD_DIM) % 2).astype(BF16)
    return pl.pallas_call(
        functools.partial(_nsa_kernel, seq=seq),
        grid=(bsz, n_q),
        in_specs=[pl.BlockSpec((tq, ATTN_WIDTH), qrow),
                  pl.BlockSpec((None, 1, N_KV_HEADS, n_chunks, LANES), lambda b, i: (0, b, 0, 0, 0)),
                  pl.BlockSpec((None, 1, N_KV_HEADS, n_chunks, LANES), lambda b, i: (1, b, 0, 0, 0)),
                  kv_spec, kv_spec, kv_spec, kv_spec,
                  pl.BlockSpec((tq, N_KV_HEADS * LANES), qrow),
                  _resident(win_bias.shape), _resident(overlap.shape), _resident(expand.shape), _resident(pick.shape)],
        out_specs=pl.BlockSpec((tq, ATTN_WIDTH), qrow),
        out_shape=jax.ShapeDtypeStruct((bsz * seq, ATTN_WIDTH), BF16),
        scratch_shapes=[shape for _ in range(N_KV_HEADS) for shape in
                        (branch_state(), pltpu.VMEM((rows, WINDOW + tq), BF16), pltpu.VMEM((rows, n_chunks), BF16))],
        compiler_params=_cparams(2),
        name="nsa_attention",
    )(q, cmp_kv, cmp_kv, ksel, vsel, kwin, vwin, ng, win_bias, overlap, expand, pick)


def _merge_kernel(x_ref, ys_ref, yp_ref, yn_ref, bg_ref, wb_ref, wo_ref, g_ref, b_ref, o_ref, *, alpha):
    merged = None
    for k, y_ref in enumerate((ys_ref, yp_ref, yn_ref)):
        term = bg_ref[:, k * D_MODEL:(k + 1) * D_MODEL].astype(F32) * _dot(y_ref[...], wb_ref[k].astype(BF16))
        merged = term if merged is None else merged + term
    r = alpha * x_ref[...] + _dot(merged.astype(BF16), wo_ref[...].astype(BF16))
    o_ref[...] = _layer_norm(r, g_ref[...], b_ref[...])


def _merge(xr, ys, yp, yn, bg, wb_stack, wo_stack, layer, g, b, alpha):
    rows = xr.shape[0]
    tile = PROJ_TILE
    row = lambda i: (i, 0)
    return pl.pallas_call(
        functools.partial(_merge_kernel, alpha=alpha),
        grid=(rows // tile,),
        in_specs=[pl.BlockSpec((tile, D_MODEL), row),
                  pl.BlockSpec((tile, SSM_WIDTH), row), pl.BlockSpec((tile, POOL_WIDTH), row),
                  pl.BlockSpec((tile, ATTN_WIDTH), row), pl.BlockSpec((tile, N_BRANCH * D_MODEL), row),
                  _resident_layer(wb_stack, layer), _resident_layer(wo_stack, layer),
                  _resident(g.shape), _resident(b.shape)],
        out_specs=pl.BlockSpec((tile, D_MODEL), row),
        out_shape=jax.ShapeDtypeStruct((rows, D_MODEL), F32),
        compiler_params=_cparams(1),
        name="branch_merge",
    )(xr, ys, yp, yn, bg, wb_stack, wo_stack, g, b)


def _ffn_kernel(x_ref, wi_ref, wo_ref, g_ref, b_ref, o_ref, *, alpha):
    x = x_ref[...]
    xb = x.astype(BF16)
    acc = None
    for c in range(FF_HIDDEN // FF_CHUNK):
        c0 = c * FF_CHUNK
        hg = _dot(xb, wi_ref[:, c0:c0 + FF_CHUNK].astype(BF16))
        hu = _dot(xb, wi_ref[:, FF_HIDDEN + c0:FF_HIDDEN + c0 + FF_CHUNK].astype(BF16))
        act = (hg * jax.nn.sigmoid(hg) * hu).astype(BF16)
        part = _dot(act, wo_ref[c0:c0 + FF_CHUNK, :].astype(BF16))
        acc = part if acc is None else acc + part
    o_ref[...] = _layer_norm(alpha * x + acc, g_ref[...], b_ref[...])


def _ffn(xr, wi_stack, wo_stack, layer, g, b, alpha):
    rows = xr.shape[0]
    tile = ROW_TILE
    row = lambda i: (i, 0)
    return pl.pallas_call(
        functools.partial(_ffn_kernel, alpha=alpha),
        grid=(rows // tile,),
        in_specs=[pl.BlockSpec((tile, D_MODEL), row), _resident_layer(wi_stack, layer),
                  _resident_layer(wo_stack, layer), _resident(g.shape), _resident(b.shape)],
        out_specs=pl.BlockSpec((tile, D_MODEL), row),
        out_shape=jax.ShapeDtypeStruct((rows, D_MODEL), F32),
        compiler_params=_cparams(1),
        name="swiglu_ffn",
    )(xr, wi_stack, wo_stack, g, b)


def _pack_s5(a_re, a_im, log_dt, b_re, b_im, c_re, c_im):
    depth = a_re.shape[0]
    a_re, a_im = a_re.astype(F32), a_im.astype(F32)
    dt = jnp.exp(log_dt.astype(F32))[..., None]
    mag = jnp.exp(a_re * dt)
    lbar_re, lbar_im = mag * jnp.cos(a_im * dt), mag * jnp.sin(a_im * dt)
    den = a_re * a_re + a_im * a_im
    coef_re = ((lbar_re - 1.0) * a_re + lbar_im * a_im) / den
    coef_im = (lbar_im * a_re - (lbar_re - 1.0) * a_im) / den
    b_re, b_im = b_re.astype(F32), b_im.astype(F32)
    bbar_re = coef_re[..., None] * b_re - coef_im[..., None] * b_im
    bbar_im = coef_re[..., None] * b_im + coef_im[..., None] * b_re
    gpc = S5_LANE_CHUNK // SSM_STATE
    n_chunk = SSM_GROUPS // gpc
    eye = jnp.eye(gpc, dtype=F32)

    def b_block(part):
        v = part.transpose(0, 1, 3, 2).reshape(depth, n_chunk, gpc, SSM_GROUP, SSM_STATE)
        return jnp.einsum('xy,dqxcp->dqxcyp', eye, v).reshape(depth, n_chunk, gpc * SSM_GROUP, gpc * SSM_STATE)

    def c_block(part):
        v = part.reshape(depth, n_chunk, gpc, SSM_GROUP, SSM_STATE)
        return jnp.einsum('xy,dqxcp->dqxpyc', eye, v).reshape(depth, n_chunk, gpc * SSM_STATE, gpc * SSM_GROUP)

    wb = jnp.concatenate([b_block(bbar_re), b_block(bbar_im)], axis=-1).astype(BF16)
    wc = jnp.concatenate([c_block(c_re.astype(F32)), c_block(-c_im.astype(F32))], axis=-2).astype(BF16)
    state_w = SSM_GROUPS * SSM_STATE
    lre = jnp.broadcast_to(lbar_re.reshape(depth, 1, state_w), (depth, SUBLANES, state_w))
    lim = jnp.broadcast_to(lbar_im.reshape(depth, 1, state_w), (depth, SUBLANES, state_w))
    return lre, lim, wb, wc


def kernel(x, positions, w_in, ssm_a_re, ssm_a_im, ssm_log_dt, ssm_b_re, ssm_b_im, ssm_c_re, ssm_c_im,
           ssm_d, ssm_w_glu, pool_w, pool_scale, cmp_pos, cmp_w1, cmp_b1, cmp_w2,
           w_branch, w_out, ln_g, ln_b, ffn_w_in, ffn_w_out):
    bsz, seq, _ = x.shape
    depth = w_in.shape[0]
    rows = bsz * seq
    assert bsz == SUBLANES and seq % PROJ_TILE == 0 and seq >= WINDOW + Q_TILE
    alpha = (2 * depth) ** 0.25

    assert w_in.shape[-1] == IN_RAW
    lre, lim, s5_wb, s5_wc = _pack_s5(ssm_a_re, ssm_a_im, ssm_log_dt, ssm_b_re, ssm_b_im, ssm_c_re, ssm_c_im)
    s5_d = ssm_d.astype(F32).reshape(depth, 1, SSM_WIDTH)
    pool_wb = pool_w.astype(BF16)
    pool_sc = pool_scale.astype(F32).reshape(depth, 1, POOL_WIDTH)
    half = CMP_STRIDE * HEAD_DIM
    cmp_posr = jnp.broadcast_to(cmp_pos.astype(F32).reshape(depth, 2, 1, CMP_BLOCK * HEAD_DIM),
                                (depth, 2, SUBLANES, CMP_BLOCK * HEAD_DIM))
    cmp_b1r = cmp_b1.astype(F32).reshape(depth, 2, 1, CMP_HIDDEN)
    cmp_w2dup = jnp.concatenate([cmp_w2, cmp_w2 * jnp.array([1.0, 0.0], cmp_w2.dtype).reshape(1, 2, 1, 1)],
                                axis=-1).astype(BF16)
    lng = ln_g.astype(F32).reshape(depth, 2, 1, D_MODEL)
    lnb = ln_b.astype(F32).reshape(depth, 2, 1, D_MODEL)

    def pack_w_tail(w):
        w_gate = w[:, COL_NG:COL_NG + N_GATE].reshape(D_MODEL, 3, N_KV_HEADS, GQA_GROUP)
        w_gate = w_gate.transpose(0, 2, 1, 3).reshape(D_MODEL, N_KV_HEADS, 3 * GQA_GROUP)
        w_gate = jnp.pad(w_gate, ((0, 0), (0, 0), (0, LANES - 3 * GQA_GROUP))).reshape(D_MODEL, N_KV_HEADS * LANES)
        return jnp.concatenate([w_gate, w[:, COL_NG + N_GATE:]], axis=-1).astype(BF16)

    cos, sin = _rope_tables(positions)
    xr = x.astype(F32).reshape(rows, D_MODEL)
    w_in_head = w_in[:, :, :COL_NG]
    for l in range(depth):
        (u_ssm, u_pool, q, cmp_in, ksel, vsel, kwin, vwin, ng, bg) = _inproj(
            xr, w_in_head, l, pack_w_tail(w_in[l]), cos, sin, bsz, seq)
        y_ssm = _s5(u_ssm.reshape(bsz, seq, SSM_WIDTH), lre[l], lim[l], s5_wb[l], s5_wc[l], s5_d[l], ssm_w_glu, l)
        y_pool = _pool(u_pool.reshape(bsz, seq, POOL_WIDTH), pool_wb[l], pool_sc[l])
        cmp_z = cmp_in.reshape(2, bsz * N_KV_HEADS, seq // CMP_STRIDE, CMP_STRIDE * HEAD_DIM)
        cmp_kv = _compress(cmp_z, cmp_w1, l, cmp_posr[l], cmp_b1r[l], cmp_w2dup[l])
        y_nsa = _nsa(q, cmp_kv, ksel, vsel, kwin, vwin, ng, bsz, seq)
        x1 = _merge(xr, y_ssm.reshape(rows, SSM_WIDTH), y_pool.reshape(rows, POOL_WIDTH), y_nsa, bg,
                    w_branch, w_out, l, lng[l, 0], lnb[l, 0], alpha)
        xr = _ffn(x1, ffn_w_in, ffn_w_out, l, lng[l, 1], lnb[l, 1], alpha)
    return xr.reshape(bsz, seq, D_MODEL).astype(x.dtype)
```

```python
import functools
import math

import jax
import jax.numpy as jnp
from jax import lax
from jax.experimental import pallas as pl
from jax.experimental.pallas import tpu as pltpu

F32 = jnp.float32
BF16 = jnp.bfloat16

D_MODEL = 1024
SSM_WIDTH = 512
SSM_GROUP = 16
SSM_GROUPS = 32
SSM_STATE = 64
POOL_WIDTH = 512
POOL_WINDOWS = (2, 4, 8, 16)
POOL_GROUP = 128
HEAD_DIM = 64
N_HEADS = 8
N_KV_HEADS = 2
GQA_GROUP = 4
ATTN_WIDTH = 512
KV_WIDTH = 128
N_BRANCH = 3
CMP_BLOCK = 32
CMP_STRIDE = 16
CMP_HIDDEN = 256
SEL_BLOCK = 64
SEL_TOP = 16
WINDOW = 512
ROPE_THETA = 10000.0
FF_HIDDEN = 2816
LN_EPS = 1e-5
NEG = -1e30
N_GATE = 3 * N_HEADS
IN_RAW = 3 * 512 + 6 * KV_WIDTH + N_GATE + N_BRANCH * D_MODEL

LANES = 128
SUBLANES = 8
VMEM_LIMIT_BYTES = 58 * 1024 * 1024

COL_SSM = 0
COL_POOL = 512
COL_Q = 1024
COL_KV = 1536
COL_NG = 2304
COL_BG = COL_NG + N_KV_HEADS * LANES
IN_PACKED = COL_BG + N_BRANCH * D_MODEL
LOG2E = 1.4426950408889634

ROW_TILE = 512
PROJ_TILE = 1024
CHUNK_PERM_ROWS = 512
S5_STEPS = 64
S5_LANE_CHUNK = 512
POOL_HALO = 16
Q_TILE = 256
K_TILE = 256
FF_CHUNK = 256


def _cparams(n_axes):
    return pltpu.CompilerParams(dimension_semantics=("arbitrary",) * n_axes,
                                vmem_limit_bytes=VMEM_LIMIT_BYTES)


def _resident(shape):
    nd = len(shape)
    return pl.BlockSpec(shape, lambda *_: (0,) * nd, pipeline_mode=pl.Buffered(1))


def _resident_layer(stacked, layer):
    nd = stacked.ndim
    return pl.BlockSpec((None,) + stacked.shape[1:], lambda *_: (layer,) + (0,) * (nd - 1),
                        pipeline_mode=pl.Buffered(1))


def _gelu_tanh(x):
    return x * (0.5 * (1.0 + jnp.tanh(math.sqrt(2.0 / math.pi) * (x + 0.044715 * (x * x * x)))))


def _layer_norm(r, g, b):
    mu = jnp.mean(r, axis=-1, keepdims=True)
    c = r - mu
    var = jnp.mean(c * c, axis=-1, keepdims=True)
    return c * lax.rsqrt(var + LN_EPS) * g + b


def _dot(a, b):
    return jnp.dot(a, b, preferred_element_type=F32)


def _dot_nt(a, b):
    return lax.dot_general(a, b, (((1,), (1,)), ((), ())), preferred_element_type=F32)


def _rope_table_kernel(pos_ref, inv_ref, cos_ref, sin_ref):
    ang = pos_ref[...] * inv_ref[...]
    lane = lax.broadcasted_iota(jnp.int32, ang.shape, 1)
    first_half = jnp.bitwise_and(lane, HEAD_DIM - 1) < HEAD_DIM // 2
    cos_ref[...] = jnp.cos(ang)
    sin_ref[...] = jnp.where(first_half, -jnp.sin(ang), jnp.sin(ang))


def _rope_tables(positions):
    rows = positions.size
    inv = ROPE_THETA ** (-jnp.arange(0, HEAD_DIM, 2, dtype=F32) / HEAD_DIM)
    inv = jnp.tile(inv, LANES // (HEAD_DIM // 2)).reshape(1, LANES)
    pos = positions.astype(F32).reshape(rows, 1)
    tile = ROW_TILE
    return pl.pallas_call(
        _rope_table_kernel,
        grid=(rows // tile,),
        in_specs=[pl.BlockSpec((tile, 1), lambda i: (i, 0)),
                  pl.BlockSpec((1, LANES), lambda i: (0, 0))],
        out_specs=[pl.BlockSpec((tile, LANES), lambda i: (i, 0))] * 2,
        out_shape=[jax.ShapeDtypeStruct((rows, LANES), F32)] * 2,
        compiler_params=_cparams(1),
        name="rope_tables",
    )(pos, inv)


def _inproj_kernel(x_ref, w_head_ref, w_tail_ref, cos_ref, sin_ref, chunk_perm_ref,
                   ussm_ref, upool_ref, q_ref, cmp_ref, ksel_ref, vsel_ref, kwin_ref, vwin_ref,
                   ng_ref, bg_ref):
    xb = x_ref[...].astype(BF16)
    cos = cos_ref[...]
    sin = sin_ref[...]
    lane = lax.broadcasted_iota(jnp.int32, cos.shape, 1)
    first_half = jnp.bitwise_and(lane, HEAD_DIM - 1) < HEAD_DIM // 2
    left = lane < HEAD_DIM

    def proj(c0, width):
        if c0 < COL_NG:
            return _dot(xb, w_head_ref[:, c0:c0 + width].astype(BF16))
        return _dot(xb, w_tail_ref[:, c0 - COL_NG:c0 - COL_NG + width])

    def rope(t):
        swapped = jnp.where(first_half, pltpu.roll(t, LANES - HEAD_DIM // 2, 1),
                            pltpu.roll(t, HEAD_DIM // 2, 1))
        return t * cos + swapped * sin

    def dup(t):
        r = pltpu.roll(t, HEAD_DIM, 1)
        return jnp.where(left, t, r), jnp.where(left, r, t)

    def with_ones(t):
        return jnp.where(left, t, 1.0), jnp.where(left, pltpu.roll(t, HEAD_DIM, 1), 1.0)

    ussm_ref[...] = proj(COL_SSM, SSM_WIDTH).astype(BF16)
    upool_ref[...] = proj(COL_POOL, POOL_WIDTH).astype(BF16)
    scale = HEAD_DIM ** -0.5 * LOG2E
    for j in range(ATTN_WIDTH // LANES):
        t = rope(proj(COL_Q + j * LANES, LANES))
        q_ref[:, j * LANES:(j + 1) * LANES] = (t * scale).astype(BF16)

    kv = proj(COL_KV, 6 * KV_WIDTH)
    perm_rows = chunk_perm_ref.shape[0]
    n_chunks = perm_rows // CMP_STRIDE
    left_c = lax.broadcasted_iota(jnp.int32, (n_chunks, LANES), 1) < HEAD_DIM
    left_p = lax.broadcasted_iota(jnp.int32, (perm_rows, LANES), 1) < HEAD_DIM
    for j, roped in enumerate((True, False)):
        t = kv[:, j * LANES:(j + 1) * LANES]
        t = (rope(t) if roped else t).astype(BF16)
        for part in range(x_ref.shape[0] // perm_rows):
            by_pos = _dot(chunk_perm_ref[...], t[part * perm_rows:(part + 1) * perm_rows])
            rolled = pltpu.roll(by_pos, HEAD_DIM, 1)
            for h, dup_h in enumerate((jnp.where(left_p, by_pos, rolled), jnp.where(left_p, rolled, by_pos))):
                for i in range(CMP_STRIDE // 2):
                    even = dup_h[(2 * i) * n_chunks:(2 * i + 1) * n_chunks]
                    odd = dup_h[(2 * i + 1) * n_chunks:(2 * i + 2) * n_chunks]
                    cmp_ref[j, 0, h, part * n_chunks:(part + 1) * n_chunks, i * LANES:(i + 1) * LANES] = (
                        jnp.where(left_c, even, odd).astype(BF16))
    for j, (ref, is_key) in enumerate(((ksel_ref, True), (vsel_ref, False), (kwin_ref, True), (vwin_ref, False))):
        t = kv[:, (2 + j) * LANES:(3 + j) * LANES]
        a, b = dup(rope(t)) if is_key else with_ones(t)
        ref[0, :, 0:LANES] = a.astype(BF16)
        ref[0, :, LANES:2 * LANES] = b.astype(BF16)

    ng_ref[...] = jax.nn.sigmoid(proj(COL_NG, N_KV_HEADS * LANES))
    for k in range(N_BRANCH):
        bg_ref[:, k * D_MODEL:(k + 1) * D_MODEL] = jax.nn.sigmoid(proj(COL_BG + k * D_MODEL, D_MODEL)).astype(BF16)


def _inproj(xr, w_stack, layer, w_tail, cos, sin, bsz, seq):
    rows = bsz * seq
    head_spec = pl.BlockSpec((None, D_MODEL, COL_NG), lambda i: (layer, 0, 0), pipeline_mode=pl.Buffered(1))
    tile = PROJ_TILE
    n_s = seq // tile
    row = lambda i: (i, 0)
    dup_spec = pl.BlockSpec((1, tile, 2 * LANES), lambda i: (i // n_s, i % n_s, 0))
    chunk_w = CMP_STRIDE * HEAD_DIM
    r = jnp.arange(CHUNK_PERM_ROWS)
    n_c = CHUNK_PERM_ROWS // CMP_STRIDE
    chunk_perm = (r[None, :] == (r[:, None] % n_c) * CMP_STRIDE + r[:, None] // n_c).astype(BF16)
    out_shape = [
        jax.ShapeDtypeStruct((rows, SSM_WIDTH), BF16),
        jax.ShapeDtypeStruct((rows, POOL_WIDTH), BF16),
        jax.ShapeDtypeStruct((rows, ATTN_WIDTH), BF16),
        jax.ShapeDtypeStruct((2, bsz, N_KV_HEADS, seq // CMP_STRIDE, chunk_w), BF16),
        jax.ShapeDtypeStruct((bsz, seq, 2 * LANES), BF16),
        jax.ShapeDtypeStruct((bsz, seq, 2 * LANES), BF16),
        jax.ShapeDtypeStruct((bsz, seq, 2 * LANES), BF16),
        jax.ShapeDtypeStruct((bsz, seq, 2 * LANES), BF16),
        jax.ShapeDtypeStruct((rows, N_KV_HEADS * LANES), F32),
        jax.ShapeDtypeStruct((rows, N_BRANCH * D_MODEL), BF16),
    ]
    out_specs = [
        pl.BlockSpec((tile, SSM_WIDTH), row),
        pl.BlockSpec((tile, POOL_WIDTH), row),
        pl.BlockSpec((tile, ATTN_WIDTH), row),
        pl.BlockSpec((2, 1, N_KV_HEADS, tile // CMP_STRIDE, chunk_w), lambda i: (0, i // n_s, 0, i % n_s, 0)),
        dup_spec, dup_spec, dup_spec, dup_spec,
        pl.BlockSpec((tile, N_KV_HEADS * LANES), row),
        pl.BlockSpec((tile, N_BRANCH * D_MODEL), row),
    ]
    return pl.pallas_call(
        _inproj_kernel,
        grid=(rows // tile,),
        in_specs=[pl.BlockSpec((tile, D_MODEL), row), head_spec, _resident(w_tail.shape),
                  pl.BlockSpec((tile, LANES), row), pl.BlockSpec((tile, LANES), row), _resident(chunk_perm.shape)],
        out_specs=out_specs,
        out_shape=out_shape,
        compiler_params=_cparams(1),
        name="in_projection",
    )(xr, w_stack, w_tail, cos, sin, chunk_perm)


def _time_major_perms(bsz, steps):
    r = jnp.arange(bsz * steps)
    to_tm = (r[None, :] == (r[:, None] % bsz) * steps + r[:, None] // bsz).astype(BF16)
    return to_tm, to_tm.T


def _time_mixers_kernel(us_ref, up_ref, tm_ref, bm_ref, lre_ref, lim_ref, wb_ref, wc_ref, d_ref, wglu_ref,
                        pw_ref, psc_ref, os_ref, op_ref, bu_ref, st_ref, y_ref, ext_ref, yp_ref):
    @pl.when(pl.program_id(0) == 0)
    def _():
        st_ref[...] = jnp.zeros(st_ref.shape, F32)
        ext_ref[0:POOL_HALO * SUBLANES, :] = jnp.zeros((POOL_HALO * SUBLANES, POOL_WIDTH), F32)

    _s5_kernel(us_ref, tm_ref, bm_ref, lre_ref, lim_ref, wb_ref, wc_ref, d_ref, wglu_ref, os_ref, bu_ref, st_ref, y_ref)
    _pool_kernel(up_ref, tm_ref, bm_ref, pw_ref, psc_ref, op_ref, ext_ref, yp_ref)


def _s5_kernel(u_ref, tm_ref, bm_ref, lre_ref, lim_ref, wb_ref, wc_ref, d_ref, wglu_ref, o_ref,
               bu_ref, st_ref, y_ref):
    bsz, n_steps, _ = u_ref.shape
    n_rows = bsz * n_steps
    n_chunk = wb_ref.shape[0]
    cw = S5_LANE_CHUNK

    u = _dot(tm_ref[...], u_ref[...].reshape(n_rows, SSM_WIDTH)).astype(BF16)
    for c in range(n_chunk):
        bu_ref[c] = _dot(u[:, c * LANES:(c + 1) * LANES], wb_ref[c])

    for c in range(n_chunk):
        lre = lre_ref[:, c * cw:(c + 1) * cw]
        lim = lim_ref[:, c * cw:(c + 1) * cw]
        hre = st_ref[0, :, c * cw:(c + 1) * cw]
        him = st_ref[1, :, c * cw:(c + 1) * cw]
        for t in range(n_steps):
            rows = slice(t * SUBLANES, (t + 1) * SUBLANES)
            hre, him = (lre * hre - lim * him + bu_ref[c, rows, 0:cw],
                        lre * him + lim * hre + bu_ref[c, rows, cw:2 * cw])
            bu_ref[c, rows, 0:cw] = hre
            bu_ref[c, rows, cw:2 * cw] = him
        st_ref[0, :, c * cw:(c + 1) * cw] = hre
        st_ref[1, :, c * cw:(c + 1) * cw] = him
        y_ref[:, c * LANES:(c + 1) * LANES] = _dot(bu_ref[c].astype(BF16), wc_ref[c])
    y = y_ref[...] + d_ref[...] * u.astype(F32)
    z = _dot(_gelu_tanh(y).astype(BF16), wglu_ref[...].astype(BF16))
    out = (z[:, :SSM_WIDTH] * jax.nn.sigmoid(z[:, SSM_WIDTH:])).astype(BF16)
    o_ref[...] = _dot(bm_ref[...], out).astype(BF16).reshape(bsz, n_steps, SSM_WIDTH)


def _time_mixers(u_ssm, u_pool, lre, lim, wb, wc, d, wglu_stack, layer, pool_w, pool_sc):
    bsz, seq, _ = u_ssm.shape
    steps = S5_STEPS
    tile = steps * bsz
    state_w = lre.shape[1]
    to_tm, to_bm = _time_major_perms(bsz, steps)
    blk = pl.BlockSpec((bsz, steps, SSM_WIDTH), lambda i: (0, i, 0))
    return pl.pallas_call(
        _time_mixers_kernel,
        grid=(seq // steps,),
        in_specs=[blk, blk, _resident(to_tm.shape), _resident(to_bm.shape),
                  _resident(lre.shape), _resident(lim.shape), _resident(wb.shape), _resident(wc.shape),
                  _resident(d.shape), _resident_layer(wglu_stack, layer),
                  _resident(pool_w.shape), _resident(pool_sc.shape)],
        out_specs=[blk, blk],
        out_shape=[jax.ShapeDtypeStruct(u_ssm.shape, BF16), jax.ShapeDtypeStruct(u_pool.shape, BF16)],
        scratch_shapes=[pltpu.VMEM((wb.shape[0], tile, 2 * S5_LANE_CHUNK), F32),
                        pltpu.VMEM((2, SUBLANES, state_w), F32),
                        pltpu.VMEM((tile, SSM_WIDTH), F32),
                        pltpu.VMEM((tile + POOL_HALO * SUBLANES, POOL_WIDTH), F32),
                        pltpu.VMEM((tile, POOL_WIDTH), BF16)],
        compiler_params=_cparams(1),
        name="time_mixers",
    )(u_ssm, u_pool, to_tm, to_bm, lre, lim, wb, wc, d, wglu_stack, pool_w, pool_sc)


def _pool_kernel(u_ref, tm_ref, bm_ref, w_ref, sc_ref, o_ref, ext_ref, y_ref):
    bsz, n_steps, _ = u_ref.shape
    n_rows = bsz * n_steps
    halo = POOL_HALO * SUBLANES
    i = pl.program_id(0)

    ext_ref[halo:halo + n_rows, :] = _dot(tm_ref[...], u_ref[...].reshape(n_rows, POOL_WIDTH))
    row = lax.broadcasted_iota(jnp.int32, (n_rows, POOL_GROUP), 0)
    t = i * (n_rows // SUBLANES) + jnp.right_shift(row, 3)
    for gi, w in enumerate(POOL_WINDOWS):
        c0 = gi * POOL_GROUP
        cur = ext_ref[halo:halo + n_rows, c0:c0 + POOL_GROUP]
        acc = cur
        for k in range(1, w):
            acc = acc + ext_ref[halo - k * SUBLANES:halo - k * SUBLANES + n_rows, c0:c0 + POOL_GROUP]
        cnt = jnp.minimum(t + 1, w).astype(F32)
        pooled = acc / cnt - cur
        mixed = _dot(pooled.astype(BF16), w_ref[gi]) * sc_ref[:, c0:c0 + POOL_GROUP]
        y_ref[:, c0:c0 + POOL_GROUP] = mixed.astype(BF16)
    ext_ref[0:halo, :] = ext_ref[n_rows:n_rows + halo, :]
    o_ref[...] = _dot(bm_ref[...], y_ref[...]).astype(BF16).reshape(bsz, n_steps, POOL_WIDTH)


def _compress_kernel(z_ref, w1_ref, pos_ref, b1_ref, w2_ref, o_ref):
    half = CMP_STRIDE * HEAD_DIM
    _, n_bh, n_chunks, width = z_ref.shape
    z = z_ref[0].reshape(n_bh * n_chunks, width)
    w_top = w1_ref[0, :half, :].astype(BF16)
    w_bot = w1_ref[0, half:, :].astype(BF16)
    bottom_next = pltpu.roll(_dot(z, w_bot), n_bh * n_chunks - 1, 0)
    pos = pos_ref[0].astype(BF16)
    cst = _dot(pos[:, :half], w_top) + _dot(pos[:, half:], w_bot)
    hid = _gelu_tanh(_dot(z, w_top) + bottom_next + cst[0:1, :] + b1_ref[0])
    out = _dot(hid.astype(BF16), w2_ref[0])
    lane = lax.broadcasted_iota(jnp.int32, out.shape, 1)
    is_value = pl.program_id(0) == 1
    out = jnp.where((lane >= HEAD_DIM) & is_value, 1.0, out).astype(BF16)
    o_ref[0] = out.reshape(n_bh, n_chunks, LANES)


def _compress(z, w1_stack, layer, pos, b1, w2dup):
    _, n_bh, n_chunks, width = z.shape
    return pl.pallas_call(
        _compress_kernel,
        grid=(2,),
        in_specs=[pl.BlockSpec((1, n_bh, n_chunks, width), lambda j: (j, 0, 0, 0)),
                  pl.BlockSpec((None, 1) + w1_stack.shape[2:], lambda j: (layer, j, 0, 0)),
                  pl.BlockSpec((1,) + pos.shape[1:], lambda j: (j, 0, 0)),
                  pl.BlockSpec((1,) + b1.shape[1:], lambda j: (j, 0, 0)),
                  pl.BlockSpec((1,) + w2dup.shape[1:], lambda j: (j, 0, 0))],
        out_specs=pl.BlockSpec((1, n_bh, n_chunks, LANES), lambda j: (j, 0, 0, 0)),
        out_shape=jax.ShapeDtypeStruct((2, n_bh, n_chunks, LANES), BF16),
        compiler_params=_cparams(1),
        name="compress_mlp",
    )(z, w1_stack, pos, b1, w2dup)


def _nsa_kernel(q_ref, kc_ref, vc_ref, ks_ref, vs_ref, kw_ref, vw_ref, g_ref,
                win_bias_ref, overlap_ref, expand_ref, pick_ref, o_ref, *scratch, seq):
    tile_idx = pl.program_id(1)
    heads = []
    for h in range(N_KV_HEADS):
        pair_cols = slice(h * 2 * LANES, (h + 1) * 2 * LANES)
        cols = slice(h * LANES, (h + 1) * LANES)
        sel_state, pw_ref, pc_ref = scratch[3 * h:3 * h + 3]
        heads.append(_nsa_head(q_ref.at[:, pair_cols], kc_ref.at[0, h], vc_ref.at[0, h],
                               ks_ref.at[0, :, cols], vs_ref.at[0, :, cols], kw_ref.at[0, :, cols],
                               vw_ref.at[0, :, cols], g_ref.at[:, cols],
                               win_bias_ref, overlap_ref, expand_ref, pick_ref, o_ref.at[:, pair_cols],
                               sel_state, pw_ref, pc_ref, seq=seq, tile_idx=tile_idx))
    streams = [next(head) for head in heads]
    def pair_step(j, stream):
        scores, consume, buf_a, buf_b, _ = stream
        scores(2 * j + 1, buf_b)
        consume(2 * j, buf_a)
        scores(2 * j + 2, buf_a)
        consume(2 * j + 1, buf_b)

    for stream in streams:
        stream[0](0, stream[2])
    for stream in streams:
        pair_step(0, stream)

    for stream in streams:
        def body(j, carry, stream=stream):
            pair_step(j, stream)
            return carry

        lax.fori_loop(1, stream[4], body, 0)
    for head in heads:
        next(head, None)


def _nsa_head(q_ref, kc_ref, vc_ref, ks_ref, vs_ref, kw_ref, vw_ref, g_ref,
              win_bias_ref, overlap_ref, expand_ref, pick_ref, o_ref,
              sel_state, pw_ref, pc_ref, *, seq, tile_idx):
    tq = q_ref.shape[0]
    tk = sel_state[0].shape[1]
    n_sel = seq // SEL_BLOCK
    n_top = min(SEL_TOP, n_sel)
    n_cmp = (seq - CMP_BLOCK) // CMP_STRIDE + 1
    t0 = tile_idx * tq
    head_rows = [slice(g * tq, (g + 1) * tq) for g in range(GQA_GROUP)]

    lane = lax.broadcasted_iota(jnp.int32, (tq, LANES), 1)
    left = lane < HEAD_DIM
    zero = jnp.zeros((tq, LANES), BF16)
    pairs = (q_ref[:, 0:LANES], q_ref[:, LANES:2 * LANES])
    q4 = jnp.concatenate([jnp.where(left, pairs[0], zero), jnp.where(left, zero, pairs[0]),
                          jnp.where(left, pairs[1], zero), jnp.where(left, zero, pairs[1])], axis=0)
    tq_col = t0 + lax.broadcasted_iota(jnp.int32, (tq, 1), 0)

    def online_branch(state, k_ref, v_ref, tile_start, tile_bias):
        buf_a, buf_b, p_ref, m_ref, a_ref, acc_ref = state
        m_ref[...] = jnp.full(m_ref.shape, NEG, F32)
        acc_ref[...] = jnp.zeros(acc_ref.shape, F32)

        def scores(kt, dst_ref):
            bias = tile_bias(kt)
            s = _dot_nt(q4, k_ref[pl.ds(tile_start(kt), tk), :])
            for rows in head_rows:
                dst_ref[rows, :] = s[rows] + bias

        def consume(kt, src_ref):
            for rows in head_rows:
                s = src_ref[rows, :]
                m_old = m_ref[rows, :]
                m_new = jnp.maximum(m_old, jnp.max(s, axis=-1, keepdims=True))
                a_ref[rows, :] = jnp.exp2(m_old - m_new)
                m_ref[rows, :] = m_new
                p_ref[rows, :] = jnp.exp2(s - jnp.concatenate([m_new] * (tk // LANES), axis=1)).astype(BF16)
            acc_ref[...] = a_ref[...] * acc_ref[...] + _dot(p_ref[...], v_ref[pl.ds(tile_start(kt), tk), :])

        return buf_a, buf_b, scores, consume, acc_ref


    wk = pw_ref.shape[1]
    w0 = pl.multiple_of(jnp.maximum(t0 - WINDOW, 0), tq)
    bias_w = win_bias_ref[jnp.minimum(tile_idx, win_bias_ref.shape[0] - 1)]
    s_w = _dot_nt(q4, kw_ref[pl.ds(w0, wk), :])
    for rows in head_rows:
        s = s_w[rows] + bias_w
        pw_ref[rows, :] = jnp.exp2(s - jnp.max(s, axis=-1, keepdims=True)).astype(BF16)
    acc_w = _dot(pw_ref[...], vw_ref[pl.ds(w0, wk), :])

    n_idx = lax.broadcasted_iota(jnp.int32, (1, kc_ref.shape[0]), 1)
    cmp_valid = (n_idx * CMP_STRIDE + (CMP_BLOCK - 1) <= tq_col) & (n_idx < n_cmp)
    s_c = _dot_nt(q4, kc_ref[...])
    p_sum = None
    for rows in head_rows:
        s = jnp.where(cmp_valid, s_c[rows], NEG)
        e = jnp.where(cmp_valid, jnp.exp2(s - jnp.max(s, axis=-1, keepdims=True)), 0.0)
        l = jnp.sum(e, axis=-1, keepdims=True)
        p = e / jnp.where(l > 0.0, l, 1.0)
        p_sum = p if p_sum is None else p_sum + p
        pc_ref[rows, :] = p.astype(BF16)
    o_cmp = _dot(pc_ref[...], vc_ref[...])

    imp = lax.dot_general(overlap_ref[...], p_sum, (((1,), (1,)), ((), ())), precision=lax.Precision.HIGHEST,
                          preferred_element_type=F32)
    jb = lax.broadcasted_iota(jnp.int32, (n_sel, tq), 0)
    tt = t0 + lax.broadcasted_iota(jnp.int32, (n_sel, tq), 1)
    cur = jnp.right_shift(tt, 6)
    forced = (jb == 0) | (jb == cur) | (jb == cur - 1)
    causal = jb * SEL_BLOCK <= tt
    score = jnp.where(forced, 1e30, jnp.where(causal, imp, NEG))
    rank = jnp.zeros((n_sel, tq), F32)
    for a in range(n_sel):
        sa = score[a:a + 1, :]
        ahead = (sa > score) | ((sa == score) & (jb > a))
        rank = rank + jnp.where(ahead, 1.0, 0.0)
    chosen = (rank < float(n_top)) & causal
    sel_bias = jnp.where(chosen, 0.0, NEG).T.astype(BF16)

    def sel_bias_tile(kt):
        k0 = kt * tk
        expand = expand_ref[:, pl.ds(tile_start(kt), tk)]
        bias = _dot(sel_bias, expand)
        return jnp.where(k0 + lax.broadcasted_iota(jnp.int32, (1, tk), 1) <= tq_col, bias, NEG)

    def tile_start(kt):
        return pl.multiple_of(jnp.minimum(kt * tk, seq - tk), tk)

    sa, sb, sel_scores, sel_consume, sel_acc = online_branch(
        sel_state, ks_ref, vs_ref, tile_start, sel_bias_tile)

    n_kt = (t0 + tq + tk - 1) // tk
    yield sel_scores, sel_consume, sa, sb, (n_kt + 1) // 2
    acc_s = sel_acc[...]

    g = g_ref[...]
    g_hi = g.astype(BF16)
    g_lo = (g - g_hi.astype(F32)).astype(BF16)
    gates = _dot(jnp.concatenate([g_hi, g_lo], axis=1), pick_ref[...])

    def pair_tile(acc, pair, normalise):
        a, b = acc[head_rows[2 * pair]], acc[head_rows[2 * pair + 1]]
        num = jnp.where(left, a, pltpu.roll(b, HEAD_DIM, 1))
        return num / jnp.where(left, pltpu.roll(a, HEAD_DIM, 1), b) if normalise else num

    for pair in range(GQA_GROUP // 2):
        out = None
        for br, (acc, normalise) in enumerate(((o_cmp, False), (acc_s, True), (acc_w, True))):
            tile = br * (GQA_GROUP // 2) + pair
            term = gates[:, tile * LANES:(tile + 1) * LANES] * pair_tile(acc, pair, normalise)
            out = term if out is None else out + term
        o_ref[:, pair * LANES:(pair + 1) * LANES] = out.astype(BF16)


def _nsa(q, cmp_kv, ksel, vsel, kwin, vwin, ng, bsz, seq):
    tq = Q_TILE
    n_q = seq // tq
    n_chunks = cmp_kv.shape[2]
    rows = GQA_GROUP * tq
    assert seq % K_TILE == 0
    qrow = lambda b, i: (b * n_q + i, 0)
    kv_spec = pl.BlockSpec((1, seq, N_KV_HEADS * LANES), lambda b, i: (b, 0, 0))
    cmp_kv = cmp_kv.reshape(2, bsz, N_KV_HEADS, n_chunks, LANES)

    def branch_state():
        return (pltpu.VMEM((rows, K_TILE), F32), pltpu.VMEM((rows, K_TILE), F32), pltpu.VMEM((rows, K_TILE), BF16),
                pltpu.VMEM((rows, LANES), F32), pltpu.VMEM((rows, LANES), F32), pltpu.VMEM((rows, LANES), F32))

    wk = WINDOW + tq
    case = jnp.arange(WINDOW // tq + 1)[:, None, None]
    dist = jnp.minimum(case * tq, WINDOW) + jnp.arange(tq)[None, :, None] - jnp.arange(wk)[None, None, :]
    win_bias = jnp.where((dist >= 0) & (dist < WINDOW), 0.0, NEG).astype(F32)
    n_sel = seq // SEL_BLOCK
    n_cmp = (seq - CMP_BLOCK) // CMP_STRIDE + 1
    cn = jnp.arange(n_chunks)[None, :] * CMP_STRIDE
    sj = jnp.arange(n_sel)[:, None] * SEL_BLOCK
    overlap = jnp.clip(jnp.minimum(cn + CMP_BLOCK, sj + SEL_BLOCK) - jnp.maximum(cn, sj), 0, None).astype(F32) / CMP_BLOCK
    overlap = jnp.where(jnp.arange(n_chunks)[None, :] < n_cmp, overlap, 0.0)
    expand = (jnp.arange(n_sel)[:, None] == jnp.arange(seq)[None, :] // SEL_BLOCK).astype(BF16)
    n_tiles = 3 * (GQA_GROUP // 2)
    src = jnp.arange(2 * LANES)[:, None] % LANES
    dst = jnp.arange(n_tiles * LANES)[None, :]
    pick = (src == 2 * (dst // LANES) + (dst // HEAD_DIM) % 2).astype(BF16)
    return pl.pallas_call(
        functools.partial(_nsa_kernel, seq=seq),
        grid=(bsz, n_q),
        in_specs=[pl.BlockSpec((tq, ATTN_WIDTH), qrow),
                  pl.BlockSpec((None, 1, N_KV_HEADS, n_chunks, LANES), lambda b, i: (0, b, 0, 0, 0)),
                  pl.BlockSpec((None, 1, N_KV_HEADS, n_chunks, LANES), lambda b, i: (1, b, 0, 0, 0)),
                  kv_spec, kv_spec, kv_spec, kv_spec,
                  pl.BlockSpec((tq, N_KV_HEADS * LANES), qrow),
                  _resident(win_bias.shape), _resident(overlap.shape), _resident(expand.shape), _resident(pick.shape)],
        out_specs=pl.BlockSpec((tq, ATTN_WIDTH), qrow),
        out_shape=jax.ShapeDtypeStruct((bsz * seq, ATTN_WIDTH), BF16),
        scratch_shapes=[shape for _ in range(N_KV_HEADS) for shape in
                        (branch_state(), pltpu.VMEM((rows, WINDOW + tq), BF16), pltpu.VMEM((rows, n_chunks), BF16))],
        compiler_params=_cparams(2),
        name="nsa_attention",
    )(q, cmp_kv, cmp_kv, ksel, vsel, kwin, vwin, ng, win_bias, overlap, expand, pick)


def _merge_kernel(x_ref, ys_ref, yp_ref, yn_ref, bg_ref, wb_ref, wo_ref, g_ref, b_ref, o_ref, *, alpha):
    merged = None
    for k, y_ref in enumerate((ys_ref, yp_ref, yn_ref)):
        term = bg_ref[:, k * D_MODEL:(k + 1) * D_MODEL].astype(F32) * _dot(y_ref[...], wb_ref[k].astype(BF16))
        merged = term if merged is None else merged + term
    r = alpha * x_ref[...] + _dot(merged.astype(BF16), wo_ref[...].astype(BF16))
    o_ref[...] = _layer_norm(r, g_ref[...], b_ref[...])


def _merge(xr, ys, yp, yn, bg, wb_stack, wo_stack, layer, g, b, alpha):
    rows = xr.shape[0]
    tile = PROJ_TILE
    row = lambda i: (i, 0)
    return pl.pallas_call(
        functools.partial(_merge_kernel, alpha=alpha),
        grid=(rows // tile,),
        in_specs=[pl.BlockSpec((tile, D_MODEL), row),
                  pl.BlockSpec((tile, SSM_WIDTH), row), pl.BlockSpec((tile, POOL_WIDTH), row),
                  pl.BlockSpec((tile, ATTN_WIDTH), row), pl.BlockSpec((tile, N_BRANCH * D_MODEL), row),
                  _resident_layer(wb_stack, layer), _resident_layer(wo_stack, layer),
                  _resident(g.shape), _resident(b.shape)],
        out_specs=pl.BlockSpec((tile, D_MODEL), row),
        out_shape=jax.ShapeDtypeStruct((rows, D_MODEL), F32),
        compiler_params=_cparams(1),
        name="branch_merge",
    )(xr, ys, yp, yn, bg, wb_stack, wo_stack, g, b)


def _ffn_kernel(x_ref, wi_ref, wo_ref, g_ref, b_ref, o_ref, *, alpha):
    x = x_ref[...]
    xb = x.astype(BF16)
    acc = None
    for c in range(FF_HIDDEN // FF_CHUNK):
        c0 = c * FF_CHUNK
        hg = _dot(xb, wi_ref[:, c0:c0 + FF_CHUNK].astype(BF16))
        hu = _dot(xb, wi_ref[:, FF_HIDDEN + c0:FF_HIDDEN + c0 + FF_CHUNK].astype(BF16))
        act = (hg * jax.nn.sigmoid(hg) * hu).astype(BF16)
        part = _dot(act, wo_ref[c0:c0 + FF_CHUNK, :].astype(BF16))
        acc = part if acc is None else acc + part
    o_ref[...] = _layer_norm(alpha * x + acc, g_ref[...], b_ref[...])


def _ffn(xr, wi_stack, wo_stack, layer, g, b, alpha):
    rows = xr.shape[0]
    tile = ROW_TILE
    row = lambda i: (i, 0)
    return pl.pallas_call(
        functools.partial(_ffn_kernel, alpha=alpha),
        grid=(rows // tile,),
        in_specs=[pl.BlockSpec((tile, D_MODEL), row), _resident_layer(wi_stack, layer),
                  _resident_layer(wo_stack, layer), _resident(g.shape), _resident(b.shape)],
        out_specs=pl.BlockSpec((tile, D_MODEL), row),
        out_shape=jax.ShapeDtypeStruct((rows, D_MODEL), F32),
        compiler_params=_cparams(1),
        name="swiglu_ffn",
    )(xr, wi_stack, wo_stack, g, b)


def _pack_s5(a_re, a_im, log_dt, b_re, b_im, c_re, c_im):
    depth = a_re.shape[0]
    a_re, a_im = a_re.astype(F32), a_im.astype(F32)
    dt = jnp.exp(log_dt.astype(F32))[..., None]
    mag = jnp.exp(a_re * dt)
    lbar_re, lbar_im = mag * jnp.cos(a_im * dt), mag * jnp.sin(a_im * dt)
    den = a_re * a_re + a_im * a_im
    coef_re = ((lbar_re - 1.0) * a_re + lbar_im * a_im) / den
    coef_im = (lbar_im * a_re - (lbar_re - 1.0) * a_im) / den
    b_re, b_im = b_re.astype(F32), b_im.astype(F32)
    bbar_re = coef_re[..., None] * b_re - coef_im[..., None] * b_im
    bbar_im = coef_re[..., None] * b_im + coef_im[..., None] * b_re
    gpc = S5_LANE_CHUNK // SSM_STATE
    n_chunk = SSM_GROUPS // gpc
    eye = jnp.eye(gpc, dtype=F32)

    def b_block(part):
        v = part.transpose(0, 1, 3, 2).reshape(depth, n_chunk, gpc, SSM_GROUP, SSM_STATE)
        return jnp.einsum('xy,dqxcp->dqxcyp', eye, v).reshape(depth, n_chunk, gpc * SSM_GROUP, gpc * SSM_STATE)

    def c_block(part):
        v = part.reshape(depth, n_chunk, gpc, SSM_GROUP, SSM_STATE)
        return jnp.einsum('xy,dqxcp->dqxpyc', eye, v).reshape(depth, n_chunk, gpc * SSM_STATE, gpc * SSM_GROUP)

    wb = jnp.concatenate([b_block(bbar_re), b_block(bbar_im)], axis=-1).astype(BF16)
    wc = jnp.concatenate([c_block(c_re.astype(F32)), c_block(-c_im.astype(F32))], axis=-2).astype(BF16)
    state_w = SSM_GROUPS * SSM_STATE
    lre = jnp.broadcast_to(lbar_re.reshape(depth, 1, state_w), (depth, SUBLANES, state_w))
    lim = jnp.broadcast_to(lbar_im.reshape(depth, 1, state_w), (depth, SUBLANES, state_w))
    return lre, lim, wb, wc


def kernel(x, positions, w_in, ssm_a_re, ssm_a_im, ssm_log_dt, ssm_b_re, ssm_b_im, ssm_c_re, ssm_c_im,
           ssm_d, ssm_w_glu, pool_w, pool_scale, cmp_pos, cmp_w1, cmp_b1, cmp_w2,
           w_branch, w_out, ln_g, ln_b, ffn_w_in, ffn_w_out):
    bsz, seq, _ = x.shape
    depth = w_in.shape[0]
    rows = bsz * seq
    assert bsz == SUBLANES and seq % PROJ_TILE == 0 and seq >= WINDOW + Q_TILE
    alpha = (2 * depth) ** 0.25

    assert w_in.shape[-1] == IN_RAW
    lre, lim, s5_wb, s5_wc = _pack_s5(ssm_a_re, ssm_a_im, ssm_log_dt, ssm_b_re, ssm_b_im, ssm_c_re, ssm_c_im)
    s5_d = ssm_d.astype(F32).reshape(depth, 1, SSM_WIDTH)
    pool_wb = pool_w.astype(BF16)
    pool_sc = pool_scale.astype(F32).reshape(depth, 1, POOL_WIDTH)
    half = CMP_STRIDE * HEAD_DIM
    cmp_posr = jnp.broadcast_to(cmp_pos.astype(F32).reshape(depth, 2, 1, CMP_BLOCK * HEAD_DIM),
                                (depth, 2, SUBLANES, CMP_BLOCK * HEAD_DIM))
    cmp_b1r = cmp_b1.astype(F32).reshape(depth, 2, 1, CMP_HIDDEN)
    cmp_w2dup = jnp.concatenate([cmp_w2, cmp_w2 * jnp.array([1.0, 0.0], cmp_w2.dtype).reshape(1, 2, 1, 1)],
                                axis=-1).astype(BF16)
    lng = ln_g.astype(F32).reshape(depth, 2, 1, D_MODEL)
    lnb = ln_b.astype(F32).reshape(depth, 2, 1, D_MODEL)

    def pack_w_tail(w):
        w_gate = w[:, COL_NG:COL_NG + N_GATE].reshape(D_MODEL, 3, N_KV_HEADS, GQA_GROUP)
        w_gate = w_gate.transpose(0, 2, 1, 3).reshape(D_MODEL, N_KV_HEADS, 3 * GQA_GROUP)
        w_gate = jnp.pad(w_gate, ((0, 0), (0, 0), (0, LANES - 3 * GQA_GROUP))).reshape(D_MODEL, N_KV_HEADS * LANES)
        return jnp.concatenate([w_gate, w[:, COL_NG + N_GATE:]], axis=-1).astype(BF16)

    cos, sin = _rope_tables(positions)
    xr = x.astype(F32).reshape(rows, D_MODEL)
    w_in_head = w_in[:, :, :COL_NG]
    for l in range(depth):
        (u_ssm, u_pool, q, cmp_in, ksel, vsel, kwin, vwin, ng, bg) = _inproj(
            xr, w_in_head, l, pack_w_tail(w_in[l]), cos, sin, bsz, seq)
        y_ssm, y_pool = _time_mixers(u_ssm.reshape(bsz, seq, SSM_WIDTH), u_pool.reshape(bsz, seq, POOL_WIDTH),
                                     lre[l], lim[l], s5_wb[l], s5_wc[l], s5_d[l], ssm_w_glu, l, pool_wb[l], pool_sc[l])
        cmp_z = cmp_in.reshape(2, bsz * N_KV_HEADS, seq // CMP_STRIDE, CMP_STRIDE * HEAD_DIM)
        cmp_kv = _compress(cmp_z, cmp_w1, l, cmp_posr[l], cmp_b1r[l], cmp_w2dup[l])
        y_nsa = _nsa(q, cmp_kv, ksel, vsel, kwin, vwin, ng, bsz, seq)
        x1 = _merge(xr, y_ssm.reshape(rows, SSM_WIDTH), y_pool.reshape(rows, POOL_WIDTH), y_nsa, bg,
                    w_branch, w_out, l, lng[l, 0], lnb[l, 0], alpha)
        xr = _ffn(x1, ffn_w_in, ffn_w_out, l, lng[l, 1], lnb[l, 1], alpha)
    return xr.reshape(bsz, seq, D_MODEL).astype(x.dtype)
```

```python
import functools
import math

import jax
import jax.numpy as jnp
from jax import lax
from jax.experimental import pallas as pl
from jax.experimental.pallas import tpu as pltpu

F32 = jnp.float32
BF16 = jnp.bfloat16

D_MODEL = 1024
SSM_WIDTH = 512
SSM_GROUP = 16
SSM_GROUPS = 32
SSM_STATE = 64
POOL_WIDTH = 512
POOL_WINDOWS = (2, 4, 8, 16)
POOL_GROUP = 128
HEAD_DIM = 64
N_HEADS = 8
N_KV_HEADS = 2
GQA_GROUP = 4
ATTN_WIDTH = 512
KV_WIDTH = 128
N_BRANCH = 3
CMP_BLOCK = 32
CMP_STRIDE = 16
CMP_HIDDEN = 256
SEL_BLOCK = 64
SEL_TOP = 16
WINDOW = 512
ROPE_THETA = 10000.0
FF_HIDDEN = 2816
LN_EPS = 1e-5
NEG = -1e30
N_GATE = 3 * N_HEADS
IN_RAW = 3 * 512 + 6 * KV_WIDTH + N_GATE + N_BRANCH * D_MODEL

LANES = 128
SUBLANES = 8
VMEM_LIMIT_BYTES = 58 * 1024 * 1024

COL_SSM = 0
COL_POOL = 512
COL_Q = 1024
COL_KV = 1536
COL_NG = 2304
COL_BG = COL_NG + N_KV_HEADS * LANES
IN_PACKED = COL_BG + N_BRANCH * D_MODEL
LOG2E = 1.4426950408889634

ROW_TILE = 512
PROJ_TILE = 1024
CHUNK_PERM_ROWS = 512
S5_STEPS = 64
S5_LANE_CHUNK = 512
POOL_HALO = 16
Q_TILE = 256
K_TILE = 256
FF_CHUNK = 256


def _cparams(n_axes, carried=False):
    return pltpu.CompilerParams(dimension_semantics=("arbitrary" if carried else "parallel",) * n_axes,
                                vmem_limit_bytes=VMEM_LIMIT_BYTES)


def _resident(shape):
    nd = len(shape)
    return pl.BlockSpec(shape, lambda *_: (0,) * nd, pipeline_mode=pl.Buffered(1))


def _resident_layer(stacked, layer):
    nd = stacked.ndim
    return pl.BlockSpec((None,) + stacked.shape[1:], lambda *_: (layer,) + (0,) * (nd - 1),
                        pipeline_mode=pl.Buffered(1))


def _gelu_tanh(x):
    return x * (0.5 * (1.0 + jnp.tanh(math.sqrt(2.0 / math.pi) * (x + 0.044715 * (x * x * x)))))


def _layer_norm(r, g, b):
    mu = jnp.mean(r, axis=-1, keepdims=True)
    c = r - mu
    var = jnp.mean(c * c, axis=-1, keepdims=True)
    return c * lax.rsqrt(var + LN_EPS) * g + b


def _dot(a, b):
    return jnp.dot(a, b, preferred_element_type=F32)


def _dot_nt(a, b):
    return lax.dot_general(a, b, (((1,), (1,)), ((), ())), preferred_element_type=F32)


def _rope_table_kernel(pos_ref, inv_ref, cos_ref, sin_ref):
    ang = pos_ref[...] * inv_ref[...]
    lane = lax.broadcasted_iota(jnp.int32, ang.shape, 1)
    first_half = jnp.bitwise_and(lane, HEAD_DIM - 1) < HEAD_DIM // 2
    cos_ref[...] = jnp.cos(ang)
    sin_ref[...] = jnp.where(first_half, -jnp.sin(ang), jnp.sin(ang))


def _rope_tables(positions):
    rows = positions.size
    inv = ROPE_THETA ** (-jnp.arange(0, HEAD_DIM, 2, dtype=F32) / HEAD_DIM)
    inv = jnp.tile(inv, LANES // (HEAD_DIM // 2)).reshape(1, LANES)
    pos = positions.astype(F32).reshape(rows, 1)
    tile = ROW_TILE
    return pl.pallas_call(
        _rope_table_kernel,
        grid=(rows // tile,),
        in_specs=[pl.BlockSpec((tile, 1), lambda i: (i, 0)),
                  pl.BlockSpec((1, LANES), lambda i: (0, 0))],
        out_specs=[pl.BlockSpec((tile, LANES), lambda i: (i, 0))] * 2,
        out_shape=[jax.ShapeDtypeStruct((rows, LANES), F32)] * 2,
        compiler_params=_cparams(1),
        name="rope_tables",
    )(pos, inv)


def _inproj_kernel(x_ref, w_head_ref, w_tail_ref, cos_ref, sin_ref, chunk_perm_ref,
                   ussm_ref, upool_ref, q_ref, cmp_ref, ksel_ref, vsel_ref, kwin_ref, vwin_ref,
                   ng_ref, bg_ref):
    xb = x_ref[...].astype(BF16)
    cos = cos_ref[...]
    sin = sin_ref[...]
    lane = lax.broadcasted_iota(jnp.int32, cos.shape, 1)
    first_half = jnp.bitwise_and(lane, HEAD_DIM - 1) < HEAD_DIM // 2
    left = lane < HEAD_DIM

    def proj(c0, width):
        if c0 < COL_NG:
            return _dot(xb, w_head_ref[:, c0:c0 + width].astype(BF16))
        return _dot(xb, w_tail_ref[:, c0 - COL_NG:c0 - COL_NG + width])

    def rope(t):
        swapped = jnp.where(first_half, pltpu.roll(t, LANES - HEAD_DIM // 2, 1),
                            pltpu.roll(t, HEAD_DIM // 2, 1))
        return t * cos + swapped * sin

    def dup(t):
        r = pltpu.roll(t, HEAD_DIM, 1)
        return jnp.where(left, t, r), jnp.where(left, r, t)

    def with_ones(t):
        return jnp.where(left, t, 1.0), jnp.where(left, pltpu.roll(t, HEAD_DIM, 1), 1.0)

    ussm_ref[...] = proj(COL_SSM, SSM_WIDTH).astype(BF16)
    upool_ref[...] = proj(COL_POOL, POOL_WIDTH).astype(BF16)
    scale = HEAD_DIM ** -0.5 * LOG2E
    for j in range(ATTN_WIDTH // LANES):
        t = rope(proj(COL_Q + j * LANES, LANES))
        q_ref[:, j * LANES:(j + 1) * LANES] = (t * scale).astype(BF16)

    kv = proj(COL_KV, 6 * KV_WIDTH)
    perm_rows = chunk_perm_ref.shape[0]
    n_chunks = perm_rows // CMP_STRIDE
    left_c = lax.broadcasted_iota(jnp.int32, (n_chunks, LANES), 1) < HEAD_DIM
    left_p = lax.broadcasted_iota(jnp.int32, (perm_rows, LANES), 1) < HEAD_DIM
    for j, roped in enumerate((True, False)):
        t = kv[:, j * LANES:(j + 1) * LANES]
        t = (rope(t) if roped else t).astype(BF16)
        for part in range(x_ref.shape[0] // perm_rows):
            by_pos = _dot(chunk_perm_ref[...], t[part * perm_rows:(part + 1) * perm_rows])
            rolled = pltpu.roll(by_pos, HEAD_DIM, 1)
            for h, dup_h in enumerate((jnp.where(left_p, by_pos, rolled), jnp.where(left_p, rolled, by_pos))):
                for i in range(CMP_STRIDE // 2):
                    even = dup_h[(2 * i) * n_chunks:(2 * i + 1) * n_chunks]
                    odd = dup_h[(2 * i + 1) * n_chunks:(2 * i + 2) * n_chunks]
                    cmp_ref[j, 0, h, part * n_chunks:(part + 1) * n_chunks, i * LANES:(i + 1) * LANES] = (
                        jnp.where(left_c, even, odd).astype(BF16))
    for j, (ref, is_key) in enumerate(((ksel_ref, True), (vsel_ref, False), (kwin_ref, True), (vwin_ref, False))):
        t = kv[:, (2 + j) * LANES:(3 + j) * LANES]
        a, b = dup(rope(t)) if is_key else with_ones(t)
        ref[0, :, 0:LANES] = a.astype(BF16)
        ref[0, :, LANES:2 * LANES] = b.astype(BF16)

    ng_ref[...] = jax.nn.sigmoid(proj(COL_NG, N_KV_HEADS * LANES))
    for k in range(N_BRANCH):
        bg_ref[:, k * D_MODEL:(k + 1) * D_MODEL] = jax.nn.sigmoid(proj(COL_BG + k * D_MODEL, D_MODEL)).astype(BF16)


def _inproj(xr, w_stack, layer, w_tail, cos, sin, bsz, seq):
    rows = bsz * seq
    head_spec = pl.BlockSpec((None, D_MODEL, COL_NG), lambda i: (layer, 0, 0), pipeline_mode=pl.Buffered(1))
    tile = PROJ_TILE
    n_s = seq // tile
    row = lambda i: (i, 0)
    dup_spec = pl.BlockSpec((1, tile, 2 * LANES), lambda i: (i // n_s, i % n_s, 0))
    chunk_w = CMP_STRIDE * HEAD_DIM
    r = jnp.arange(CHUNK_PERM_ROWS)
    n_c = CHUNK_PERM_ROWS // CMP_STRIDE
    chunk_perm = (r[None, :] == (r[:, None] % n_c) * CMP_STRIDE + r[:, None] // n_c).astype(BF16)
    out_shape = [
        jax.ShapeDtypeStruct((rows, SSM_WIDTH), BF16),
        jax.ShapeDtypeStruct((rows, POOL_WIDTH), BF16),
        jax.ShapeDtypeStruct((rows, ATTN_WIDTH), BF16),
        jax.ShapeDtypeStruct((2, bsz, N_KV_HEADS, seq // CMP_STRIDE, chunk_w), BF16),
        jax.ShapeDtypeStruct((bsz, seq, 2 * LANES), BF16),
        jax.ShapeDtypeStruct((bsz, seq, 2 * LANES), BF16),
        jax.ShapeDtypeStruct((bsz, seq, 2 * LANES), BF16),
        jax.ShapeDtypeStruct((bsz, seq, 2 * LANES), BF16),
        jax.ShapeDtypeStruct((rows, N_KV_HEADS * LANES), F32),
        jax.ShapeDtypeStruct((rows, N_BRANCH * D_MODEL), BF16),
    ]
    out_specs = [
        pl.BlockSpec((tile, SSM_WIDTH), row),
        pl.BlockSpec((tile, POOL_WIDTH), row),
        pl.BlockSpec((tile, ATTN_WIDTH), row),
        pl.BlockSpec((2, 1, N_KV_HEADS, tile // CMP_STRIDE, chunk_w), lambda i: (0, i // n_s, 0, i % n_s, 0)),
        dup_spec, dup_spec, dup_spec, dup_spec,
        pl.BlockSpec((tile, N_KV_HEADS * LANES), row),
        pl.BlockSpec((tile, N_BRANCH * D_MODEL), row),
    ]
    return pl.pallas_call(
        _inproj_kernel,
        grid=(rows // tile,),
        in_specs=[pl.BlockSpec((tile, D_MODEL), row), head_spec, _resident(w_tail.shape),
                  pl.BlockSpec((tile, LANES), row), pl.BlockSpec((tile, LANES), row), _resident(chunk_perm.shape)],
        out_specs=out_specs,
        out_shape=out_shape,
        compiler_params=_cparams(1),
        name="in_projection",
    )(xr, w_stack, w_tail, cos, sin, chunk_perm)


def _time_major_perms(bsz, steps):
    r = jnp.arange(bsz * steps)
    to_tm = (r[None, :] == (r[:, None] % bsz) * steps + r[:, None] // bsz).astype(BF16)
    return to_tm, to_tm.T


def _time_mixers_kernel(us_ref, up_ref, tm_ref, bm_ref, lre_ref, lim_ref, wb_ref, wc_ref, d_ref, wglu_ref,
                        pw_ref, psc_ref, os_ref, op_ref, bu_ref, st_ref, y_ref, ext_ref, yp_ref):
    @pl.when(pl.program_id(0) == 0)
    def _():
        st_ref[...] = jnp.zeros(st_ref.shape, F32)
        ext_ref[0:POOL_HALO * SUBLANES, :] = jnp.zeros((POOL_HALO * SUBLANES, POOL_WIDTH), F32)

    _s5_kernel(us_ref, tm_ref, bm_ref, lre_ref, lim_ref, wb_ref, wc_ref, d_ref, wglu_ref, os_ref, bu_ref, st_ref, y_ref)
    _pool_kernel(up_ref, tm_ref, bm_ref, pw_ref, psc_ref, op_ref, ext_ref, yp_ref)


def _s5_kernel(u_ref, tm_ref, bm_ref, lre_ref, lim_ref, wb_ref, wc_ref, d_ref, wglu_ref, o_ref,
               bu_ref, st_ref, y_ref):
    bsz, n_steps, _ = u_ref.shape
    n_rows = bsz * n_steps
    n_chunk = wb_ref.shape[0]
    cw = S5_LANE_CHUNK

    u = _dot(tm_ref[...], u_ref[...].reshape(n_rows, SSM_WIDTH)).astype(BF16)
    for c in range(n_chunk):
        bu_ref[c] = _dot(u[:, c * LANES:(c + 1) * LANES], wb_ref[c])

    for c in range(n_chunk):
        lre = lre_ref[:, c * cw:(c + 1) * cw]
        lim = lim_ref[:, c * cw:(c + 1) * cw]
        hre = st_ref[0, :, c * cw:(c + 1) * cw]
        him = st_ref[1, :, c * cw:(c + 1) * cw]
        for t in range(n_steps):
            rows = slice(t * SUBLANES, (t + 1) * SUBLANES)
            hre, him = (lre * hre - lim * him + bu_ref[c, rows, 0:cw],
                        lre * him + lim * hre + bu_ref[c, rows, cw:2 * cw])
            bu_ref[c, rows, 0:cw] = hre
            bu_ref[c, rows, cw:2 * cw] = him
        st_ref[0, :, c * cw:(c + 1) * cw] = hre
        st_ref[1, :, c * cw:(c + 1) * cw] = him
        y_ref[:, c * LANES:(c + 1) * LANES] = _dot(bu_ref[c].astype(BF16), wc_ref[c])
    y = y_ref[...] + d_ref[...] * u.astype(F32)
    z = _dot(_gelu_tanh(y).astype(BF16), wglu_ref[...].astype(BF16))
    out = (z[:, :SSM_WIDTH] * jax.nn.sigmoid(z[:, SSM_WIDTH:])).astype(BF16)
    o_ref[...] = _dot(bm_ref[...], out).astype(BF16).reshape(bsz, n_steps, SSM_WIDTH)


def _time_mixers(u_ssm, u_pool, lre, lim, wb, wc, d, wglu_stack, layer, pool_w, pool_sc):
    bsz, seq, _ = u_ssm.shape
    steps = S5_STEPS
    tile = steps * bsz
    state_w = lre.shape[1]
    to_tm, to_bm = _time_major_perms(bsz, steps)
    blk = pl.BlockSpec((bsz, steps, SSM_WIDTH), lambda i: (0, i, 0))
    return pl.pallas_call(
        _time_mixers_kernel,
        grid=(seq // steps,),
        in_specs=[blk, blk, _resident(to_tm.shape), _resident(to_bm.shape),
                  _resident(lre.shape), _resident(lim.shape), _resident(wb.shape), _resident(wc.shape),
                  _resident(d.shape), _resident_layer(wglu_stack, layer),
                  _resident(pool_w.shape), _resident(pool_sc.shape)],
        out_specs=[blk, blk],
        out_shape=[jax.ShapeDtypeStruct(u_ssm.shape, BF16), jax.ShapeDtypeStruct(u_pool.shape, BF16)],
        scratch_shapes=[pltpu.VMEM((wb.shape[0], tile, 2 * S5_LANE_CHUNK), F32),
                        pltpu.VMEM((2, SUBLANES, state_w), F32),
                        pltpu.VMEM((tile, SSM_WIDTH), F32),
                        pltpu.VMEM((tile + POOL_HALO * SUBLANES, POOL_WIDTH), F32),
                        pltpu.VMEM((tile, POOL_WIDTH), BF16)],
        compiler_params=_cparams(1, carried=True),
        name="time_mixers",
    )(u_ssm, u_pool, to_tm, to_bm, lre, lim, wb, wc, d, wglu_stack, pool_w, pool_sc)


def _pool_kernel(u_ref, tm_ref, bm_ref, w_ref, sc_ref, o_ref, ext_ref, y_ref):
    bsz, n_steps, _ = u_ref.shape
    n_rows = bsz * n_steps
    halo = POOL_HALO * SUBLANES
    i = pl.program_id(0)

    ext_ref[halo:halo + n_rows, :] = _dot(tm_ref[...], u_ref[...].reshape(n_rows, POOL_WIDTH))
    row = lax.broadcasted_iota(jnp.int32, (n_rows, POOL_GROUP), 0)
    t = i * (n_rows // SUBLANES) + jnp.right_shift(row, 3)
    for gi, w in enumerate(POOL_WINDOWS):
        c0 = gi * POOL_GROUP
        cur = ext_ref[halo:halo + n_rows, c0:c0 + POOL_GROUP]
        acc = cur
        for k in range(1, w):
            acc = acc + ext_ref[halo - k * SUBLANES:halo - k * SUBLANES + n_rows, c0:c0 + POOL_GROUP]
        cnt = jnp.minimum(t + 1, w).astype(F32)
        pooled = acc / cnt - cur
        mixed = _dot(pooled.astype(BF16), w_ref[gi]) * sc_ref[:, c0:c0 + POOL_GROUP]
        y_ref[:, c0:c0 + POOL_GROUP] = mixed.astype(BF16)
    ext_ref[0:halo, :] = ext_ref[n_rows:n_rows + halo, :]
    o_ref[...] = _dot(bm_ref[...], y_ref[...]).astype(BF16).reshape(bsz, n_steps, POOL_WIDTH)


def _compress_kernel(z_ref, w1_ref, pos_ref, b1_ref, w2_ref, o_ref):
    half = CMP_STRIDE * HEAD_DIM
    _, n_bh, n_chunks, width = z_ref.shape
    z = z_ref[0].reshape(n_bh * n_chunks, width)
    w_top = w1_ref[0, :half, :].astype(BF16)
    w_bot = w1_ref[0, half:, :].astype(BF16)
    bottom_next = pltpu.roll(_dot(z, w_bot), n_bh * n_chunks - 1, 0)
    pos = pos_ref[0].astype(BF16)
    cst = _dot(pos[:, :half], w_top) + _dot(pos[:, half:], w_bot)
    hid = _gelu_tanh(_dot(z, w_top) + bottom_next + cst[0:1, :] + b1_ref[0])
    out = _dot(hid.astype(BF16), w2_ref[0])
    lane = lax.broadcasted_iota(jnp.int32, out.shape, 1)
    is_value = pl.program_id(0) == 1
    out = jnp.where((lane >= HEAD_DIM) & is_value, 1.0, out).astype(BF16)
    o_ref[0] = out.reshape(n_bh, n_chunks, LANES)


def _compress(z, w1_stack, layer, pos, b1, w2dup):
    _, n_bh, n_chunks, width = z.shape
    return pl.pallas_call(
        _compress_kernel,
        grid=(2,),
        in_specs=[pl.BlockSpec((1, n_bh, n_chunks, width), lambda j: (j, 0, 0, 0)),
                  pl.BlockSpec((None, 1) + w1_stack.shape[2:], lambda j: (layer, j, 0, 0)),
                  pl.BlockSpec((1,) + pos.shape[1:], lambda j: (j, 0, 0)),
                  pl.BlockSpec((1,) + b1.shape[1:], lambda j: (j, 0, 0)),
                  pl.BlockSpec((1,) + w2dup.shape[1:], lambda j: (j, 0, 0))],
        out_specs=pl.BlockSpec((1, n_bh, n_chunks, LANES), lambda j: (j, 0, 0, 0)),
        out_shape=jax.ShapeDtypeStruct((2, n_bh, n_chunks, LANES), BF16),
        compiler_params=_cparams(1),
        name="compress_mlp",
    )(z, w1_stack, pos, b1, w2dup)


def _nsa_kernel(q_ref, kc_ref, vc_ref, ks_ref, vs_ref, kw_ref, vw_ref, g_ref,
                win_bias_ref, overlap_ref, expand_ref, pick_ref, o_ref, *scratch, seq):
    tile_idx = pl.program_id(1)
    heads = []
    for h in range(N_KV_HEADS):
        pair_cols = slice(h * 2 * LANES, (h + 1) * 2 * LANES)
        cols = slice(h * LANES, (h + 1) * LANES)
        sel_state, pw_ref, pc_ref = scratch[3 * h:3 * h + 3]
        heads.append(_nsa_head(q_ref.at[:, pair_cols], kc_ref.at[0, h], vc_ref.at[0, h],
                               ks_ref.at[0, :, cols], vs_ref.at[0, :, cols], kw_ref.at[0, :, cols],
                               vw_ref.at[0, :, cols], g_ref.at[:, cols],
                               win_bias_ref, overlap_ref, expand_ref, pick_ref, o_ref.at[:, pair_cols],
                               sel_state, pw_ref, pc_ref, seq=seq, tile_idx=tile_idx))
    streams = [next(head) for head in heads]
    def pair_step(j, stream):
        scores, consume, buf_a, buf_b, _ = stream
        scores(2 * j + 1, buf_b)
        consume(2 * j, buf_a)
        scores(2 * j + 2, buf_a)
        consume(2 * j + 1, buf_b)

    for stream in streams:
        stream[0](0, stream[2])
    for stream in streams:
        pair_step(0, stream)

    for stream in streams:
        def body(j, carry, stream=stream):
            pair_step(j, stream)
            return carry

        lax.fori_loop(1, stream[4], body, 0)
    for head in heads:
        next(head, None)


def _nsa_head(q_ref, kc_ref, vc_ref, ks_ref, vs_ref, kw_ref, vw_ref, g_ref,
              win_bias_ref, overlap_ref, expand_ref, pick_ref, o_ref,
              sel_state, pw_ref, pc_ref, *, seq, tile_idx):
    tq = q_ref.shape[0]
    tk = sel_state[0].shape[1]
    n_sel = seq // SEL_BLOCK
    n_top = min(SEL_TOP, n_sel)
    n_cmp = (seq - CMP_BLOCK) // CMP_STRIDE + 1
    t0 = tile_idx * tq
    head_rows = [slice(g * tq, (g + 1) * tq) for g in range(GQA_GROUP)]

    lane = lax.broadcasted_iota(jnp.int32, (tq, LANES), 1)
    left = lane < HEAD_DIM
    zero = jnp.zeros((tq, LANES), BF16)
    pairs = (q_ref[:, 0:LANES], q_ref[:, LANES:2 * LANES])
    q4 = jnp.concatenate([jnp.where(left, pairs[0], zero), jnp.where(left, zero, pairs[0]),
                          jnp.where(left, pairs[1], zero), jnp.where(left, zero, pairs[1])], axis=0)
    tq_col = t0 + lax.broadcasted_iota(jnp.int32, (tq, 1), 0)

    def online_branch(state, k_ref, v_ref, tile_start, tile_bias):
        buf_a, buf_b, p_ref, m_ref, a_ref, acc_ref = state
        m_ref[...] = jnp.full(m_ref.shape, NEG, F32)
        acc_ref[...] = jnp.zeros(acc_ref.shape, F32)

        def scores(kt, dst_ref):
            bias = tile_bias(kt)
            s = _dot_nt(q4, k_ref[pl.ds(tile_start(kt), tk), :])
            for rows in head_rows:
                dst_ref[rows, :] = s[rows] + bias

        def consume(kt, src_ref):
            for rows in head_rows:
                s = src_ref[rows, :]
                m_old = m_ref[rows, :]
                m_new = jnp.maximum(m_old, jnp.max(s, axis=-1, keepdims=True))
                a_ref[rows, :] = jnp.exp2(m_old - m_new)
                m_ref[rows, :] = m_new
                p_ref[rows, :] = jnp.exp2(s - jnp.concatenate([m_new] * (tk // LANES), axis=1)).astype(BF16)
            acc_ref[...] = a_ref[...] * acc_ref[...] + _dot(p_ref[...], v_ref[pl.ds(tile_start(kt), tk), :])

        return buf_a, buf_b, scores, consume, acc_ref


    wk = pw_ref.shape[1]
    w0 = pl.multiple_of(jnp.maximum(t0 - WINDOW, 0), tq)
    bias_w = win_bias_ref[jnp.minimum(tile_idx, win_bias_ref.shape[0] - 1)]
    s_w = _dot_nt(q4, kw_ref[pl.ds(w0, wk), :])
    for rows in head_rows:
        s = s_w[rows] + bias_w
        pw_ref[rows, :] = jnp.exp2(s - jnp.max(s, axis=-1, keepdims=True)).astype(BF16)
    acc_w = _dot(pw_ref[...], vw_ref[pl.ds(w0, wk), :])

    n_idx = lax.broadcasted_iota(jnp.int32, (1, kc_ref.shape[0]), 1)
    cmp_valid = (n_idx * CMP_STRIDE + (CMP_BLOCK - 1) <= tq_col) & (n_idx < n_cmp)
    s_c = _dot_nt(q4, kc_ref[...])
    p_sum = None
    for rows in head_rows:
        s = jnp.where(cmp_valid, s_c[rows], NEG)
        e = jnp.where(cmp_valid, jnp.exp2(s - jnp.max(s, axis=-1, keepdims=True)), 0.0)
        l = jnp.sum(e, axis=-1, keepdims=True)
        p = e / jnp.where(l > 0.0, l, 1.0)
        p_sum = p if p_sum is None else p_sum + p
        pc_ref[rows, :] = p.astype(BF16)
    o_cmp = _dot(pc_ref[...], vc_ref[...])

    imp = lax.dot_general(overlap_ref[...], p_sum, (((1,), (1,)), ((), ())), precision=lax.Precision.HIGHEST,
                          preferred_element_type=F32)
    jb = lax.broadcasted_iota(jnp.int32, (n_sel, tq), 0)
    tt = t0 + lax.broadcasted_iota(jnp.int32, (n_sel, tq), 1)
    cur = jnp.right_shift(tt, 6)
    forced = (jb == 0) | (jb == cur) | (jb == cur - 1)
    causal = jb * SEL_BLOCK <= tt
    score = jnp.where(forced, 1e30, jnp.where(causal, imp, NEG))
    rank = jnp.zeros((n_sel, tq), F32)
    for a in range(n_sel):
        sa = score[a:a + 1, :]
        ahead = (sa > score) | ((sa == score) & (jb > a))
        rank = rank + jnp.where(ahead, 1.0, 0.0)
    chosen = (rank < float(n_top)) & causal
    sel_bias = jnp.where(chosen, 0.0, NEG).T.astype(BF16)

    def sel_bias_tile(kt):
        k0 = kt * tk
        expand = expand_ref[:, pl.ds(tile_start(kt), tk)]
        bias = _dot(sel_bias, expand)
        return jnp.where(k0 + lax.broadcasted_iota(jnp.int32, (1, tk), 1) <= tq_col, bias, NEG)

    def tile_start(kt):
        return pl.multiple_of(jnp.minimum(kt * tk, seq - tk), tk)

    sa, sb, sel_scores, sel_consume, sel_acc = online_branch(
        sel_state, ks_ref, vs_ref, tile_start, sel_bias_tile)

    n_kt = (t0 + tq + tk - 1) // tk
    yield sel_scores, sel_consume, sa, sb, (n_kt + 1) // 2
    acc_s = sel_acc[...]

    g = g_ref[...]
    g_hi = g.astype(BF16)
    g_lo = (g - g_hi.astype(F32)).astype(BF16)
    gates = _dot(jnp.concatenate([g_hi, g_lo], axis=1), pick_ref[...])

    def pair_tile(acc, pair, normalise):
        a, b = acc[head_rows[2 * pair]], acc[head_rows[2 * pair + 1]]
        num = jnp.where(left, a, pltpu.roll(b, HEAD_DIM, 1))
        return num / jnp.where(left, pltpu.roll(a, HEAD_DIM, 1), b) if normalise else num

    for pair in range(GQA_GROUP // 2):
        out = None
        for br, (acc, normalise) in enumerate(((o_cmp, False), (acc_s, True), (acc_w, True))):
            tile = br * (GQA_GROUP // 2) + pair
            term = gates[:, tile * LANES:(tile + 1) * LANES] * pair_tile(acc, pair, normalise)
            out = term if out is None else out + term
        o_ref[:, pair * LANES:(pair + 1) * LANES] = out.astype(BF16)


def _nsa(q, cmp_kv, ksel, vsel, kwin, vwin, ng, bsz, seq):
    tq = Q_TILE
    n_q = seq // tq
    n_chunks = cmp_kv.shape[2]
    rows = GQA_GROUP * tq
    assert seq % K_TILE == 0
    qrow = lambda b, i: (b * n_q + i, 0)
    kv_spec = pl.BlockSpec((1, seq, N_KV_HEADS * LANES), lambda b, i: (b, 0, 0))
    cmp_kv = cmp_kv.reshape(2, bsz, N_KV_HEADS, n_chunks, LANES)

    def branch_state():
        return (pltpu.VMEM((rows, K_TILE), F32), pltpu.VMEM((rows, K_TILE), F32), pltpu.VMEM((rows, K_TILE), BF16),
                pltpu.VMEM((rows, LANES), F32), pltpu.VMEM((rows, LANES), F32), pltpu.VMEM((rows, LANES), F32))

    wk = WINDOW + tq
    case = jnp.arange(WINDOW // tq + 1)[:, None, None]
    dist = jnp.minimum(case * tq, WINDOW) + jnp.arange(tq)[None, :, None] - jnp.arange(wk)[None, None, :]
    win_bias = jnp.where((dist >= 0) & (dist < WINDOW), 0.0, NEG).astype(F32)
    n_sel = seq // SEL_BLOCK
    n_cmp = (seq - CMP_BLOCK) // CMP_STRIDE + 1
    cn = jnp.arange(n_chunks)[None, :] * CMP_STRIDE
    sj = jnp.arange(n_sel)[:, None] * SEL_BLOCK
    overlap = jnp.clip(jnp.minimum(cn + CMP_BLOCK, sj + SEL_BLOCK) - jnp.maximum(cn, sj), 0, None).astype(F32) / CMP_BLOCK
    overlap = jnp.where(jnp.arange(n_chunks)[None, :] < n_cmp, overlap, 0.0)
    expand = (jnp.arange(n_sel)[:, None] == jnp.arange(seq)[None, :] // SEL_BLOCK).astype(BF16)
    n_tiles = 3 * (GQA_GROUP // 2)
    src = jnp.arange(2 * LANES)[:, None] % LANES
    dst = jnp.arange(n_tiles * LANES)[None, :]
    pick = (src == 2 * (dst // LANES) + (dst // HEAD_DIM) % 2).astype(BF16)
    return pl.pallas_call(
        functools.partial(_nsa_kernel, seq=seq),
        grid=(bsz, n_q),
        in_specs=[pl.BlockSpec((tq, ATTN_WIDTH), qrow),
                  pl.BlockSpec((None, 1, N_KV_HEADS, n_chunks, LANES), lambda b, i: (0, b, 0, 0, 0)),
                  pl.BlockSpec((None, 1, N_KV_HEADS, n_chunks, LANES), lambda b, i: (1, b, 0, 0, 0)),
                  kv_spec, kv_spec, kv_spec, kv_spec,
                  pl.BlockSpec((tq, N_KV_HEADS * LANES), qrow),
                  _resident(win_bias.shape), _resident(overlap.shape), _resident(expand.shape), _resident(pick.shape)],
        out_specs=pl.BlockSpec((tq, ATTN_WIDTH), qrow),
        out_shape=jax.ShapeDtypeStruct((bsz * seq, ATTN_WIDTH), BF16),
        scratch_shapes=[shape for _ in range(N_KV_HEADS) for shape in
                        (branch_state(), pltpu.VMEM((rows, WINDOW + tq), BF16), pltpu.VMEM((rows, n_chunks), BF16))],
        compiler_params=_cparams(2),
        name="nsa_attention",
    )(q, cmp_kv, cmp_kv, ksel, vsel, kwin, vwin, ng, win_bias, overlap, expand, pick)


def _merge_kernel(x_ref, ys_ref, yp_ref, yn_ref, bg_ref, wb_ref, wo_ref, g_ref, b_ref, o_ref, *, alpha):
    merged = None
    for k, y_ref in enumerate((ys_ref, yp_ref, yn_ref)):
        term = bg_ref[:, k * D_MODEL:(k + 1) * D_MODEL].astype(F32) * _dot(y_ref[...], wb_ref[k].astype(BF16))
        merged = term if merged is None else merged + term
    r = alpha * x_ref[...] + _dot(merged.astype(BF16), wo_ref[...].astype(BF16))
    o_ref[...] = _layer_norm(r, g_ref[...], b_ref[...])


def _merge(xr, ys, yp, yn, bg, wb_stack, wo_stack, layer, g, b, alpha):
    rows = xr.shape[0]
    tile = PROJ_TILE
    row = lambda i: (i, 0)
    return pl.pallas_call(
        functools.partial(_merge_kernel, alpha=alpha),
        grid=(rows // tile,),
        in_specs=[pl.BlockSpec((tile, D_MODEL), row),
                  pl.BlockSpec((tile, SSM_WIDTH), row), pl.BlockSpec((tile, POOL_WIDTH), row),
                  pl.BlockSpec((tile, ATTN_WIDTH), row), pl.BlockSpec((tile, N_BRANCH * D_MODEL), row),
                  _resident_layer(wb_stack, layer), _resident_layer(wo_stack, layer),
                  _resident(g.shape), _resident(b.shape)],
        out_specs=pl.BlockSpec((tile, D_MODEL), row),
        out_shape=jax.ShapeDtypeStruct((rows, D_MODEL), F32),
        compiler_params=_cparams(1),
        name="branch_merge",
    )(xr, ys, yp, yn, bg, wb_stack, wo_stack, g, b)


def _ffn_kernel(x_ref, wi_ref, wo_ref, g_ref, b_ref, o_ref, *, alpha):
    x = x_ref[...]
    xb = x.astype(BF16)
    acc = None
    for c in range(FF_HIDDEN // FF_CHUNK):
        c0 = c * FF_CHUNK
        hg = _dot(xb, wi_ref[:, c0:c0 + FF_CHUNK].astype(BF16))
        hu = _dot(xb, wi_ref[:, FF_HIDDEN + c0:FF_HIDDEN + c0 + FF_CHUNK].astype(BF16))
        act = (hg * jax.nn.sigmoid(hg) * hu).astype(BF16)
        part = _dot(act, wo_ref[c0:c0 + FF_CHUNK, :].astype(BF16))
        acc = part if acc is None else acc + part
    o_ref[...] = _layer_norm(alpha * x + acc, g_ref[...], b_ref[...])


def _ffn(xr, wi_stack, wo_stack, layer, g, b, alpha):
    rows = xr.shape[0]
    tile = ROW_TILE
    row = lambda i: (i, 0)
    return pl.pallas_call(
        functools.partial(_ffn_kernel, alpha=alpha),
        grid=(rows // tile,),
        in_specs=[pl.BlockSpec((tile, D_MODEL), row), _resident_layer(wi_stack, layer),
                  _resident_layer(wo_stack, layer), _resident(g.shape), _resident(b.shape)],
        out_specs=pl.BlockSpec((tile, D_MODEL), row),
        out_shape=jax.ShapeDtypeStruct((rows, D_MODEL), F32),
        compiler_params=_cparams(1),
        name="swiglu_ffn",
    )(xr, wi_stack, wo_stack, g, b)


def _pack_s5(a_re, a_im, log_dt, b_re, b_im, c_re, c_im):
    depth = a_re.shape[0]
    a_re, a_im = a_re.astype(F32), a_im.astype(F32)
    dt = jnp.exp(log_dt.astype(F32))[..., None]
    mag = jnp.exp(a_re * dt)
    lbar_re, lbar_im = mag * jnp.cos(a_im * dt), mag * jnp.sin(a_im * dt)
    den = a_re * a_re + a_im * a_im
    coef_re = ((lbar_re - 1.0) * a_re + lbar_im * a_im) / den
    coef_im = (lbar_im * a_re - (lbar_re - 1.0) * a_im) / den
    b_re, b_im = b_re.astype(F32), b_im.astype(F32)
    bbar_re = coef_re[..., None] * b_re - coef_im[..., None] * b_im
    bbar_im = coef_re[..., None] * b_im + coef_im[..., None] * b_re
    gpc = S5_LANE_CHUNK // SSM_STATE
    n_chunk = SSM_GROUPS // gpc
    eye = jnp.eye(gpc, dtype=F32)

    def b_block(part):
        v = part.transpose(0, 1, 3, 2).reshape(depth, n_chunk, gpc, SSM_GROUP, SSM_STATE)
        return jnp.einsum('xy,dqxcp->dqxcyp', eye, v).reshape(depth, n_chunk, gpc * SSM_GROUP, gpc * SSM_STATE)

    def c_block(part):
        v = part.reshape(depth, n_chunk, gpc, SSM_GROUP, SSM_STATE)
        return jnp.einsum('xy,dqxcp->dqxpyc', eye, v).reshape(depth, n_chunk, gpc * SSM_STATE, gpc * SSM_GROUP)

    wb = jnp.concatenate([b_block(bbar_re), b_block(bbar_im)], axis=-1).astype(BF16)
    wc = jnp.concatenate([c_block(c_re.astype(F32)), c_block(-c_im.astype(F32))], axis=-2).astype(BF16)
    state_w = SSM_GROUPS * SSM_STATE
    lre = jnp.broadcast_to(lbar_re.reshape(depth, 1, state_w), (depth, SUBLANES, state_w))
    lim = jnp.broadcast_to(lbar_im.reshape(depth, 1, state_w), (depth, SUBLANES, state_w))
    return lre, lim, wb, wc


def kernel(x, positions, w_in, ssm_a_re, ssm_a_im, ssm_log_dt, ssm_b_re, ssm_b_im, ssm_c_re, ssm_c_im,
           ssm_d, ssm_w_glu, pool_w, pool_scale, cmp_pos, cmp_w1, cmp_b1, cmp_w2,
           w_branch, w_out, ln_g, ln_b, ffn_w_in, ffn_w_out):
    bsz, seq, _ = x.shape
    depth = w_in.shape[0]
    rows = bsz * seq
    assert bsz == SUBLANES and seq % PROJ_TILE == 0 and seq >= WINDOW + Q_TILE
    alpha = (2 * depth) ** 0.25

    assert w_in.shape[-1] == IN_RAW
    lre, lim, s5_wb, s5_wc = _pack_s5(ssm_a_re, ssm_a_im, ssm_log_dt, ssm_b_re, ssm_b_im, ssm_c_re, ssm_c_im)
    s5_d = ssm_d.astype(F32).reshape(depth, 1, SSM_WIDTH)
    pool_wb = pool_w.astype(BF16)
    pool_sc = pool_scale.astype(F32).reshape(depth, 1, POOL_WIDTH)
    half = CMP_STRIDE * HEAD_DIM
    cmp_posr = jnp.broadcast_to(cmp_pos.astype(F32).reshape(depth, 2, 1, CMP_BLOCK * HEAD_DIM),
                                (depth, 2, SUBLANES, CMP_BLOCK * HEAD_DIM))
    cmp_b1r = cmp_b1.astype(F32).reshape(depth, 2, 1, CMP_HIDDEN)
    cmp_w2dup = jnp.concatenate([cmp_w2, cmp_w2 * jnp.array([1.0, 0.0], cmp_w2.dtype).reshape(1, 2, 1, 1)],
                                axis=-1).astype(BF16)
    lng = ln_g.astype(F32).reshape(depth, 2, 1, D_MODEL)
    lnb = ln_b.astype(F32).reshape(depth, 2, 1, D_MODEL)

    def pack_w_tail(w):
        w_gate = w[:, COL_NG:COL_NG + N_GATE].reshape(D_MODEL, 3, N_KV_HEADS, GQA_GROUP)
        w_gate = w_gate.transpose(0, 2, 1, 3).reshape(D_MODEL, N_KV_HEADS, 3 * GQA_GROUP)
        w_gate = jnp.pad(w_gate, ((0, 0), (0, 0), (0, LANES - 3 * GQA_GROUP))).reshape(D_MODEL, N_KV_HEADS * LANES)
        return jnp.concatenate([w_gate, w[:, COL_NG + N_GATE:]], axis=-1).astype(BF16)

    cos, sin = _rope_tables(positions)
    xr = x.astype(F32).reshape(rows, D_MODEL)
    w_in_head = w_in[:, :, :COL_NG]
    for l in range(depth):
        (u_ssm, u_pool, q, cmp_in, ksel, vsel, kwin, vwin, ng, bg) = _inproj(
            xr, w_in_head, l, pack_w_tail(w_in[l]), cos, sin, bsz, seq)
        y_ssm, y_pool = _time_mixers(u_ssm.reshape(bsz, seq, SSM_WIDTH), u_pool.reshape(bsz, seq, POOL_WIDTH),
                                     lre[l], lim[l], s5_wb[l], s5_wc[l], s5_d[l], ssm_w_glu, l, pool_wb[l], pool_sc[l])
        cmp_z = cmp_in.reshape(2, bsz * N_KV_HEADS, seq // CMP_STRIDE, CMP_STRIDE * HEAD_DIM)
        cmp_kv = _compress(cmp_z, cmp_w1, l, cmp_posr[l], cmp_b1r[l], cmp_w2dup[l])
        y_nsa = _nsa(q, cmp_kv, ksel, vsel, kwin, vwin, ng, bsz, seq)
        x1 = _merge(xr, y_ssm.reshape(rows, SSM_WIDTH), y_pool.reshape(rows, POOL_WIDTH), y_nsa, bg,
                    w_branch, w_out, l, lng[l, 0], lnb[l, 0], alpha)
        xr = _ffn(x1, ffn_w_in, ffn_w_out, l, lng[l, 1], lnb[l, 1], alpha)
    return xr.reshape(bsz, seq, D_MODEL).astype(x.dtype)
```
